```python
import math
import jax, jax.numpy as jnp
from jax import lax
import numpy as np

D_MODEL = 1024
BATCH = 2
SEQ = 16384
DEPTH = 2
DEC_BATCH = 16
DEC_SEQ = 2048
PAST_LEN = 128

HEAD_DIM = 64
N_EVEN = (DEPTH + 1) // 2
N_ODD = DEPTH // 2
DA_HEADS = 4
DA_VDIM = 2 * HEAD_DIM
DA_QK = DA_HEADS * 2 * HEAD_DIM
DA_V = DA_HEADS * DA_VDIM
HY_CH = D_MODEL // 2
HY_ORDER = 2
HY_BANDS = 8
HY_EMB = 1 + 2 * HY_BANDS
HY_HIDDEN = 64
HY_FAST_DECAY = 0.3
HY_SLOW_DECAY = 1.5
HY_TARGET = 1e-2
EVEN_IN = 2 * DA_QK + DA_V + 3 * HY_CH
EVEN_OUT = DA_V + HY_CH
NA_HEADS = 8
GRID_W = 64
NA_ROWS_MAX = 8
NA_COLS = 16
NA_W = NA_HEADS * HEAD_DIM
WG_HEADS = 8
WG_KV_HEADS = 2
WG_WINDOW = 128
WG_BLOCK = 128
WG_Q = WG_HEADS * HEAD_DIM
WG_KV = WG_KV_HEADS * HEAD_DIM
ODD_IN = 3 * NA_W + WG_Q + 2 * WG_KV
ODD_OUT = NA_W + WG_Q
T5_BUCKETS = 32
T5_MAX_DIST = 128
T5_HEADS = 8
N_MEM = 256
MEM_HEADS = 4
MEM_W = MEM_HEADS * HEAD_DIM
D_FF = 4 * D_MODEL
Q_BLOCK = 128
EPS = 1e-6
NEG_INF = -1e30

kernel_name = "hybrid_diffattn_hyena_natten_swa_encoder"


def rms_f32(x, g):
    xf = x.astype(jnp.float32)
    return xf * lax.rsqrt(jnp.mean(xf * xf, axis=-1, keepdims=True) + EPS) * g.astype(jnp.float32)


def rmsnorm(x, g):
    return rms_f32(x, g).astype(x.dtype)


def t5_bucket(rel):
    half = T5_BUCKETS // 2
    exact = half // 2
    n = jnp.abs(rel)
    nf = jnp.maximum(n, 1).astype(jnp.float32)
    large = exact + (jnp.log(nf / exact) / math.log(T5_MAX_DIST / exact) * (half - exact)).astype(jnp.int32)
    large = jnp.minimum(large, half - 1)
    return jnp.where(rel > 0, half, 0) + jnp.where(n < exact, n, large)


def lambda_init(layer):
    return 0.8 - 0.6 * math.exp(-0.3 * layer)


def diff_attention(q, k, v, lam, q_gain, k_gain, sub_gain, t5_table, lam_init):
    B, L = q.shape[:2]
    nb = L // Q_BLOCK
    q = rms_f32(q, q_gain) * (HEAD_DIM ** -0.5)
    k = rms_f32(k, k_gain)
    v = v.astype(jnp.float32)
    lf = lam.astype(jnp.float32)
    lam_val = jnp.exp(jnp.sum(lf[0] * lf[1])) - jnp.exp(jnp.sum(lf[2] * lf[3])) + lam_init
    qb = jnp.moveaxis(q.reshape(B, nb, Q_BLOCK, DA_HEADS, 2, HEAD_DIM), 1, 0)
    kpos = jnp.arange(L, dtype=jnp.int32)
    table = t5_table.astype(jnp.float32)

    def block(args):
        i, qi = args
        qpos = i * Q_BLOCK + jnp.arange(Q_BLOCK, dtype=jnp.int32)
        bias = table[t5_bucket(kpos[None, :] - qpos[:, None])]
        bias = jnp.transpose(bias, (2, 0, 1)).reshape(DA_HEADS, 2, Q_BLOCK, L)
        s = jnp.einsum('bqhmd,bkhmd->bhmqk', qi, k) + bias[None]
        p = jax.nn.softmax(s, axis=-1)
        a = p[:, :, 0] - lam_val * p[:, :, 1]
        return jnp.einsum('bhqk,bkhd->bqhd', a, v)

    o = lax.map(block, (jnp.arange(nb, dtype=jnp.int32), qb))
    o = jnp.moveaxis(o, 0, 1).reshape(B, L, DA_HEADS, DA_VDIM)
    o = rms_f32(o, sub_gain) * (1.0 - lam_init)
    return o.reshape(B, L, DA_V)


def hyena_filters(L, w1, b1, freq, w2, b2, w3):
    t_idx = jnp.arange(L, dtype=jnp.float32)
    t01 = t_idx / max(L - 1, 1)
    w = 2.0 * math.pi * t_idx / L
    f = jnp.linspace(1e-4, HY_BANDS - 1, HY_BANDS, dtype=jnp.float32)
    ang = w[:, None] * f[None, :]
    z = jnp.concatenate([t01[:, None], jnp.cos(ang), -jnp.sin(ang)], axis=-1)
    fr = freq.astype(jnp.float32)
    h = jnp.sin(fr[0] * (z @ w1.astype(jnp.float32) + b1.astype(jnp.float32)))
    h = jnp.sin(fr[1] * (h @ w2.astype(jnp.float32) + b2.astype(jnp.float32)))
    h = (h @ w3.astype(jnp.float32)).reshape(L, HY_ORDER, 2, HY_CH)
    min_decay = math.log(HY_TARGET) / HY_SLOW_DECAY
    max_decay = math.log(HY_TARGET) / HY_FAST_DECAY
    deltas = jnp.abs(jnp.linspace(min_decay, max_decay, HY_CH, dtype=jnp.float32))
    h = h * jnp.exp(-t01[:, None] * deltas[None, :])[:, None, None, :]
    fwd, bwd = h[:, :, 0], h[:, :, 1]
    k2 = jnp.concatenate([fwd, jnp.zeros((1, HY_ORDER, HY_CH), jnp.float32), bwd[1:][::-1]], axis=0)
    k2 = k2 / jnp.sum(jnp.abs(k2), axis=0, keepdims=True)
    return jnp.fft.rfft(k2, n=2 * L, axis=0)


def hyena_mixer(u, conv_w, conv_b, skip, kf):
    L = u.shape[1]
    up = jnp.pad(u, ((0, 0), (1, 1), (0, 0)))
    uc = up[:, :-2] * conv_w[0] + up[:, 1:-1] * conv_w[1] + up[:, 2:] * conv_w[2] + conv_b
    v, x1, x2 = jnp.split(uc.astype(jnp.float32), 3, axis=-1)
    sk = skip.astype(jnp.float32)
    z = v
    for o, gate in enumerate((x1, x2)):
        zc = jnp.fft.irfft(jnp.fft.rfft(z, n=2 * L, axis=1) * kf[:, o][None], n=2 * L, axis=1)[:, :L]
        z = gate * (zc + sk[o] * z)
    return z


def neighbourhood_attention(q, k, v, q_gain, k_gain, rpb):
    B, L = q.shape[:2]
    rows = L // GRID_W
    kr = min(NA_ROWS_MAX, rows)
    q = rms_f32(q, q_gain) * (HEAD_DIM ** -0.5)
    k = rms_f32(k, k_gain).reshape(B, rows, GRID_W, NA_HEADS, HEAD_DIM)
    v = v.astype(jnp.float32).reshape(B, rows, GRID_W, NA_HEADS, HEAD_DIM)
    qr = jnp.moveaxis(q.reshape(B, rows, GRID_W, NA_HEADS, HEAD_DIM), 1, 0)
    cols = jnp.arange(GRID_W, dtype=jnp.int32)
    c_start = jnp.clip(cols - NA_COLS // 2, 0, GRID_W - NA_COLS)
    idx_c = c_start[:, None] + jnp.arange(NA_COLS, dtype=jnp.int32)[None, :]
    dc = idx_c - cols[:, None] + (NA_COLS - 1)
    rpb_f = rpb.astype(jnp.float32)

    def row(args):
        r, qi = args
        r_start = jnp.clip(r - kr // 2, 0, rows - kr)
        kg = lax.dynamic_slice_in_dim(k, r_start, kr, axis=1)[:, :, idx_c]
        vg = lax.dynamic_slice_in_dim(v, r_start, kr, axis=1)[:, :, idx_c]
        dr = r_start + jnp.arange(kr, dtype=jnp.int32) - r + (NA_ROWS_MAX - 1)
        bias = rpb_f[:, dr[None, :, None], dc[:, None, :]]
        s = jnp.einsum('bchd,bicjhd->bhcij', qi, kg) + bias[None]
        p = jax.nn.softmax(s.reshape(B, NA_HEADS, GRID_W, kr * NA_COLS), axis=-1).reshape(s.shape)
        return jnp.einsum('bhcij,bicjhd->bchd', p, vg)

    o = lax.map(row, (jnp.arange(rows, dtype=jnp.int32), qr))
    return jnp.moveaxis(o, 0, 1).reshape(B, L, NA_W)


def window_gqa(q, k, v, q_gain, k_gain, sink, t5_table):
    B, L = q.shape[:2]
    nb = L // WG_BLOCK
    grp = WG_HEADS // WG_KV_HEADS
    q = (rms_f32(q, q_gain) * (HEAD_DIM ** -0.5)).reshape(B, nb, WG_BLOCK, WG_KV_HEADS, grp, HEAD_DIM)
    k = rms_f32(k, k_gain)
    v = v.astype(jnp.float32)

    def band(t):
        tp = jnp.pad(t, ((0, 0), (WG_BLOCK, WG_BLOCK), (0, 0), (0, 0)))
        tp = tp.reshape(B, nb + 2, WG_BLOCK, WG_KV_HEADS, HEAD_DIM)
        return jnp.concatenate([tp[:, :-2], tp[:, 1:-1], tp[:, 2:]], axis=2)

    kb, vb = band(k), band(v)
    qi = jnp.arange(WG_BLOCK, dtype=jnp.int32)
    ki = jnp.arange(3 * WG_BLOCK, dtype=jnp.int32)
    rel = ki[None, :] - WG_BLOCK - qi[:, None]
    bias = t5_table.astype(jnp.float32)[t5_bucket(rel)]
    bias = jnp.transpose(bias, (2, 0, 1)).reshape(WG_KV_HEADS, grp, WG_BLOCK, 3 * WG_BLOCK)
    kpos = jnp.arange(nb, dtype=jnp.int32)[:, None] * WG_BLOCK - WG_BLOCK + ki[None, :]
    valid = (jnp.abs(rel) <= WG_WINDOW)[None] & ((kpos >= 0) & (kpos < L))[:, None, :]
    s = jnp.einsum('bnqgrd,bnkgd->bngrqk', q, kb) + bias[None, None]
    s = jnp.where(valid[None, :, None, None], s, NEG_INF)
    sink_l = jnp.broadcast_to(sink.astype(jnp.float32).reshape(WG_KV_HEADS, grp, 1, 1), s.shape[:-1] + (1,))
    p = jax.nn.softmax(jnp.concatenate([s, sink_l], axis=-1), axis=-1)[..., :-1]
    o = jnp.einsum('bngrqk,bnkgd->bnqgrd', p, vb)
    return o.reshape(B, L, WG_Q)


def memory_attention(x, mem, norm_g, mem_norm_g, wq, wkv, wo, qg, kg):
    B, L, _ = x.shape
    M = mem.shape[1]
    h = rmsnorm(x, norm_g)
    m = rmsnorm(mem, mem_norm_g)
    q = rms_f32((h @ wq).reshape(B, L, MEM_HEADS, HEAD_DIM), qg) * (HEAD_DIM ** -0.5)
    kv = (m @ wkv).reshape(B, M, 2, MEM_HEADS, HEAD_DIM)
    k = rms_f32(kv[:, :, 0], kg)
    v = kv[:, :, 1].astype(jnp.float32)
    p = jax.nn.softmax(jnp.einsum('blhd,bmhd->bhlm', q, k), axis=-1)
    o = jnp.einsum('bhlm,bmhd->blhd', p, v).reshape(B, L, MEM_W)
    return o.astype(x.dtype) @ wo


def sq_relu_mlp(x, g, w1, w2):
    h = rmsnorm(x, g) @ w1
    return jnp.square(jax.nn.relu(h)) @ w2


def trunk(x, mem, t5_table, norm_mix, norm_mem, norm_memkv, norm_mlp,
          w_in_even, da_q_gain, da_k_gain, da_lambda, da_sub_gain,
          hy_conv_w, hy_conv_b, hy_w1, hy_b1, hy_freq, hy_w2, hy_b2, hy_w3, hy_skip, w_out_even,
          w_in_odd, na_q_gain, na_k_gain, na_rpb, wg_q_gain, wg_k_gain, wg_sink, w_out_odd,
          mem_wq, mem_wkv, mem_wo, mem_q_gain, mem_k_gain, mlp_w1, mlp_w2):
    B, L, _ = x.shape
    for layer in range(DEPTH):
        h = rmsnorm(x, norm_mix[layer])
        if layer % 2 == 0:
            j = layer // 2
            proj = h @ w_in_even[j]
            qa, ka, va, ub = jnp.split(proj, [DA_QK, 2 * DA_QK, 2 * DA_QK + DA_V], axis=-1)
            oa = diff_attention(qa.reshape(B, L, DA_HEADS, 2, HEAD_DIM),
                                ka.reshape(B, L, DA_HEADS, 2, HEAD_DIM),
                                va.reshape(B, L, DA_HEADS, DA_VDIM),
                                da_lambda[j], da_q_gain[j], da_k_gain[j], da_sub_gain[j],
                                t5_table, lambda_init(layer))
            kf = hyena_filters(L, hy_w1[j], hy_b1[j], hy_freq[j], hy_w2[j], hy_b2[j], hy_w3[j])
            ob = hyena_mixer(ub, hy_conv_w[j], hy_conv_b[j], hy_skip[j], kf)
            mix = jnp.concatenate([oa, ob], axis=-1).astype(x.dtype) @ w_out_even[j]
        else:
            j = layer // 2
            proj = h @ w_in_odd[j]
            qc, kc, vc, qd, kd, vd = jnp.split(
                proj, [NA_W, 2 * NA_W, 3 * NA_W, 3 * NA_W + WG_Q, 3 * NA_W + WG_Q + WG_KV], axis=-1)
            oc = neighbourhood_attention(qc.reshape(B, L, NA_HEADS, HEAD_DIM),
                                         kc.reshape(B, L, NA_HEADS, HEAD_DIM),
                                         vc.reshape(B, L, NA_HEADS, HEAD_DIM),
                                         na_q_gain[j], na_k_gain[j], na_rpb[j])
            od = window_gqa(qd.reshape(B, L, WG_HEADS, HEAD_DIM),
                            kd.reshape(B, L, WG_KV_HEADS, HEAD_DIM),
                            vd.reshape(B, L, WG_KV_HEADS, HEAD_DIM),
                            wg_q_gain[j], wg_k_gain[j], wg_sink[j], t5_table)
            mix = jnp.concatenate([oc, od], axis=-1).astype(x.dtype) @ w_out_odd[j]
        x = x + mix.astype(x.dtype)
        x = x + memory_attention(x, mem, norm_mem[layer], norm_memkv[layer], mem_wq[layer],
                                 mem_wkv[layer], mem_wo[layer], mem_q_gain[layer],
                                 mem_k_gain[layer]).astype(x.dtype)
        x = x + sq_relu_mlp(x, norm_mlp[layer], mlp_w1[layer], mlp_w2[layer]).astype(x.dtype)
    return x


def setup_inputs(seed: int = 0) -> dict:
    key = jax.random.key(seed)
    ks = iter(jax.random.split(key, 40))

    def nrm(shape, scale):
        return jax.random.normal(next(ks), shape, jnp.float32) * scale

    def gain(shape):
        return 1.0 + nrm(shape, 0.05)

    return {
        "x_prompt": nrm((BATCH, SEQ, D_MODEL), 1.0),
        "x_sample": nrm((DEC_BATCH, DEC_SEQ, D_MODEL), 1.0),
        "mem_prompt": nrm((BATCH, N_MEM, D_MODEL), 1.0),
        "mem_sample": nrm((DEC_BATCH, N_MEM, D_MODEL), 1.0),
        "t5_table": nrm((T5_BUCKETS, T5_HEADS), 0.2),
        "norm_mix": gain((DEPTH, D_MODEL)),
        "norm_mem": gain((DEPTH, D_MODEL)),
        "norm_memkv": gain((DEPTH, D_MODEL)),
        "norm_mlp": gain((DEPTH, D_MODEL)),
        "w_in_even": nrm((N_EVEN, D_MODEL, EVEN_IN), D_MODEL ** -0.5),
        "da_q_gain": gain((N_EVEN, HEAD_DIM)),
        "da_k_gain": gain((N_EVEN, HEAD_DIM)),
        "da_lambda": nrm((N_EVEN, 4, HEAD_DIM), 0.1),
        "da_sub_gain": gain((N_EVEN, DA_VDIM)),
        "hy_conv_w": nrm((N_EVEN, 3, 3 * HY_CH), 3 ** -0.5),
        "hy_conv_b": nrm((N_EVEN, 3 * HY_CH), 0.02),
        "hy_w1": nrm((N_EVEN, HY_EMB, HY_HIDDEN), HY_EMB ** -0.5),
        "hy_b1": nrm((N_EVEN, HY_HIDDEN), 0.02),
        "hy_freq": gain((N_EVEN, 2, HY_HIDDEN)),
        "hy_w2": nrm((N_EVEN, HY_HIDDEN, HY_HIDDEN), HY_HIDDEN ** -0.5),
        "hy_b2": nrm((N_EVEN, HY_HIDDEN), 0.02),
        "hy_w3": nrm((N_EVEN, HY_HIDDEN, HY_ORDER * 2 * HY_CH), HY_HIDDEN ** -0.5),
        "hy_skip": nrm((N_EVEN, HY_ORDER, HY_CH), 1.0),
        "w_out_even": nrm((N_EVEN, EVEN_OUT, D_MODEL), EVEN_OUT ** -0.5),
        "w_in_odd": nrm((N_ODD, D_MODEL, ODD_IN), D_MODEL ** -0.5),
        "na_q_gain": gain((N_ODD, HEAD_DIM)),
        "na_k_gain": gain((N_ODD, HEAD_DIM)),
        "na_rpb": nrm((N_ODD, NA_HEADS, 2 * NA_ROWS_MAX - 1, 2 * NA_COLS - 1), 0.2),
        "wg_q_gain": gain((N_ODD, HEAD_DIM)),
        "wg_k_gain": gain((N_ODD, HEAD_DIM)),
        "wg_sink": nrm((N_ODD, WG_HEADS), 0.5),
        "w_out_odd": nrm((N_ODD, ODD_OUT, D_MODEL), ODD_OUT ** -0.5),
        "mem_wq": nrm((DEPTH, D_MODEL, MEM_W), D_MODEL ** -0.5),
        "mem_wkv": nrm((DEPTH, D_MODEL, 2 * MEM_W), D_MODEL ** -0.5),
        "mem_wo": nrm((DEPTH, MEM_W, D_MODEL), MEM_W ** -0.5),
        "mem_q_gain": gain((DEPTH, HEAD_DIM)),
        "mem_k_gain": gain((DEPTH, HEAD_DIM)),
        "mlp_w1": nrm((DEPTH, D_MODEL, D_FF), D_MODEL ** -0.5),
        "mlp_w2": nrm((DEPTH, D_FF, D_MODEL), D_FF ** -0.5),
    }


def reference(x_prompt, x_sample, mem_prompt, mem_sample, t5_table, norm_mix, norm_mem, norm_memkv,
              norm_mlp, w_in_even, da_q_gain, da_k_gain, da_lambda, da_sub_gain, hy_conv_w, hy_conv_b,
              hy_w1, hy_b1, hy_freq, hy_w2, hy_b2, hy_w3, hy_skip, w_out_even, w_in_odd, na_q_gain,
              na_k_gain, na_rpb, wg_q_gain, wg_k_gain, wg_sink, w_out_odd, mem_wq, mem_wkv, mem_wo,
              mem_q_gain, mem_k_gain, mlp_w1, mlp_w2):
    weights = (t5_table, norm_mix, norm_mem, norm_memkv, norm_mlp,
               w_in_even, da_q_gain, da_k_gain, da_lambda, da_sub_gain,
               hy_conv_w, hy_conv_b, hy_w1, hy_b1, hy_freq, hy_w2, hy_b2, hy_w3, hy_skip, w_out_even,
               w_in_odd, na_q_gain, na_k_gain, na_rpb, wg_q_gain, wg_k_gain, wg_sink, w_out_odd,
               mem_wq, mem_wkv, mem_wo, mem_q_gain, mem_k_gain, mlp_w1, mlp_w2)
    y_prompt = trunk(x_prompt, mem_prompt, *weights)
    y_sample = trunk(x_sample, mem_sample, *weights)
    return (y_prompt, y_sample)
```

```python
import functools
import math

import numpy as np
import jax
import jax.numpy as jnp
from jax import lax
from jax.experimental import pallas as pl
from jax.experimental.pallas import tpu as pltpu

F32 = jnp.float32
BF16 = jnp.bfloat16

D_MODEL = 1024
DEPTH = 2
HEAD_DIM = 64
DA_HEADS = 4
DA_VDIM = 2 * HEAD_DIM
DA_QK = DA_HEADS * 2 * HEAD_DIM
DA_V = DA_HEADS * DA_VDIM
HY_CH = D_MODEL // 2
HY_ORDER = 2
HY_BANDS = 8
HY_HIDDEN = 64
HY_FAST_DECAY = 0.3
HY_SLOW_DECAY = 1.5
HY_TARGET = 1e-2
NA_HEADS = 8
GRID_W = 64
NA_ROWS = 8
NA_COLS = 16
NA_W = NA_HEADS * HEAD_DIM
WG_HEADS = 8
WG_KV_HEADS = 2
WG_WINDOW = 128
WG_BLOCK = 128
WG_Q = WG_HEADS * HEAD_DIM
WG_KV = WG_KV_HEADS * HEAD_DIM
T5_BUCKETS = 32
T5_MAX_DIST = 128
T5_HEADS = 8
MEM_HEADS = 4
MEM_W = MEM_HEADS * HEAD_DIM
D_FF = 4 * D_MODEL
EPS = 1e-6
NEG_INF = -1e30

LANES = 128
FFT_LANES = 256
VMEM_LIMIT = 56 * 1024 * 1024


def _cparams(sem, vmem=None):
    return pltpu.CompilerParams(dimension_semantics=sem, vmem_limit_bytes=vmem or VMEM_LIMIT)


def _full(shape):
    n = len(shape)
    return pl.BlockSpec(shape, lambda *_: (0,) * n)


def _dot(a, b):
    return jnp.dot(a, b, preferred_element_type=F32)


def _dot_nt(a, b):
    return lax.dot_general(a, b, (((1,), (1,)), ((), ())), preferred_element_type=F32)


def _group_gmat():
    g = np.arange(LANES) // HEAD_DIM
    return jnp.asarray((g[:, None] == g[None, :]).astype(np.float32), dtype=BF16)


def _rms_rows(x, g):
    ms = jnp.mean(x * x, axis=-1, keepdims=True)
    return x * lax.rsqrt(ms + EPS) * g


def _group_rms(y, gmat, gain):
    outs = []
    for c in range(y.shape[1] // LANES):
        yc = y[:, c * LANES:(c + 1) * LANES]
        ss = _dot((yc * yc).astype(BF16), gmat)
        outs.append(yc * lax.rsqrt(ss * (1.0 / HEAD_DIM) + EPS) * gain[:, c * LANES:(c + 1) * LANES])
    return outs


def _inproj_kernel(*refs, n_tr):
    it = iter(refs)
    x_ref, g_ref = next(it), next(it)
    wgn_ref, gain_ref, gmat_ref = next(it), next(it), next(it)
    wpl_ref = next(it)
    wtr_ref = next(it) if n_tr else None
    ogn_ref, opl_ref = next(it), next(it)
    otr_ref = next(it) if n_tr else None

    hn = _rms_rows(x_ref[...], g_ref[...]).astype(BF16)
    y = _dot(hn, wgn_ref[...])
    gmat = gmat_ref[...]
    gain = gain_ref[...]
    for c, yc in enumerate(_group_rms(y, gmat, gain)):
        ogn_ref[:, c * LANES:(c + 1) * LANES] = yc.astype(ogn_ref.dtype)
    opl_ref[...] = _dot(hn, wpl_ref[...]).astype(opl_ref.dtype)
    if n_tr:
        otr_ref[0] = _dot_nt(wtr_ref[...], hn)


def _inproj(x2d, g, w_gn, gain, w_pl, w_tr=None, batch=None, tm=512):
    M, D = x2d.shape
    n_gn, n_pl = w_gn.shape[1], w_pl.shape[1]
    n_tr = 0 if w_tr is None else w_tr.shape[0]
    in_specs = [pl.BlockSpec((tm, D), lambda i: (i, 0)), _full((1, D)),
                _full((D, n_gn)), _full((1, n_gn)), _full((LANES, LANES)), _full((D, n_pl))]
    args = [x2d, g.reshape(1, D), w_gn, gain.reshape(1, n_gn), _group_gmat(), w_pl]
    out_shape = [jax.ShapeDtypeStruct((M, n_gn), BF16), jax.ShapeDtypeStruct((M, n_pl), BF16)]
    out_specs = [pl.BlockSpec((tm, n_gn), lambda i: (i, 0)), pl.BlockSpec((tm, n_pl), lambda i: (i, 0))]
    if n_tr:
        L = M // batch
        nt = L // tm
        in_specs.append(_full((n_tr, D)))
        args.append(w_tr)
        out_shape.append(jax.ShapeDtypeStruct((batch, n_tr, L), F32))
        out_specs.append(pl.BlockSpec((1, n_tr, tm), lambda i: (i // nt, 0, i % nt)))
    return pl.pallas_call(
        functools.partial(_inproj_kernel, n_tr=n_tr), name="inproj",
        grid=(M // tm,), in_specs=in_specs, out_specs=out_specs, out_shape=out_shape,
        compiler_params=_cparams(("parallel",)))(*args)


def _outproj_kernel(x_ref, a_ref, b_ref, wa_ref, wb_ref, o_ref, *, b_cm):
    b = b_ref[0].T.astype(BF16) if b_cm else b_ref[...]
    o_ref[...] = x_ref[...] + _dot(a_ref[...], wa_ref[...]) + _dot(b, wb_ref[...])


def _outproj(x2d, a, b, wa, wb, *, b_cm, batch=None, tm=512):
    M, D = x2d.shape
    if b_cm:
        L = M // batch
        nt = L // tm
        b_spec = pl.BlockSpec((1, b.shape[1], tm), lambda i: (i // nt, 0, i % nt))
    else:
        b_spec = pl.BlockSpec((tm, b.shape[1]), lambda i: (i, 0))
    return pl.pallas_call(
        functools.partial(_outproj_kernel, b_cm=b_cm), name="outproj",
        grid=(M // tm,),
        in_specs=[pl.BlockSpec((tm, D), lambda i: (i, 0)), pl.BlockSpec((tm, a.shape[1]), lambda i: (i, 0)),
                  b_spec, _full(wa.shape), _full(wb.shape)],
        out_specs=pl.BlockSpec((tm, D), lambda i: (i, 0)),
        out_shape=jax.ShapeDtypeStruct((M, D), F32),
        compiler_params=_cparams(("parallel",)))(x2d, a, b, wa, wb)


def _mlp_kernel(x_ref, g_ref, w1_ref, w2_ref, o_ref, hn_ref):
    j = pl.program_id(1)

    @pl.when(j == 0)
    def _():
        x = x_ref[...]
        hn_ref[...] = _rms_rows(x, g_ref[...]).astype(BF16)
        o_ref[...] = x

    h = _dot(hn_ref[...], w1_ref[...])
    h = jnp.square(jnp.maximum(h, 0.0))
    o_ref[...] += _dot(h.astype(BF16), w2_ref[...])


def _mlp(x2d, g, w1, w2, tm=1024, tf=512):
    M, D = x2d.shape
    F = w1.shape[1]
    tm = min(tm, M)
    return pl.pallas_call(
        _mlp_kernel, name="mlp",
        grid=(M // tm, F // tf),
        in_specs=[pl.BlockSpec((tm, D), lambda i, j: (i, 0)), _full((1, D)),
                  pl.BlockSpec((D, tf), lambda i, j: (0, j)), pl.BlockSpec((tf, D), lambda i, j: (j, 0))],
        out_specs=pl.BlockSpec((tm, D), lambda i, j: (i, 0)),
        out_shape=jax.ShapeDtypeStruct((M, D), F32),
        scratch_shapes=[pltpu.VMEM((tm, D), BF16)],
        compiler_params=_cparams(("parallel", "arbitrary")))(x2d, g.reshape(1, D), w1, w2)


def _memkv_kernel(m_ref, g_ref, w_ref, gain_ref, gmat_ref, k_ref, v_ref):
    mn = _rms_rows(m_ref[0], g_ref[...]).astype(BF16)
    kv = _dot(mn, w_ref[...])
    for c, kc in enumerate(_group_rms(kv[:, :MEM_W], gmat_ref[...], gain_ref[...])):
        k_ref[0, :, c * LANES:(c + 1) * LANES] = kc.astype(BF16)
    v_ref[0] = kv[:, MEM_W:].astype(BF16)


def _memkv(mem, g, wkv, kgain):
    B, M, D = mem.shape
    return pl.pallas_call(
        _memkv_kernel, name="memkv", grid=(B,),
        in_specs=[pl.BlockSpec((1, M, D), lambda b: (b, 0, 0)), _full((1, D)), _full(wkv.shape),
                  _full((1, MEM_W)), _full((LANES, LANES))],
        out_specs=[pl.BlockSpec((1, M, MEM_W), lambda b: (b, 0, 0))] * 2,
        out_shape=[jax.ShapeDtypeStruct((B, M, MEM_W), BF16)] * 2,
        compiler_params=_cparams(("parallel",)))(mem, g.reshape(1, D), wkv, kgain.reshape(1, MEM_W), _group_gmat())


def _memattn_kernel(x_ref, g_ref, wq_ref, qgain_ref, gmat_ref, k_ref, v_ref, wo_ref, o_ref, oh_ref):
    x = x_ref[0]
    hn = _rms_rows(x, g_ref[...]).astype(BF16)
    q = _dot(hn, wq_ref[...])
    qn = jnp.concatenate(_group_rms(q, gmat_ref[...], qgain_ref[...]), axis=1).astype(BF16)
    k = k_ref[0]
    v = v_ref[0]
    for h in range(MEM_HEADS):
        sl = slice(h * HEAD_DIM, (h + 1) * HEAD_DIM)
        s = _dot_nt(qn[:, sl], k[:, sl])
        e = jnp.exp(s - jnp.max(s, axis=-1, keepdims=True))
        p = e / jnp.sum(e, axis=-1, keepdims=True)
        oh_ref[:, sl] = _dot(p.astype(BF16), v[:, sl])
    o_ref[0] = x + _dot(oh_ref[...].astype(BF16), wo_ref[...])


def _memattn(x, kn, vn, g, wq, qgain, wo, tm=512):
    B, L, D = x.shape
    M = kn.shape[1]
    return pl.pallas_call(
        _memattn_kernel, name="memattn", grid=(B, L // tm),
        in_specs=[pl.BlockSpec((1, tm, D), lambda b, i: (b, i, 0)), _full((1, D)), _full(wq.shape),
                  _full((1, MEM_W)), _full((LANES, LANES)),
                  pl.BlockSpec((1, M, MEM_W), lambda b, i: (b, 0, 0)),
                  pl.BlockSpec((1, M, MEM_W), lambda b, i: (b, 0, 0)), _full(wo.shape)],
        out_specs=pl.BlockSpec((1, tm, D), lambda b, i: (b, i, 0)),
        out_shape=jax.ShapeDtypeStruct((B, L, D), F32),
        scratch_shapes=[pltpu.VMEM((tm, MEM_W), F32)],
        compiler_params=_cparams(("parallel", "parallel")))(
            x, g.reshape(1, D), wq, qgain.reshape(1, MEM_W), _group_gmat(), kn, vn, wo)


def _t5_bucket_np(rel):
    half = T5_BUCKETS // 2
    exact = half // 2
    n = np.abs(rel)
    nf = np.maximum(n, 1).astype(np.float64)
    large = exact + (np.log(nf / exact) / math.log(T5_MAX_DIST / exact) * (half - exact)).astype(np.int32)
    large = np.minimum(large, half - 1)
    return (np.where(rel > 0, half, 0) + np.where(n < exact, n, large)).astype(np.int32)


def _t5bias_kernel(table_ref, bucket_ref, mask_ref, o_ref):
    h = pl.program_id(0)
    bucket = bucket_ref[...]
    acc = jnp.zeros(bucket.shape, F32)
    for b in range(T5_BUCKETS):
        acc = jnp.where(bucket == b, table_ref[b, h], acc)
    o_ref[0] = acc + mask_ref[...]


def _t5bias(table, rel, mask=None):
    bucket = jnp.asarray(_t5_bucket_np(rel))
    mask = jnp.zeros(rel.shape, F32) if mask is None else jnp.asarray(mask, F32)
    nd = rel.ndim
    return pl.pallas_call(
        _t5bias_kernel, name="t5bias", grid=(T5_HEADS,),
        in_specs=[pl.BlockSpec(memory_space=pltpu.SMEM), _full(rel.shape), _full(rel.shape)],
        out_specs=pl.BlockSpec((1,) + rel.shape, lambda h: (h,) + (0,) * nd),
        out_shape=jax.ShapeDtypeStruct((T5_HEADS,) + rel.shape, F32),
        compiler_params=_cparams(("arbitrary",)))(table, bucket, mask)


def _diffattn_kernel(lam_ref, q_ref, k_ref, v_ref, bias_ref, sg_ref, o_ref, m_ref, l_ref, acc_ref, *, lam_init):
    j = pl.program_id(3)

    @pl.when(j == 0)
    def _():
        m_ref[...] = jnp.full(m_ref.shape, NEG_INF, F32)
        l_ref[...] = jnp.zeros(l_ref.shape, F32)
        acc_ref[...] = jnp.zeros(acc_ref.shape, F32)

    q = q_ref[0]
    k = k_ref[0]
    v = v_ref[0]
    for m in range(2):
        sl = slice(m * HEAD_DIM, (m + 1) * HEAD_DIM)
        s = _dot_nt(q[:, sl], k[:, sl]) + bias_ref[m, 0]
        m_prev = m_ref[m]
        m_new = jnp.maximum(m_prev, jnp.max(s, axis=-1, keepdims=True))
        alpha = jnp.exp(m_prev - m_new)
        p = jnp.exp(s - m_new)
        l_ref[m] = alpha * l_ref[m] + jnp.sum(p, axis=-1, keepdims=True)
        acc_ref[m] = alpha * acc_ref[m] + _dot(p.astype(BF16), v)
        m_ref[m] = m_new

    @pl.when(j == pl.num_programs(3) - 1)
    def _():
        lf = lam_ref[...]
        lam = (jnp.exp(jnp.sum(lf[0:1] * lf[1:2], axis=-1, keepdims=True))
               - jnp.exp(jnp.sum(lf[2:3] * lf[3:4], axis=-1, keepdims=True)) + lam_init)
        o = acc_ref[0] / l_ref[0] - lam * (acc_ref[1] / l_ref[1])
        o = _rms_rows(o, sg_ref[...]) * (1.0 - lam_init)
        o_ref[0] = o.astype(o_ref.dtype)


def _diffattn(qk, v, bias, lam, sub_gain, lam_init, T):
    B, L, _ = qk.shape
    nb = L // T
    return pl.pallas_call(
        functools.partial(_diffattn_kernel, lam_init=lam_init), name="diffattn",
        grid=(B, DA_HEADS, nb, nb),
        in_specs=[_full((4, HEAD_DIM)),
                  pl.BlockSpec((1, T, LANES), lambda b, h, i, j: (b, i, h)),
                  pl.BlockSpec((1, T, LANES), lambda b, h, i, j: (b, j, DA_HEADS + h)),
                  pl.BlockSpec((1, T, LANES), lambda b, h, i, j: (b, j, h)),
                  pl.BlockSpec((2, 1, T, T), lambda b, h, i, j: (h, jnp.clip(j - i + 1, 0, 2), 0, 0)),
                  _full((1, DA_VDIM))],
        out_specs=pl.BlockSpec((1, T, LANES), lambda b, h, i, j: (b, i, h)),
        out_shape=jax.ShapeDtypeStruct((B, L, DA_V), BF16),
        scratch_shapes=[pltpu.VMEM((2, T, 1), F32), pltpu.VMEM((2, T, 1), F32), pltpu.VMEM((2, T, DA_VDIM), F32)],
        compiler_params=_cparams(("parallel", "parallel", "parallel", "arbitrary")))(
            lam, qk, qk, v, bias, sub_gain.reshape(1, DA_VDIM))


def _wgqa_kernel(sink_ref, q_ref, kp_ref, kc_ref, kn_ref, vp_ref, vc_ref, vn_ref, bias_ref, o_ref, kw_ref, vw_ref):
    n = pl.program_id(1)
    nb = pl.num_programs(1)
    W = WG_BLOCK
    kw_ref[0:W] = kp_ref[0]
    kw_ref[W:2 * W] = kc_ref[0]
    kw_ref[2 * W:3 * W] = kn_ref[0]
    vw_ref[0:W] = vp_ref[0]
    vw_ref[W:2 * W] = vc_ref[0]
    vw_ref[2 * W:3 * W] = vn_ref[0]
    col = lax.broadcasted_iota(jnp.int32, (W, 3 * W), 1)
    valid = jnp.logical_and(jnp.logical_or(n > 0, col >= W), jnp.logical_or(n < nb - 1, col < 2 * W))
    q = q_ref[0]
    grp = WG_HEADS // WG_KV_HEADS
    for h in range(WG_HEADS):
        g = h // grp
        ksl = slice(g * HEAD_DIM, (g + 1) * HEAD_DIM)
        s = _dot_nt(q[:, h * HEAD_DIM:(h + 1) * HEAD_DIM], kw_ref[:, ksl]) + bias_ref[h]
        s = jnp.where(valid, s, NEG_INF)
        sk = sink_ref[h]
        mx = jnp.maximum(jnp.max(s, axis=-1, keepdims=True), sk)
        e = jnp.exp(s - mx)
        den = jnp.sum(e, axis=-1, keepdims=True) + jnp.exp(sk - mx)
        p = e / den
        o_ref[0, :, h * HEAD_DIM:(h + 1) * HEAD_DIM] = _dot(p.astype(BF16), vw_ref[:, ksl]).astype(o_ref.dtype)


def _wgqa(qk, vv, bias, sink, q_col, k_col, v_col):
    B, L, _ = qk.shape
    W = WG_BLOCK
    nb = L // W
    prev = lambda n: jnp.maximum(n - 1, 0)
    nxt = lambda n: jnp.minimum(n + 1, nb - 1)
    return pl.pallas_call(
        _wgqa_kernel, name="wgqa", grid=(B, nb),
        in_specs=[pl.BlockSpec(memory_space=pltpu.SMEM),
                  pl.BlockSpec((1, W, WG_Q), lambda b, n: (b, n, q_col)),
                  pl.BlockSpec((1, W, LANES), lambda b, n: (b, prev(n), k_col)),
                  pl.BlockSpec((1, W, LANES), lambda b, n: (b, n, k_col)),
                  pl.BlockSpec((1, W, LANES), lambda b, n: (b, nxt(n), k_col)),
                  pl.BlockSpec((1, W, LANES), lambda b, n: (b, prev(n), v_col)),
                  pl.BlockSpec((1, W, LANES), lambda b, n: (b, n, v_col)),
                  pl.BlockSpec((1, W, LANES), lambda b, n: (b, nxt(n), v_col)),
                  _full(bias.shape)],
        out_specs=pl.BlockSpec((1, W, WG_Q), lambda b, n: (b, n, 0)),
        out_shape=jax.ShapeDtypeStruct((B, L, WG_Q), BF16),
        scratch_shapes=[pltpu.VMEM((3 * W, LANES), BF16), pltpu.VMEM((3 * W, LANES), BF16)],
        compiler_params=_cparams(("parallel", "arbitrary")))(sink, qk, qk, qk, qk, vv, vv, vv, bias)


def _nabias_kernel(rpb_ref, o_ref):
    h = pl.program_id(0)
    c = lax.broadcasted_iota(jnp.int32, (GRID_W, LANES), 0)
    lane = lax.broadcasted_iota(jnp.int32, (GRID_W, LANES), 1)
    cc = lane % GRID_W
    hi = lane >= GRID_W
    c_start = jnp.clip(c - NA_COLS // 2, 0, GRID_W - NA_COLS)
    valid = jnp.logical_and(cc >= c_start, cc < c_start + NA_COLS)
    dc = cc - c + (NA_COLS - 1)
    n_dr = 2 * NA_ROWS - 1
    pairs = []
    for dr in range(n_dr - 1):
        acc = jnp.full((GRID_W, LANES), NEG_INF, F32)
        for d in range(2 * NA_COLS - 1):
            val = jnp.where(hi, rpb_ref[h, dr + 1, d], rpb_ref[h, dr, d])
            acc = jnp.where(jnp.logical_and(valid, dc == d), val, acc)
        pairs.append(acc)
    for e in range(NA_ROWS):
        for i2 in range(NA_ROWS // 2):
            o_ref[e, 0, :, i2 * LANES:(i2 + 1) * LANES] = pairs[e + 2 * i2]


def _nabias(rpb):
    return pl.pallas_call(
        _nabias_kernel, name="nabias", grid=(NA_HEADS,),
        in_specs=[pl.BlockSpec(memory_space=pltpu.SMEM)],
        out_specs=pl.BlockSpec((NA_ROWS, 1, GRID_W, NA_ROWS * GRID_W), lambda h: (0, h, 0, 0)),
        out_shape=jax.ShapeDtypeStruct((NA_ROWS, NA_HEADS, GRID_W, NA_ROWS * GRID_W), F32),
        compiler_params=_cparams(("arbitrary",)))(rpb)


def _natten_kernel(q_ref, kp_ref, kc_ref, kn_ref, vp_ref, vc_ref, vn_ref, bias_ref, o_ref, kw_ref, vw_ref, *, rows):
    i = pl.program_id(1)
    RB = NA_ROWS
    T = RB * GRID_W
    kw_ref[0:T] = kp_ref[0]
    kw_ref[T:2 * T] = kc_ref[0]
    kw_ref[2 * T:3 * T] = kn_ref[0]
    vw_ref[0:T] = vp_ref[0]
    vw_ref[T:2 * T] = vc_ref[0]
    vw_ref[2 * T:3 * T] = vn_ref[0]
    for qr in range(RB):
        r = i * RB + qr
        r_start = jnp.clip(r - NA_ROWS // 2, 0, rows - NA_ROWS)
        off = pl.multiple_of((r_start - i * RB + RB) * GRID_W, GRID_W)
        e = r_start - r + (NA_ROWS - 1)
        kw = kw_ref[pl.ds(off, T), :]
        vw = vw_ref[pl.ds(off, T), :]
        qrow = q_ref[0, qr * GRID_W:(qr + 1) * GRID_W, :]
        for h in range(NA_HEADS):
            sl = slice(h * HEAD_DIM, (h + 1) * HEAD_DIM)
            s = _dot_nt(qrow[:, sl], kw[:, sl]) + bias_ref[e, h]
            ex = jnp.exp(s - jnp.max(s, axis=-1, keepdims=True))
            p = ex / jnp.sum(ex, axis=-1, keepdims=True)
            o_ref[0, qr * GRID_W:(qr + 1) * GRID_W, sl] = _dot(p.astype(BF16), vw[:, sl]).astype(o_ref.dtype)


def _natten(qk, vv, bias):
    B, L, _ = qk.shape
    rows = L // GRID_W
    T = NA_ROWS * GRID_W
    nb = L // T
    prev = lambda n: jnp.maximum(n - 1, 0)
    nxt = lambda n: jnp.minimum(n + 1, nb - 1)
    return pl.pallas_call(
        functools.partial(_natten_kernel, rows=rows), name="natten", grid=(B, nb),
        in_specs=[pl.BlockSpec((1, T, NA_W), lambda b, n: (b, n, 0)),
                  pl.BlockSpec((1, T, NA_W), lambda b, n: (b, prev(n), 1)),
                  pl.BlockSpec((1, T, NA_W), lambda b, n: (b, n, 1)),
                  pl.BlockSpec((1, T, NA_W), lambda b, n: (b, nxt(n), 1)),
                  pl.BlockSpec((1, T, NA_W), lambda b, n: (b, prev(n), 0)),
                  pl.BlockSpec((1, T, NA_W), lambda b, n: (b, n, 0)),
                  pl.BlockSpec((1, T, NA_W), lambda b, n: (b, nxt(n), 0)),
                  _full(bias.shape)],
        out_specs=pl.BlockSpec((1, T, NA_W), lambda b, n: (b, n, 0)),
        out_shape=jax.ShapeDtypeStruct((B, L, NA_W), BF16),
        scratch_shapes=[pltpu.VMEM((3 * T, NA_W), BF16), pltpu.VMEM((3 * T, NA_W), BF16)],
        compiler_params=_cparams(("parallel", "arbitrary")))(qk, qk, qk, qk, vv, vv, vv, bias)


def _split(a):
    hi = a.astype(BF16)
    return hi, (a - hi.astype(F32)).astype(BF16)


def _np_split(a):
    a = np.asarray(a, np.float32)
    hi = a.astype(BF16)
    lo = (a - hi.astype(np.float32)).astype(BF16)
    return hi, lo


def _dot3_cl(ch, cl, x):
    xh, xl = _split(x)
    return _dot(ch, xh) + _dot(cl, xh) + _dot(ch, xl)


def _dot3_cr(x, ch, cl):
    xh, xl = _split(x)
    return _dot(xh, ch) + _dot(xl, ch) + _dot(xh, cl)


def _fft_consts(L):
    N2 = FFT_LANES
    N = 2 * L
    N1 = N // N2
    N1h = N1 // 2
    G = max(1, LANES // N1h)
    Gf = max(1, LANES // N1)
    k = np.arange(N1)
    ang1 = 2.0 * np.pi * np.outer(k, k) / N1
    C1, S1 = np.cos(ang1), np.sin(ang1)
    eye = np.eye
    KC, KS = np.kron(eye(G), C1[:, :N1h]), np.kron(eye(G), S1[:, :N1h])
    fa = np.block([[KC, KS], [-KS, KC]])
    KCh, KSh = np.kron(eye(G), C1[:N1h, :]), np.kron(eye(G), S1[:N1h, :])
    fg = np.block([[KCh, -KSh], [KSh, KCh]]) / N
    faf = np.concatenate([np.kron(eye(Gf), C1), -np.kron(eye(Gf), S1)], axis=0)
    n2 = np.arange(N2)
    angt = 2.0 * np.pi * np.outer(k, n2) / N
    twr, twi = np.cos(angt), -np.sin(angt)
    ang2 = 2.0 * np.pi * np.outer(n2, n2) / N2
    C2, S2 = np.cos(ang2), np.sin(ang2)
    m2f = np.block([[C2, -S2], [S2, C2]])
    m2i = np.block([[C2, S2], [-S2, C2]])
    return dict(N=N, N1=N1, N1h=N1h, N2=N2, G=G, Gf=Gf, fa=_np_split(fa), fg=_np_split(fg), faf=_np_split(faf),
                m2f=_np_split(m2f), m2i=_np_split(m2i),
                twr=np.asarray(twr, np.float32), twi=np.asarray(twi, np.float32))


def _hyfilt_kernel(w1t_ref, b1_ref, fr_ref, w2t_ref, b2_ref, w3t_ref, delta_ref, band_ref, o_ref, sum_ref, *, L, tn):
    s = pl.program_id(0)
    n = s * tn + lax.broadcasted_iota(jnp.int32, (1, tn), 1)
    t = jnp.where(n < L, n, 2 * L - n).astype(F32)
    t01 = t / float(max(L - 1, 1))
    w = (2.0 * math.pi) * t / float(L)
    ang = band_ref[...] * w
    cs, sn = jnp.cos(ang), -jnp.sin(ang)
    w1t = w1t_ref[...]
    h = w1t[:, 0:1] * t01
    for b in range(HY_BANDS):
        h = h + w1t[:, 1 + b:2 + b] * cs[b:b + 1] + w1t[:, 1 + HY_BANDS + b:2 + HY_BANDS + b] * sn[b:b + 1]
    fr = fr_ref[...]
    h = jnp.sin(fr[:, 0:1] * (h + b1_ref[...]))
    h = jnp.sin(fr[:, 1:2] * (jnp.dot(w2t_ref[...], h, preferred_element_type=F32,
                                      precision=lax.Precision.HIGHEST) + b2_ref[...]))
    y = jnp.dot(w3t_ref[0], h, preferred_element_type=F32, precision=lax.Precision.HIGHEST)
    y = y * jnp.exp(-delta_ref[...] * t01)
    y = jnp.where(n == L, 0.0, y)
    o_ref[...] = y
    a = jnp.abs(y)
    part = a[:, 0:LANES]
    for c in range(1, tn // LANES):
        part = part + a[:, c * LANES:(c + 1) * LANES]

    @pl.when(s == 0)
    def _():
        sum_ref[...] = jnp.zeros(sum_ref.shape, F32)

    sum_ref[...] += part


def _hyena_filters_td(L, w1, b1, freq, w2, b2, w3, tn=1024):
    OC = HY_ORDER * HY_CH
    w3t = jnp.transpose(w3.reshape(HY_HIDDEN, HY_ORDER, 2, HY_CH), (2, 1, 3, 0)).reshape(2, OC, HY_HIDDEN)
    min_decay = math.log(HY_TARGET) / HY_SLOW_DECAY
    max_decay = math.log(HY_TARGET) / HY_FAST_DECAY
    deltas = np.abs(np.linspace(min_decay, max_decay, HY_CH, dtype=np.float32))
    delta = jnp.asarray(np.tile(deltas, HY_ORDER).reshape(OC, 1))
    bands = jnp.asarray(np.linspace(1e-4, HY_BANDS - 1, HY_BANDS, dtype=np.float32).reshape(HY_BANDS, 1))
    N = 2 * L
    nh = L // tn
    return pl.pallas_call(
        functools.partial(_hyfilt_kernel, L=L, tn=tn), name="hyfilt", grid=(N // tn,),
        in_specs=[_full((HY_HIDDEN, 1 + 2 * HY_BANDS)), _full((HY_HIDDEN, 1)), _full((HY_HIDDEN, 2)),
                  _full((HY_HIDDEN, HY_HIDDEN)), _full((HY_HIDDEN, 1)),
                  pl.BlockSpec((1, OC, HY_HIDDEN), lambda s: (s // nh, 0, 0)),
                  _full((OC, 1)), _full((HY_BANDS, 1))],
        out_specs=[pl.BlockSpec((OC, tn), lambda s: (0, s)), _full((OC, LANES))],
        out_shape=[jax.ShapeDtypeStruct((OC, N), F32), jax.ShapeDtypeStruct((OC, LANES), F32)],
        compiler_params=_cparams(("arbitrary",)))(
            w1.T, b1.reshape(HY_HIDDEN, 1), freq.T, w2.T, b2.reshape(HY_HIDDEN, 1), w3t, delta, bands)


def _hyspec_kernel(k_ref, sum_ref, fah_ref, fal_ref, twr_ref, twi_ref, m2h_ref, m2l_ref, kr_ref, ki_ref, p_ref,
                   *, R, N1, N2, Gf):
    tot = jnp.sum(sum_ref[...], axis=-1, keepdims=True)
    mi = Gf * N1
    for g in range(R // Gf):
        xs = []
        for r in range(Gf):
            row = g * Gf + r
            xs.append(k_ref[row] / tot[row:row + 1, :])
        x = xs[0] if Gf == 1 else jnp.concatenate(xs, axis=0)
        a = _dot3_cl(fah_ref[...], fal_ref[...], x)
        ar, ai = a[:mi], a[mi:]
        twr, twi = twr_ref[...], twi_ref[...]
        p_ref[g * mi:(g + 1) * mi, 0:N2] = ar * twr - ai * twi
        p_ref[g * mi:(g + 1) * mi, N2:2 * N2] = ar * twi + ai * twr
    b = _dot3_cr(p_ref[...], m2h_ref[...], m2l_ref[...])
    kr_ref[...] = b[:, :N2].reshape(R, N1, N2)
    ki_ref[...] = b[:, N2:].reshape(R, N1, N2)


def _hyena_spectra(ktd, ksum, fc):
    OC = ktd.shape[0]
    N1, N2, Gf = fc["N1"], fc["N2"], fc["Gf"]
    R = 1024 // N1
    twr = jnp.asarray(np.tile(fc["twr"], (Gf, 1)))
    twi = jnp.asarray(np.tile(fc["twi"], (Gf, 1)))
    fah, fal = fc["faf"]
    m2h, m2l = fc["m2f"]
    return pl.pallas_call(
        functools.partial(_hyspec_kernel, R=R, N1=N1, N2=N2, Gf=Gf), name="hyspec", grid=(OC // R,),
        in_specs=[pl.BlockSpec((R, N1, N2), lambda i: (i, 0, 0)), pl.BlockSpec((R, LANES), lambda i: (i, 0)),
                  _full(fah.shape), _full(fal.shape), _full(twr.shape), _full(twi.shape),
                  _full(m2h.shape), _full(m2l.shape)],
        out_specs=[pl.BlockSpec((R, N1, N2), lambda i: (i, 0, 0))] * 2,
        out_shape=[jax.ShapeDtypeStruct((OC, N1, N2), F32)] * 2,
        scratch_shapes=[pltpu.VMEM((R * N1, 2 * N2), F32)],
        compiler_params=_cparams(("parallel",)))(ktd.reshape(OC, N1, N2), ksum, fah, fal, twr, twi, m2h, m2l)


def _hyconv_kernel(cw_ref, cb_ref, sk_ref, u_ref, kr_ref, ki_ref, fah_ref, fal_ref, fgh_ref, fgl_ref,
                   twr_ref, twi_ref, mfh_ref, mfl_ref, mih_ref, mil_ref, o_ref,
                   z_ref, x1_ref, x2_ref, p_ref, q_ref, *, R, N1, N2, G):
    N1h = N1 // 2
    cb = pl.program_id(0)
    rows = R * N1h
    ng = R // G
    mi = G * N1h
    mo = G * N1
    lane = lax.broadcasted_iota(jnp.int32, (N1h, N2), 1)
    sub = lax.broadcasted_iota(jnp.int32, (N1h, N2), 0)
    first = jnp.logical_and(lane == 0, sub == 0)
    last = jnp.logical_and(lane == N2 - 1, sub == N1h - 1)

    def shortconv(u, part, ch):
        rl = pltpu.roll(u, 1, 1)
        prev = jnp.where(lane == 0, pltpu.roll(rl, 1, 0), rl)
        prev = jnp.where(first, 0.0, prev)
        rr = pltpu.roll(u, N2 - 1, 1)
        nxt = jnp.where(lane == N2 - 1, pltpu.roll(rr, N1h - 1, 0), rr)
        nxt = jnp.where(last, 0.0, nxt)
        c = part * HY_CH + ch
        return prev * cw_ref[0, c] + u * cw_ref[1, c] + nxt * cw_ref[2, c] + cb_ref[c]

    def prep(r, carry):
        ch = cb * R + r
        g = r // G
        rg = r % G
        for pbatch in range(2):
            base = pl.multiple_of((g * 2 + pbatch) * mi + rg * N1h, 8)
            z_ref[pl.ds(base, N1h), :] = shortconv(u_ref[0, pbatch, 0, r], 0, ch)
            x1_ref[pl.ds(base, N1h), :] = shortconv(u_ref[0, pbatch, 1, r], 1, ch)
            x2_ref[pl.ds(base, N1h), :] = shortconv(u_ref[0, pbatch, 2, r], 2, ch)
        return carry

    lax.fori_loop(0, R, prep, 0)

    def conv(order):
        twr, twi = twr_ref[...], twi_ref[...]
        for g in range(ng):
            a = _dot3_cl(fah_ref[...], fal_ref[...], z_ref[g * 2 * mi:(g + 1) * 2 * mi, :])
            ar, ai = a[:mo], a[mo:]
            p_ref[g * mo:(g + 1) * mo, 0:N2] = ar * twr - ai * twi
            p_ref[g * mo:(g + 1) * mo, N2:2 * N2] = ar * twi + ai * twr
        b = _dot3_cr(p_ref[...], mfh_ref[...], mfl_ref[...])
        br, bi = b[:, :N2], b[:, N2:]
        kr = kr_ref[order].reshape(R * N1, N2)
        ki = ki_ref[order].reshape(R * N1, N2)
        p_ref[:, 0:N2] = br * kr - bi * ki
        p_ref[:, N2:2 * N2] = br * ki + bi * kr
        d = _dot3_cr(p_ref[...], mih_ref[...], mil_ref[...])
        dr, di = d[:, :N2], d[:, N2:]
        for g in range(ng):
            sl = slice(g * mo, (g + 1) * mo)
            q_ref[0:mo, :] = dr[sl] * twr + di[sl] * twi
            q_ref[mo:2 * mo, :] = di[sl] * twr - dr[sl] * twi
            p_ref[g * 2 * mi:(g + 1) * 2 * mi, 0:N2] = _dot3_cl(fgh_ref[...], fgl_ref[...], q_ref[...])

    def gate(order, x_ref, final):
        def body(r, carry):
            ch = cb * R + r
            g = r // G
            rg = r % G
            skv = sk_ref[order, ch]
            for pbatch in range(2):
                base = pl.multiple_of((g * 2 + pbatch) * mi + rg * N1h, 8)
                zc = p_ref[pl.ds(base, N1h), 0:N2]
                zn = x_ref[pl.ds(base, N1h), :] * (zc + skv * z_ref[pl.ds(base, N1h), :])
                if final:
                    o_ref[0, pbatch, r] = zn
                else:
                    z_ref[pl.ds(base, N1h), :] = zn
            return carry
        lax.fori_loop(0, R, body, 0)

    conv(0)
    gate(0, x1_ref, False)
    conv(1)
    gate(1, x2_ref, True)


def _hyena_conv(ucm, conv_w, conv_b, skip, kr, ki, fc):
    B = ucm.shape[0]
    C = HY_CH
    N1, N2, G = fc["N1"], fc["N2"], fc["G"]
    N1h = N1 // 2
    R = 1024 // N1
    u = ucm.reshape(B // 2, 2, 3, C, N1h, N2)
    twr = jnp.asarray(np.tile(fc["twr"], (G, 1)))
    twi = jnp.asarray(np.tile(fc["twi"], (G, 1)))
    consts = [*fc["fa"], *fc["fg"], twr, twi, *fc["m2f"], *fc["m2i"]]
    rows = R * N1h
    smem = pl.BlockSpec(memory_space=pltpu.SMEM)
    out = pl.pallas_call(
        functools.partial(_hyconv_kernel, R=R, N1=N1, N2=N2, G=G), name="hyconv",
        grid=(C // R, B // 2),
        in_specs=[smem, smem, smem,
                  pl.BlockSpec((1, 2, 3, R, N1h, N2), lambda c, p: (p, 0, 0, c, 0, 0)),
                  pl.BlockSpec((HY_ORDER, R, N1, N2), lambda c, p: (0, c, 0, 0)),
                  pl.BlockSpec((HY_ORDER, R, N1, N2), lambda c, p: (0, c, 0, 0))]
                 + [_full(a.shape) for a in consts],
        out_specs=pl.BlockSpec((1, 2, R, N1h, N2), lambda c, p: (p, 0, c, 0, 0)),
        out_shape=jax.ShapeDtypeStruct((B // 2, 2, C, N1h, N2), F32),
        scratch_shapes=[pltpu.VMEM((2 * rows, N2), F32)] * 3
                       + [pltpu.VMEM((R * N1, 2 * N2), F32), pltpu.VMEM((2 * G * N1, N2), F32)],
        compiler_params=_cparams(("parallel", "arbitrary")))(
            conv_w, conv_b, skip, u, kr.reshape(HY_ORDER, C, N1, N2), ki.reshape(HY_ORDER, C, N1, N2), *consts)
    return out.reshape(B, C, N1h * N2)


def _lambda_init(layer):
    return 0.8 - 0.6 * math.exp(-0.3 * layer)


def _tile_gain(g, reps, scale=1.0):
    return jnp.tile(g.astype(F32), reps) * scale


def _trunk(x, mem, p, shared):
    B, L, D = x.shape
    M = B * L
    scale = HEAD_DIM ** -0.5
    x2 = x.reshape(M, D)
    for layer in range(DEPTH):
        j = layer // 2
        if layer % 2 == 0:
            w_in = p["w_in_even"][j].astype(BF16)
            gain = jnp.concatenate([_tile_gain(p["da_q_gain"][j], DA_QK // HEAD_DIM, scale),
                                    _tile_gain(p["da_k_gain"][j], DA_QK // HEAD_DIM)])
            qk, v, ucm = _inproj(x2, p["norm_mix"][layer], w_in[:, :2 * DA_QK], gain,
                                 w_in[:, 2 * DA_QK:2 * DA_QK + DA_V], w_in[:, 2 * DA_QK + DA_V:].T, batch=B)
            T = shared["da_T"]
            oa = _diffattn(qk.reshape(B, L, 2 * DA_QK), v.reshape(B, L, DA_V), shared["da_bias"],
                           p["da_lambda"][j], p["da_sub_gain"][j], _lambda_init(layer), T)
            fc = _fft_consts(L)
            ktd, ksum = _hyena_filters_td(L, p["hy_w1"][j], p["hy_b1"][j], p["hy_freq"][j], p["hy_w2"][j],
                                          p["hy_b2"][j], p["hy_w3"][j])
            kr, ki = _hyena_spectra(ktd, ksum, fc)
            ob = _hyena_conv(ucm, p["hy_conv_w"][j], p["hy_conv_b"][j], p["hy_skip"][j], kr, ki, fc)
            w_out = p["w_out_even"][j].astype(BF16)
            x2 = _outproj(x2, oa.reshape(M, DA_V), ob, w_out[:DA_V], w_out[DA_V:], b_cm=True, batch=B)
        else:
            w_in = p["w_in_odd"][j].astype(BF16)
            c0 = 3 * NA_W
            w_gn = jnp.concatenate([w_in[:, :2 * NA_W], w_in[:, c0:c0 + WG_Q + WG_KV]], axis=1)
            w_pl = jnp.concatenate([w_in[:, 2 * NA_W:3 * NA_W], w_in[:, c0 + WG_Q + WG_KV:]], axis=1)
            gain = jnp.concatenate([_tile_gain(p["na_q_gain"][j], NA_HEADS, scale),
                                    _tile_gain(p["na_k_gain"][j], NA_HEADS),
                                    _tile_gain(p["wg_q_gain"][j], WG_HEADS, scale),
                                    _tile_gain(p["wg_k_gain"][j], WG_KV_HEADS)])
            qk, vv = _inproj(x2, p["norm_mix"][layer], w_gn, gain, w_pl)
            qk = qk.reshape(B, L, -1)
            vv = vv.reshape(B, L, -1)
            oc = _natten(qk, vv, shared["na_bias"][j])
            od = _wgqa(qk, vv, shared["wg_bias"], p["wg_sink"][j], q_col=2 * NA_W // WG_Q,
                       k_col=(2 * NA_W + WG_Q) // LANES, v_col=NA_W // LANES)
            w_out = p["w_out_odd"][j].astype(BF16)
            x2 = _outproj(x2, oc.reshape(M, NA_W), od.reshape(M, WG_Q), w_out[:NA_W], w_out[NA_W:], b_cm=False)
        kn, vn = _memkv(mem, p["norm_memkv"][layer], p["mem_wkv"][layer].astype(BF16),
                        _tile_gain(p["mem_k_gain"][layer], MEM_HEADS))
        x3 = _memattn(x2.reshape(B, L, D), kn, vn, p["norm_mem"][layer], p["mem_wq"][layer].astype(BF16),
                      _tile_gain(p["mem_q_gain"][layer], MEM_HEADS, scale), p["mem_wo"][layer].astype(BF16))
        x2 = _mlp(x3.reshape(M, D), p["norm_mlp"][layer], p["mlp_w1"][layer].astype(BF16),
                  p["mlp_w2"][layer].astype(BF16))
    return x2.reshape(B, L, D)


def _shared_tables(p, da_T):
    idx = np.arange(da_T)
    rel_da = np.stack([d * da_T + idx[None, :] - idx[:, None] for d in (-1, 0, 1)])
    qi = np.arange(WG_BLOCK)
    ki = np.arange(3 * WG_BLOCK)
    rel_wg = ki[None, :] - WG_BLOCK - qi[:, None]
    mask_wg = np.where(np.abs(rel_wg) <= WG_WINDOW, 0.0, NEG_INF).astype(np.float32)
    return dict(da_T=da_T,
                da_bias=_t5bias(p["t5_table"], rel_da),
                wg_bias=_t5bias(p["t5_table"], rel_wg, mask_wg),
                na_bias=[_nabias(p["na_rpb"][j]) for j in range(p["na_rpb"].shape[0])])


def kernel(x_prompt, x_sample, mem_prompt, mem_sample, t5_table, norm_mix, norm_mem, norm_memkv, norm_mlp, w_in_even, da_q_gain, da_k_gain, da_lambda, da_sub_gain, hy_conv_w, hy_conv_b, hy_w1, hy_b1, hy_freq, hy_w2, hy_b2, hy_w3, hy_skip, w_out_even, w_in_odd, na_q_gain, na_k_gain, na_rpb, wg_q_gain, wg_k_gain, wg_sink, w_out_odd, mem_wq, mem_wkv, mem_wo, mem_q_gain, mem_k_gain, mlp_w1, mlp_w2):
    p = dict(t5_table=t5_table, norm_mix=norm_mix, norm_mem=norm_mem, norm_memkv=norm_memkv, norm_mlp=norm_mlp,
             w_in_even=w_in_even, da_q_gain=da_q_gain, da_k_gain=da_k_gain, da_lambda=da_lambda,
             da_sub_gain=da_sub_gain, hy_conv_w=hy_conv_w, hy_conv_b=hy_conv_b, hy_w1=hy_w1, hy_b1=hy_b1,
             hy_freq=hy_freq, hy_w2=hy_w2, hy_b2=hy_b2, hy_w3=hy_w3, hy_skip=hy_skip, w_out_even=w_out_even,
             w_in_odd=w_in_odd, na_q_gain=na_q_gain, na_k_gain=na_k_gain, na_rpb=na_rpb, wg_q_gain=wg_q_gain,
             wg_k_gain=wg_k_gain, wg_sink=wg_sink, w_out_odd=w_out_odd, mem_wq=mem_wq, mem_wkv=mem_wkv,
             mem_wo=mem_wo, mem_q_gain=mem_q_gain, mem_k_gain=mem_k_gain, mlp_w1=mlp_w1, mlp_w2=mlp_w2)
    shared = _shared_tables(p, da_T=512)
    return (_trunk(x_prompt, mem_prompt, p, shared), _trunk(x_sample, mem_sample, p, shared))
```

```python
import functools
import math

import numpy as np
import jax
import jax.numpy as jnp
from jax import lax
from jax.experimental import pallas as pl
from jax.experimental.pallas import tpu as pltpu

F32 = jnp.float32
BF16 = jnp.bfloat16

D_MODEL = 1024
DEPTH = 2
HEAD_DIM = 64
DA_HEADS = 4
DA_VDIM = 2 * HEAD_DIM
DA_QK = DA_HEADS * 2 * HEAD_DIM
DA_V = DA_HEADS * DA_VDIM
HY_CH = D_MODEL // 2
HY_ORDER = 2
HY_BANDS = 8
HY_HIDDEN = 64
HY_FAST_DECAY = 0.3
HY_SLOW_DECAY = 1.5
HY_TARGET = 1e-2
NA_HEADS = 8
GRID_W = 64
NA_ROWS = 8
NA_COLS = 16
NA_W = NA_HEADS * HEAD_DIM
WG_HEADS = 8
WG_KV_HEADS = 2
WG_WINDOW = 128
WG_BLOCK = 128
WG_Q = WG_HEADS * HEAD_DIM
WG_KV = WG_KV_HEADS * HEAD_DIM
T5_BUCKETS = 32
T5_MAX_DIST = 128
T5_HEADS = 8
MEM_HEADS = 4
MEM_W = MEM_HEADS * HEAD_DIM
D_FF = 4 * D_MODEL
EPS = 1e-6
NEG_INF = -1e30
LOG2E = 1.4426950408889634

LANES = 128
FFT_LANES = 256
VMEM_LIMIT = 56 * 1024 * 1024


def _cparams(sem, vmem=None):
    return pltpu.CompilerParams(dimension_semantics=sem, vmem_limit_bytes=vmem or VMEM_LIMIT)


def _full(shape):
    n = len(shape)
    return pl.BlockSpec(shape, lambda *_: (0,) * n)


def _dot(a, b):
    return jnp.dot(a, b, preferred_element_type=F32)


def _dot_nt(a, b):
    return lax.dot_general(a, b, (((1,), (1,)), ((), ())), preferred_element_type=F32)


def _group_gmat():
    g = np.arange(LANES) // HEAD_DIM
    return jnp.asarray((g[:, None] == g[None, :]).astype(np.float32), dtype=BF16)


def _rms_rows(x, g):
    ms = jnp.mean(x * x, axis=-1, keepdims=True)
    return x * lax.rsqrt(ms + EPS) * g


def _group_rms(y, gmat, gain):
    outs = []
    for c in range(y.shape[1] // LANES):
        yc = y[:, c * LANES:(c + 1) * LANES]
        ss = _dot((yc * yc).astype(BF16), gmat)
        outs.append(yc * lax.rsqrt(ss * (1.0 / HEAD_DIM) + EPS) * gain[:, c * LANES:(c + 1) * LANES])
    return outs


def _inproj_kernel(*refs, n_tr):
    it = iter(refs)
    x_ref, g_ref = next(it), next(it)
    wgn_ref, gain_ref, gmat_ref = next(it), next(it), next(it)
    wpl_ref = next(it)
    wtr_ref = next(it) if n_tr else None
    ogn_ref, opl_ref = next(it), next(it)
    otr_ref = next(it) if n_tr else None

    hn = _rms_rows(x_ref[...], g_ref[...]).astype(BF16)
    y = _dot(hn, wgn_ref[...])
    gmat = gmat_ref[...]
    gain = gain_ref[...]
    for c, yc in enumerate(_group_rms(y, gmat, gain)):
        ogn_ref[:, c * LANES:(c + 1) * LANES] = yc.astype(ogn_ref.dtype)
    opl_ref[...] = _dot(hn, wpl_ref[...]).astype(opl_ref.dtype)
    if n_tr:
        otr_ref[0] = _dot_nt(wtr_ref[...], hn)


def _inproj(x2d, g, w_gn, gain, w_pl, w_tr=None, batch=None, tm=512):
    M, D = x2d.shape
    n_gn, n_pl = w_gn.shape[1], w_pl.shape[1]
    n_tr = 0 if w_tr is None else w_tr.shape[0]
    in_specs = [pl.BlockSpec((tm, D), lambda i: (i, 0)), _full((1, D)),
                _full((D, n_gn)), _full((1, n_gn)), _full((LANES, LANES)), _full((D, n_pl))]
    args = [x2d, g.reshape(1, D), w_gn, gain.reshape(1, n_gn), _group_gmat(), w_pl]
    out_shape = [jax.ShapeDtypeStruct((M, n_gn), BF16), jax.ShapeDtypeStruct((M, n_pl), BF16)]
    out_specs = [pl.BlockSpec((tm, n_gn), lambda i: (i, 0)), pl.BlockSpec((tm, n_pl), lambda i: (i, 0))]
    if n_tr:
        L = M // batch
        nt = L // tm
        in_specs.append(_full((n_tr, D)))
        args.append(w_tr)
        out_shape.append(jax.ShapeDtypeStruct((batch, n_tr, L), F32))
        out_specs.append(pl.BlockSpec((1, n_tr, tm), lambda i: (i // nt, 0, i % nt)))
    return pl.pallas_call(
        functools.partial(_inproj_kernel, n_tr=n_tr), name="inproj",
        grid=(M // tm,), in_specs=in_specs, out_specs=out_specs, out_shape=out_shape,
        compiler_params=_cparams(("parallel",)))(*args)


def _inproj_even_kernel(x_ref, g_ref, wk_ref, gain_ref, gmat_ref, wcm_ref, k_ref, qt_ref, vt_ref, ut_ref):
    hn = _rms_rows(x_ref[...], g_ref[...]).astype(BF16)
    yk = _dot(hn, wk_ref[...])
    for c, yc in enumerate(_group_rms(yk, gmat_ref[...], gain_ref[...])):
        k_ref[:, c * LANES:(c + 1) * LANES] = yc.astype(k_ref.dtype)
    yt = _dot_nt(wcm_ref[...], hn)
    tm = yt.shape[1]
    q = yt[:DA_QK].reshape(DA_QK // HEAD_DIM, HEAD_DIM, tm)
    ms = jnp.mean(q * q, axis=1, keepdims=True)
    qt_ref[0, 0] = (q * lax.rsqrt(ms + EPS)).reshape(DA_QK, tm).astype(qt_ref.dtype)
    vt_ref[0, 0] = yt[DA_QK:DA_QK + DA_V].astype(vt_ref.dtype)
    ut_ref[0] = yt[DA_QK + DA_V:]


def _inproj_even(x2d, g, w_k, k_gain, w_cm, batch, tm):
    M, D = x2d.shape
    L = M // batch
    nt = L // tm
    n_u = w_cm.shape[0] - DA_QK - DA_V
    return pl.pallas_call(
        _inproj_even_kernel, name="inproj_even", grid=(M // tm,),
        in_specs=[pl.BlockSpec((tm, D), lambda i: (i, 0)), _full((1, D)), _full((D, DA_QK)), _full((1, DA_QK)),
                  _full((LANES, LANES)), _full(w_cm.shape)],
        out_specs=[pl.BlockSpec((tm, DA_QK), lambda i: (i, 0)),
                   pl.BlockSpec((1, 1, DA_QK, tm), lambda i: (i // nt, i % nt, 0, 0)),
                   pl.BlockSpec((1, 1, DA_V, tm), lambda i: (i // nt, i % nt, 0, 0)),
                   pl.BlockSpec((1, n_u, tm), lambda i: (i // nt, 0, i % nt))],
        out_shape=[jax.ShapeDtypeStruct((M, DA_QK), BF16),
                   jax.ShapeDtypeStruct((batch, nt, DA_QK, tm), BF16),
                   jax.ShapeDtypeStruct((batch, nt, DA_V, tm), BF16),
                   jax.ShapeDtypeStruct((batch, n_u, L), F32)],
        compiler_params=_cparams(("parallel",)))(
            x2d, g.reshape(1, D), w_k, k_gain.reshape(1, DA_QK), _group_gmat(), w_cm)


def _outproj_kernel(x_ref, a_ref, b_ref, wa_ref, wb_ref, o_ref, *, b_cm):
    b = b_ref[0].T.astype(BF16) if b_cm else b_ref[...]
    o_ref[...] = x_ref[...] + _dot(a_ref[...], wa_ref[...]) + _dot(b, wb_ref[...])


def _outproj(x2d, a, b, wa, wb, *, b_cm, batch=None, tm=512):
    M, D = x2d.shape
    if b_cm:
        L = M // batch
        nt = L // tm
        b_spec = pl.BlockSpec((1, b.shape[1], tm), lambda i: (i // nt, 0, i % nt))
    else:
        b_spec = pl.BlockSpec((tm, b.shape[1]), lambda i: (i, 0))
    return pl.pallas_call(
        functools.partial(_outproj_kernel, b_cm=b_cm), name="outproj",
        grid=(M // tm,),
        in_specs=[pl.BlockSpec((tm, D), lambda i: (i, 0)), pl.BlockSpec((tm, a.shape[1]), lambda i: (i, 0)),
                  b_spec, _full(wa.shape), _full(wb.shape)],
        out_specs=pl.BlockSpec((tm, D), lambda i: (i, 0)),
        out_shape=jax.ShapeDtypeStruct((M, D), F32),
        compiler_params=_cparams(("parallel",)))(x2d, a, b, wa, wb)


def _mlp_kernel(x_ref, g_ref, w1_ref, w2_ref, o_ref, hn_ref):
    j = pl.program_id(1)

    @pl.when(j == 0)
    def _():
        x = x_ref[...]
        hn_ref[...] = _rms_rows(x, g_ref[...]).astype(BF16)
        o_ref[...] = x

    h = _dot(hn_ref[...], w1_ref[...])
    h = jnp.square(jnp.maximum(h, 0.0))
    o_ref[...] += _dot(h.astype(BF16), w2_ref[...])


def _mlp(x2d, g, w1, w2, tm=1024, tf=512):
    M, D = x2d.shape
    F = w1.shape[1]
    tm = min(tm, M)
    return pl.pallas_call(
        _mlp_kernel, name="mlp",
        grid=(M // tm, F // tf),
        in_specs=[pl.BlockSpec((tm, D), lambda i, j: (i, 0)), _full((1, D)),
                  pl.BlockSpec((D, tf), lambda i, j: (0, j)), pl.BlockSpec((tf, D), lambda i, j: (j, 0))],
        out_specs=pl.BlockSpec((tm, D), lambda i, j: (i, 0)),
        out_shape=jax.ShapeDtypeStruct((M, D), F32),
        scratch_shapes=[pltpu.VMEM((tm, D), BF16)],
        compiler_params=_cparams(("parallel", "arbitrary")))(x2d, g.reshape(1, D), w1, w2)


def _memkv_kernel(m_ref, g_ref, w_ref, gain_ref, gmat_ref, k_ref, v_ref):
    mn = _rms_rows(m_ref[0], g_ref[...]).astype(BF16)
    kv = _dot(mn, w_ref[...])
    for c, kc in enumerate(_group_rms(kv[:, :MEM_W], gmat_ref[...], gain_ref[...])):
        k_ref[0, :, c * LANES:(c + 1) * LANES] = kc.astype(BF16)
    v_ref[0] = kv[:, MEM_W:].astype(BF16)


def _memkv(mem, g, wkv, kgain):
    B, M, D = mem.shape
    return pl.pallas_call(
        _memkv_kernel, name="memkv", grid=(B,),
        in_specs=[pl.BlockSpec((1, M, D), lambda b: (b, 0, 0)), _full((1, D)), _full(wkv.shape),
                  _full((1, MEM_W)), _full((LANES, LANES))],
        out_specs=[pl.BlockSpec((1, M, MEM_W), lambda b: (b, 0, 0))] * 2,
        out_shape=[jax.ShapeDtypeStruct((B, M, MEM_W), BF16)] * 2,
        compiler_params=_cparams(("parallel",)))(mem, g.reshape(1, D), wkv, kgain.reshape(1, MEM_W), _group_gmat())


def _memattn_kernel(x_ref, g_ref, wq_ref, qgain_ref, gmat_ref, k_ref, v_ref, wo_ref, o_ref, oh_ref):
    x = x_ref[0]
    hn = _rms_rows(x, g_ref[...]).astype(BF16)
    q = _dot(hn, wq_ref[...])
    qn = jnp.concatenate(_group_rms(q, gmat_ref[...], qgain_ref[...]), axis=1).astype(BF16)
    k = k_ref[0]
    v = v_ref[0]
    for h in range(MEM_HEADS):
        sl = slice(h * HEAD_DIM, (h + 1) * HEAD_DIM)
        s = _dot_nt(qn[:, sl], k[:, sl])
        e = jnp.exp(s - jnp.max(s, axis=-1, keepdims=True))
        p = e / jnp.sum(e, axis=-1, keepdims=True)
        oh_ref[:, sl] = _dot(p.astype(BF16), v[:, sl])
    o_ref[0] = x + _dot(oh_ref[...].astype(BF16), wo_ref[...])


def _memattn(x, kn, vn, g, wq, qgain, wo, tm=512):
    B, L, D = x.shape
    M = kn.shape[1]
    return pl.pallas_call(
        _memattn_kernel, name="memattn", grid=(B, L // tm),
        in_specs=[pl.BlockSpec((1, tm, D), lambda b, i: (b, i, 0)), _full((1, D)), _full(wq.shape),
                  _full((1, MEM_W)), _full((LANES, LANES)),
                  pl.BlockSpec((1, M, MEM_W), lambda b, i: (b, 0, 0)),
                  pl.BlockSpec((1, M, MEM_W), lambda b, i: (b, 0, 0)), _full(wo.shape)],
        out_specs=pl.BlockSpec((1, tm, D), lambda b, i: (b, i, 0)),
        out_shape=jax.ShapeDtypeStruct((B, L, D), F32),
        scratch_shapes=[pltpu.VMEM((tm, MEM_W), F32)],
        compiler_params=_cparams(("parallel", "parallel")))(
            x, g.reshape(1, D), wq, qgain.reshape(1, MEM_W), _group_gmat(), kn, vn, wo)


def _t5_bucket_np(rel):
    half = T5_BUCKETS // 2
    exact = half // 2
    n = np.abs(rel)
    nf = np.maximum(n, 1).astype(np.float64)
    large = exact + (np.log(nf / exact) / math.log(T5_MAX_DIST / exact) * (half - exact)).astype(np.int32)
    large = np.minimum(large, half - 1)
    return (np.where(rel > 0, half, 0) + np.where(n < exact, n, large)).astype(np.int32)


def _t5bias_kernel(table_ref, bucket_ref, mask_ref, o_ref, *, scale):
    h = pl.program_id(0)
    bucket = bucket_ref[...]
    acc = jnp.zeros(bucket.shape, F32)
    for b in range(T5_BUCKETS):
        acc = jnp.where(bucket == b, table_ref[b, h], acc)
    o_ref[0] = acc * scale + mask_ref[...]


def _t5bias(table, rel, mask=None, scale=1.0):
    bucket = jnp.asarray(_t5_bucket_np(rel))
    mask = jnp.zeros(rel.shape, F32) if mask is None else jnp.asarray(mask, F32)
    nd = rel.ndim
    return pl.pallas_call(
        functools.partial(_t5bias_kernel, scale=scale), name="t5bias", grid=(T5_HEADS,),
        in_specs=[pl.BlockSpec(memory_space=pltpu.SMEM), _full(rel.shape), _full(rel.shape)],
        out_specs=pl.BlockSpec((1,) + rel.shape, lambda h: (h,) + (0,) * nd),
        out_shape=jax.ShapeDtypeStruct((T5_HEADS,) + rel.shape, F32),
        compiler_params=_cparams(("arbitrary",)))(table, bucket, mask)


def _diffattn_kernel(lam_ref, q_ref, k_ref, v_ref, bias_ref, sg_ref, o_ref, qp_ref, m_ref, l_ref, acc_ref,
                     *, lam_init, T, nkv):
    i = pl.program_id(2)
    q = q_ref[0, 0]
    row = lax.broadcasted_iota(jnp.int32, q.shape, 0)
    zero = jnp.zeros(q.shape, q.dtype)
    qp_ref[0] = jnp.where(row < HEAD_DIM, q, zero)
    qp_ref[1] = jnp.where(row >= HEAD_DIM, q, zero)
    m_ref[...] = jnp.full(m_ref.shape, NEG_INF, F32)
    l_ref[...] = jnp.zeros(l_ref.shape, F32)
    acc_ref[...] = jnp.zeros(acc_ref.shape, F32)

    def body(jj, carry):
        k = k_ref[0, pl.ds(pl.multiple_of(jj * T, T), T), :]
        v = v_ref[0, jj]
        d = jnp.clip(jj - i + 1, 0, 2)
        for m in range(2):
            s = _dot(k, qp_ref[m]) + bias_ref[m, d]
            m_prev = m_ref[m]
            m_new = jnp.maximum(m_prev, jnp.max(s, axis=0, keepdims=True))
            alpha = jnp.exp2(m_prev - m_new)
            p = jnp.exp2(s - m_new)
            l_ref[m] = alpha * l_ref[m] + jnp.sum(p, axis=0, keepdims=True)
            acc_ref[m] = alpha * acc_ref[m] + _dot(v, p.astype(BF16))
            m_ref[m] = m_new
        return carry

    lax.fori_loop(0, nkv, body, 0)

    lf = lam_ref[...]
    lam = (jnp.exp(jnp.sum(lf[0:1] * lf[1:2], axis=-1, keepdims=True))
           - jnp.exp(jnp.sum(lf[2:3] * lf[3:4], axis=-1, keepdims=True)) + lam_init)
    o = acc_ref[0] / l_ref[0] - lam * (acc_ref[1] / l_ref[1])
    ms = jnp.mean(o * o, axis=0, keepdims=True)
    o = o * lax.rsqrt(ms + EPS) * sg_ref[...] * (1.0 - lam_init)
    o_ref[0] = o.T.astype(o_ref.dtype)


def _diffattn(qt, k, vt, bias, lam, sub_gain, lam_init):
    B, nb, _, T = qt.shape
    L = nb * T
    return pl.pallas_call(
        functools.partial(_diffattn_kernel, lam_init=lam_init, T=T, nkv=nb), name="diffattn",
        grid=(B, DA_HEADS, nb),
        in_specs=[_full((4, HEAD_DIM)),
                  pl.BlockSpec((1, 1, LANES, T), lambda b, h, i: (b, i, h, 0)),
                  pl.BlockSpec((1, L, LANES), lambda b, h, i: (b, 0, h)),
                  pl.BlockSpec((1, nb, LANES, T), lambda b, h, i: (b, 0, h, 0)),
                  pl.BlockSpec((2, 3, T, T), lambda b, h, i: (h, 0, 0, 0)),
                  _full((DA_VDIM, 1))],
        out_specs=pl.BlockSpec((1, T, LANES), lambda b, h, i: (b, i, h)),
        out_shape=jax.ShapeDtypeStruct((B, L, DA_V), BF16),
        scratch_shapes=[pltpu.VMEM((2, LANES, T), BF16), pltpu.VMEM((2, 1, T), F32), pltpu.VMEM((2, 1, T), F32),
                        pltpu.VMEM((2, DA_VDIM, T), F32)],
        compiler_params=_cparams(("parallel", "parallel", "arbitrary")))(
            lam, qt, k, vt, bias, sub_gain.reshape(DA_VDIM, 1))


def _wgqa_kernel(sink_ref, q_ref, kp_ref, kc_ref, kn_ref, vp_ref, vc_ref, vn_ref, bias_ref, o_ref, kw_ref, vw_ref):
    n = pl.program_id(1)
    nb = pl.num_programs(1)
    W = WG_BLOCK
    kw_ref[0:W] = kp_ref[0]
    kw_ref[W:2 * W] = kc_ref[0]
    kw_ref[2 * W:3 * W] = kn_ref[0]
    vw_ref[0:W] = vp_ref[0]
    vw_ref[W:2 * W] = vc_ref[0]
    vw_ref[2 * W:3 * W] = vn_ref[0]
    col = lax.broadcasted_iota(jnp.int32, (W, 3 * W), 1)
    valid = jnp.logical_and(jnp.logical_or(n > 0, col >= W), jnp.logical_or(n < nb - 1, col < 2 * W))
    q = q_ref[0]
    grp = WG_HEADS // WG_KV_HEADS
    for h in range(WG_HEADS):
        g = h // grp
        ksl = slice(g * HEAD_DIM, (g + 1) * HEAD_DIM)
        s = _dot_nt(q[:, h * HEAD_DIM:(h + 1) * HEAD_DIM], kw_ref[:, ksl]) + bias_ref[h]
        s = jnp.where(valid, s, NEG_INF)
        sk = sink_ref[h]
        mx = jnp.maximum(jnp.max(s, axis=-1, keepdims=True), sk)
        e = jnp.exp(s - mx)
        den = jnp.sum(e, axis=-1, keepdims=True) + jnp.exp(sk - mx)
        p = e / den
        o_ref[0, :, h * HEAD_DIM:(h + 1) * HEAD_DIM] = _dot(p.astype(BF16), vw_ref[:, ksl]).astype(o_ref.dtype)


def _wgqa(qk, vv, bias, sink, q_col, k_col, v_col):
    B, L, _ = qk.shape
    W = WG_BLOCK
    nb = L // W
    prev = lambda n: jnp.maximum(n - 1, 0)
    nxt = lambda n: jnp.minimum(n + 1, nb - 1)
    return pl.pallas_call(
        _wgqa_kernel, name="wgqa", grid=(B, nb),
        in_specs=[pl.BlockSpec(memory_space=pltpu.SMEM),
                  pl.BlockSpec((1, W, WG_Q), lambda b, n: (b, n, q_col)),
                  pl.BlockSpec((1, W, LANES), lambda b, n: (b, prev(n), k_col)),
                  pl.BlockSpec((1, W, LANES), lambda b, n: (b, n, k_col)),
                  pl.BlockSpec((1, W, LANES), lambda b, n: (b, nxt(n), k_col)),
                  pl.BlockSpec((1, W, LANES), lambda b, n: (b, prev(n), v_col)),
                  pl.BlockSpec((1, W, LANES), lambda b, n: (b, n, v_col)),
                  pl.BlockSpec((1, W, LANES), lambda b, n: (b, nxt(n), v_col)),
                  _full(bias.shape)],
        out_specs=pl.BlockSpec((1, W, WG_Q), lambda b, n: (b, n, 0)),
        out_shape=jax.ShapeDtypeStruct((B, L, WG_Q), BF16),
        scratch_shapes=[pltpu.VMEM((3 * W, LANES), BF16), pltpu.VMEM((3 * W, LANES), BF16)],
        compiler_params=_cparams(("parallel", "arbitrary")))(sink, qk, qk, qk, qk, vv, vv, vv, bias)


def _nabias_kernel(rpb_ref, o_ref):
    h = pl.program_id(0)
    c = lax.broadcasted_iota(jnp.int32, (GRID_W, LANES), 0)
    lane = lax.broadcasted_iota(jnp.int32, (GRID_W, LANES), 1)
    cc = lane % GRID_W
    hi = lane >= GRID_W
    c_start = jnp.clip(c - NA_COLS // 2, 0, GRID_W - NA_COLS)
    valid = jnp.logical_and(cc >= c_start, cc < c_start + NA_COLS)
    dc = cc - c + (NA_COLS - 1)
    n_dr = 2 * NA_ROWS - 1
    pairs = []
    for dr in range(n_dr - 1):
        acc = jnp.full((GRID_W, LANES), NEG_INF, F32)
        for d in range(2 * NA_COLS - 1):
            val = jnp.where(hi, rpb_ref[h, dr + 1, d], rpb_ref[h, dr, d])
            acc = jnp.where(jnp.logical_and(valid, dc == d), val, acc)
        pairs.append(acc)
    for e in range(NA_ROWS):
        for i2 in range(NA_ROWS // 2):
            o_ref[e, 0, :, i2 * LANES:(i2 + 1) * LANES] = pairs[e + 2 * i2]


def _nabias(rpb):
    return pl.pallas_call(
        _nabias_kernel, name="nabias", grid=(NA_HEADS,),
        in_specs=[pl.BlockSpec(memory_space=pltpu.SMEM)],
        out_specs=pl.BlockSpec((NA_ROWS, 1, GRID_W, NA_ROWS * GRID_W), lambda h: (0, h, 0, 0)),
        out_shape=jax.ShapeDtypeStruct((NA_ROWS, NA_HEADS, GRID_W, NA_ROWS * GRID_W), F32),
        compiler_params=_cparams(("arbitrary",)))(rpb)


def _natten_kernel(q_ref, kp_ref, kc_ref, kn_ref, vp_ref, vc_ref, vn_ref, bias_ref, o_ref, kw_ref, vw_ref, *, rows):
    i = pl.program_id(1)
    RB = NA_ROWS
    T = RB * GRID_W
    kw_ref[0:T] = kp_ref[0]
    kw_ref[T:2 * T] = kc_ref[0]
    kw_ref[2 * T:3 * T] = kn_ref[0]
    vw_ref[0:T] = vp_ref[0]
    vw_ref[T:2 * T] = vc_ref[0]
    vw_ref[2 * T:3 * T] = vn_ref[0]
    for qr in range(RB):
        r = i * RB + qr
        r_start = jnp.clip(r - NA_ROWS // 2, 0, rows - NA_ROWS)
        off = pl.multiple_of((r_start - i * RB + RB) * GRID_W, GRID_W)
        e = r_start - r + (NA_ROWS - 1)
        kw = kw_ref[pl.ds(off, T), :]
        vw = vw_ref[pl.ds(off, T), :]
        qrow = q_ref[0, qr * GRID_W:(qr + 1) * GRID_W, :]
        for h in range(NA_HEADS):
            sl = slice(h * HEAD_DIM, (h + 1) * HEAD_DIM)
            s = _dot_nt(qrow[:, sl], kw[:, sl]) + bias_ref[e, h]
            ex = jnp.exp(s - jnp.max(s, axis=-1, keepdims=True))
            p = ex / jnp.sum(ex, axis=-1, keepdims=True)
            o_ref[0, qr * GRID_W:(qr + 1) * GRID_W, sl] = _dot(p.astype(BF16), vw[:, sl]).astype(o_ref.dtype)


def _natten(qk, vv, bias):
    B, L, _ = qk.shape
    rows = L // GRID_W
    T = NA_ROWS * GRID_W
    nb = L // T
    prev = lambda n: jnp.maximum(n - 1, 0)
    nxt = lambda n: jnp.minimum(n + 1, nb - 1)
    return pl.pallas_call(
        functools.partial(_natten_kernel, rows=rows), name="natten", grid=(B, nb),
        in_specs=[pl.BlockSpec((1, T, NA_W), lambda b, n: (b, n, 0)),
                  pl.BlockSpec((1, T, NA_W), lambda b, n: (b, prev(n), 1)),
                  pl.BlockSpec((1, T, NA_W), lambda b, n: (b, n, 1)),
                  pl.BlockSpec((1, T, NA_W), lambda b, n: (b, nxt(n), 1)),
                  pl.BlockSpec((1, T, NA_W), lambda b, n: (b, prev(n), 0)),
                  pl.BlockSpec((1, T, NA_W), lambda b, n: (b, n, 0)),
                  pl.BlockSpec((1, T, NA_W), lambda b, n: (b, nxt(n), 0)),
                  _full(bias.shape)],
        out_specs=pl.BlockSpec((1, T, NA_W), lambda b, n: (b, n, 0)),
        out_shape=jax.ShapeDtypeStruct((B, L, NA_W), BF16),
        scratch_shapes=[pltpu.VMEM((3 * T, NA_W), BF16), pltpu.VMEM((3 * T, NA_W), BF16)],
        compiler_params=_cparams(("parallel", "arbitrary")))(qk, qk, qk, qk, vv, vv, vv, bias)


def _split(a):
    hi = a.astype(BF16)
    return hi, (a - hi.astype(F32)).astype(BF16)


def _np_split(a):
    a = np.asarray(a, np.float32)
    hi = a.astype(BF16)
    lo = (a - hi.astype(np.float32)).astype(BF16)
    return hi, lo


def _dot3_cl(ch, cl, x):
    xh, xl = _split(x)
    return _dot(ch, xh) + _dot(cl, xh) + _dot(ch, xl)


def _dot3_cr(x, ch, cl):
    xh, xl = _split(x)
    return _dot(xh, ch) + _dot(xl, ch) + _dot(xh, cl)


def _fft_consts(L):
    N2 = FFT_LANES
    N = 2 * L
    N1 = N // N2
    N1h = N1 // 2
    G = max(1, LANES // N1h)
    Gf = max(1, LANES // N1)
    k = np.arange(N1)
    ang1 = 2.0 * np.pi * np.outer(k, k) / N1
    C1, S1 = np.cos(ang1), np.sin(ang1)
    eye = np.eye
    KC, KS = np.kron(eye(G), C1[:, :N1h]), np.kron(eye(G), S1[:, :N1h])
    fa = np.block([[KC, KS], [-KS, KC]])
    KCh, KSh = np.kron(eye(G), C1[:N1h, :]), np.kron(eye(G), S1[:N1h, :])
    fg = np.block([[KCh, -KSh], [KSh, KCh]]) / N
    faf = np.concatenate([np.kron(eye(Gf), C1), -np.kron(eye(Gf), S1)], axis=0)
    n2 = np.arange(N2)
    angt = 2.0 * np.pi * np.outer(k, n2) / N
    twr, twi = np.cos(angt), -np.sin(angt)
    ang2 = 2.0 * np.pi * np.outer(n2, n2) / N2
    C2, S2 = np.cos(ang2), np.sin(ang2)
    m2f = np.block([[C2, -S2], [S2, C2]])
    m2i = np.block([[C2, S2], [-S2, C2]])
    return dict(N=N, N1=N1, N1h=N1h, N2=N2, G=G, Gf=Gf, fa=_np_split(fa), fg=_np_split(fg), faf=_np_split(faf),
                m2f=_np_split(m2f), m2i=_np_split(m2i),
                twr=np.asarray(twr, np.float32), twi=np.asarray(twi, np.float32))


def _hyfilt_kernel(w1t_ref, b1_ref, fr_ref, w2t_ref, b2_ref, w3t_ref, delta_ref, band_ref, o_ref, sum_ref, *, L, tn):
    s = pl.program_id(0)
    n = s * tn + lax.broadcasted_iota(jnp.int32, (1, tn), 1)
    t = jnp.where(n < L, n, 2 * L - n).astype(F32)
    t01 = t / float(max(L - 1, 1))
    w = (2.0 * math.pi) * t / float(L)
    ang = band_ref[...] * w
    cs, sn = jnp.cos(ang), -jnp.sin(ang)
    w1t = w1t_ref[...]
    h = w1t[:, 0:1] * t01
    for b in range(HY_BANDS):
        h = h + w1t[:, 1 + b:2 + b] * cs[b:b + 1] + w1t[:, 1 + HY_BANDS + b:2 + HY_BANDS + b] * sn[b:b + 1]
    fr = fr_ref[...]
    h = jnp.sin(fr[:, 0:1] * (h + b1_ref[...]))
    h = jnp.sin(fr[:, 1:2] * (jnp.dot(w2t_ref[...], h, preferred_element_type=F32,
                                      precision=lax.Precision.HIGHEST) + b2_ref[...]))
    y = jnp.dot(w3t_ref[0], h, preferred_element_type=F32, precision=lax.Precision.HIGHEST)
    y = y * jnp.exp(-delta_ref[...] * t01)
    y = jnp.where(n == L, 0.0, y)
    o_ref[...] = y
    a = jnp.abs(y)
    part = a[:, 0:LANES]
    for c in range(1, tn // LANES):
        part = part + a[:, c * LANES:(c + 1) * LANES]

    @pl.when(s == 0)
    def _():
        sum_ref[...] = jnp.zeros(sum_ref.shape, F32)

    sum_ref[...] += part


def _hyena_filters_td(L, w1, b1, freq, w2, b2, w3, tn=1024):
    OC = HY_ORDER * HY_CH
    w3t = jnp.transpose(w3.reshape(HY_HIDDEN, HY_ORDER, 2, HY_CH), (2, 1, 3, 0)).reshape(2, OC, HY_HIDDEN)
    min_decay = math.log(HY_TARGET) / HY_SLOW_DECAY
    max_decay = math.log(HY_TARGET) / HY_FAST_DECAY
    deltas = np.abs(np.linspace(min_decay, max_decay, HY_CH, dtype=np.float32))
    delta = jnp.asarray(np.tile(deltas, HY_ORDER).reshape(OC, 1))
    bands = jnp.asarray(np.linspace(1e-4, HY_BANDS - 1, HY_BANDS, dtype=np.float32).reshape(HY_BANDS, 1))
    N = 2 * L
    nh = L // tn
    return pl.pallas_call(
        functools.partial(_hyfilt_kernel, L=L, tn=tn), name="hyfilt", grid=(N // tn,),
        in_specs=[_full((HY_HIDDEN, 1 + 2 * HY_BANDS)), _full((HY_HIDDEN, 1)), _full((HY_HIDDEN, 2)),
                  _full((HY_HIDDEN, HY_HIDDEN)), _full((HY_HIDDEN, 1)),
                  pl.BlockSpec((1, OC, HY_HIDDEN), lambda s: (s // nh, 0, 0)),
                  _full((OC, 1)), _full((HY_BANDS, 1))],
        out_specs=[pl.BlockSpec((OC, tn), lambda s: (0, s)), _full((OC, LANES))],
        out_shape=[jax.ShapeDtypeStruct((OC, N), F32), jax.ShapeDtypeStruct((OC, LANES), F32)],
        compiler_params=_cparams(("arbitrary",)))(
            w1.T, b1.reshape(HY_HIDDEN, 1), freq.T, w2.T, b2.reshape(HY_HIDDEN, 1), w3t, delta, bands)


def _hyspec_kernel(k_ref, sum_ref, fah_ref, fal_ref, twr_ref, twi_ref, m2h_ref, m2l_ref, kr_ref, ki_ref, p_ref,
                   *, R, N1, N2, Gf):
    tot = jnp.sum(sum_ref[...], axis=-1, keepdims=True)
    mi = Gf * N1
    for g in range(R // Gf):
        xs = []
        for r in range(Gf):
            row = g * Gf + r
            xs.append(k_ref[row] / tot[row:row + 1, :])
        x = xs[0] if Gf == 1 else jnp.concatenate(xs, axis=0)
        a = _dot3_cl(fah_ref[...], fal_ref[...], x)
        ar, ai = a[:mi], a[mi:]
        twr, twi = twr_ref[...], twi_ref[...]
        p_ref[g * mi:(g + 1) * mi, 0:N2] = ar * twr - ai * twi
        p_ref[g * mi:(g + 1) * mi, N2:2 * N2] = ar * twi + ai * twr
    b = _dot3_cr(p_ref[...], m2h_ref[...], m2l_ref[...])
    kr_ref[...] = b[:, :N2].reshape(R, N1, N2)
    ki_ref[...] = b[:, N2:].reshape(R, N1, N2)


def _hyena_spectra(ktd, ksum, fc):
    OC = ktd.shape[0]
    N1, N2, Gf = fc["N1"], fc["N2"], fc["Gf"]
    R = 1024 // N1
    twr = jnp.asarray(np.tile(fc["twr"], (Gf, 1)))
    twi = jnp.asarray(np.tile(fc["twi"], (Gf, 1)))
    fah, fal = fc["faf"]
    m2h, m2l = fc["m2f"]
    return pl.pallas_call(
        functools.partial(_hyspec_kernel, R=R, N1=N1, N2=N2, Gf=Gf), name="hyspec", grid=(OC // R,),
        in_specs=[pl.BlockSpec((R, N1, N2), lambda i: (i, 0, 0)), pl.BlockSpec((R, LANES), lambda i: (i, 0)),
                  _full(fah.shape), _full(fal.shape), _full(twr.shape), _full(twi.shape),
                  _full(m2h.shape), _full(m2l.shape)],
        out_specs=[pl.BlockSpec((R, N1, N2), lambda i: (i, 0, 0))] * 2,
        out_shape=[jax.ShapeDtypeStruct((OC, N1, N2), F32)] * 2,
        scratch_shapes=[pltpu.VMEM((R * N1, 2 * N2), F32)],
        compiler_params=_cparams(("parallel",)))(ktd.reshape(OC, N1, N2), ksum, fah, fal, twr, twi, m2h, m2l)


def _hyconv_kernel(cw_ref, cb_ref, sk_ref, u_ref, kr_ref, ki_ref, fah_ref, fal_ref, fgh_ref, fgl_ref,
                   twr_ref, twi_ref, mfh_ref, mfl_ref, mih_ref, mil_ref, o_ref,
                   z_ref, x1_ref, x2_ref, p_ref, q_ref, *, R, N1, N2, G):
    N1h = N1 // 2
    cb = pl.program_id(0)
    rows = R * N1h
    ng = R // G
    mi = G * N1h
    mo = G * N1
    lane = lax.broadcasted_iota(jnp.int32, (N1h, N2), 1)
    sub = lax.broadcasted_iota(jnp.int32, (N1h, N2), 0)
    first = jnp.logical_and(lane == 0, sub == 0)
    last = jnp.logical_and(lane == N2 - 1, sub == N1h - 1)

    def shortconv(u, part, ch):
        rl = pltpu.roll(u, 1, 1)
        prev = jnp.where(lane == 0, pltpu.roll(rl, 1, 0), rl)
        prev = jnp.where(first, 0.0, prev)
        rr = pltpu.roll(u, N2 - 1, 1)
        nxt = jnp.where(lane == N2 - 1, pltpu.roll(rr, N1h - 1, 0), rr)
        nxt = jnp.where(last, 0.0, nxt)
        c = part * HY_CH + ch
        return prev * cw_ref[0, c] + u * cw_ref[1, c] + nxt * cw_ref[2, c] + cb_ref[c]

    def prep(r, carry):
        ch = cb * R + r
        g = r // G
        rg = r % G
        for pbatch in range(2):
            base = pl.multiple_of((g * 2 + pbatch) * mi + rg * N1h, 8)
            z_ref[pl.ds(base, N1h), :] = shortconv(u_ref[0, pbatch, 0, r], 0, ch)
            x1_ref[pl.ds(base, N1h), :] = shortconv(u_ref[0, pbatch, 1, r], 1, ch)
            x2_ref[pl.ds(base, N1h), :] = shortconv(u_ref[0, pbatch, 2, r], 2, ch)
        return carry

    lax.fori_loop(0, R, prep, 0)

    def conv(order):
        twr, twi = twr_ref[...], twi_ref[...]
        for g in range(ng):
            a = _dot3_cl(fah_ref[...], fal_ref[...], z_ref[g * 2 * mi:(g + 1) * 2 * mi, :])
            ar, ai = a[:mo], a[mo:]
            p_ref[g * mo:(g + 1) * mo, 0:N2] = ar * twr - ai * twi
            p_ref[g * mo:(g + 1) * mo, N2:2 * N2] = ar * twi + ai * twr
        b = _dot3_cr(p_ref[...], mfh_ref[...], mfl_ref[...])
        br, bi = b[:, :N2], b[:, N2:]
        kr = kr_ref[order].reshape(R * N1, N2)
        ki = ki_ref[order].reshape(R * N1, N2)
        p_ref[:, 0:N2] = br * kr - bi * ki
        p_ref[:, N2:2 * N2] = br * ki + bi * kr
        d = _dot3_cr(p_ref[...], mih_ref[...], mil_ref[...])
        dr, di = d[:, :N2], d[:, N2:]
        for g in range(ng):
            sl = slice(g * mo, (g + 1) * mo)
            q_ref[0:mo, :] = dr[sl] * twr + di[sl] * twi
            q_ref[mo:2 * mo, :] = di[sl] * twr - dr[sl] * twi
            p_ref[g * 2 * mi:(g + 1) * 2 * mi, 0:N2] = _dot3_cl(fgh_ref[...], fgl_ref[...], q_ref[...])

    def gate(order, x_ref, final):
        def body(r, carry):
            ch = cb * R + r
            g = r // G
            rg = r % G
            skv = sk_ref[order, ch]
            for pbatch in range(2):
                base = pl.multiple_of((g * 2 + pbatch) * mi + rg * N1h, 8)
                zc = p_ref[pl.ds(base, N1h), 0:N2]
                zn = x_ref[pl.ds(base, N1h), :] * (zc + skv * z_ref[pl.ds(base, N1h), :])
                if final:
                    o_ref[0, pbatch, r] = zn
                else:
                    z_ref[pl.ds(base, N1h), :] = zn
            return carry
        lax.fori_loop(0, R, body, 0)

    conv(0)
    gate(0, x1_ref, False)
    conv(1)
    gate(1, x2_ref, True)


def _hyena_conv(ucm, conv_w, conv_b, skip, kr, ki, fc):
    B = ucm.shape[0]
    C = HY_CH
    N1, N2, G = fc["N1"], fc["N2"], fc["G"]
    N1h = N1 // 2
    R = 1024 // N1
    u = ucm.reshape(B // 2, 2, 3, C, N1h, N2)
    twr = jnp.asarray(np.tile(fc["twr"], (G, 1)))
    twi = jnp.asarray(np.tile(fc["twi"], (G, 1)))
    consts = [*fc["fa"], *fc["fg"], twr, twi, *fc["m2f"], *fc["m2i"]]
    rows = R * N1h
    smem = pl.BlockSpec(memory_space=pltpu.SMEM)
    out = pl.pallas_call(
        functools.partial(_hyconv_kernel, R=R, N1=N1, N2=N2, G=G), name="hyconv",
        grid=(C // R, B // 2),
        in_specs=[smem, smem, smem,
                  pl.BlockSpec((1, 2, 3, R, N1h, N2), lambda c, p: (p, 0, 0, c, 0, 0)),
                  pl.BlockSpec((HY_ORDER, R, N1, N2), lambda c, p: (0, c, 0, 0)),
                  pl.BlockSpec((HY_ORDER, R, N1, N2), lambda c, p: (0, c, 0, 0))]
                 + [_full(a.shape) for a in consts],
        out_specs=pl.BlockSpec((1, 2, R, N1h, N2), lambda c, p: (p, 0, c, 0, 0)),
        out_shape=jax.ShapeDtypeStruct((B // 2, 2, C, N1h, N2), F32),
        scratch_shapes=[pltpu.VMEM((2 * rows, N2), F32)] * 3
                       + [pltpu.VMEM((R * N1, 2 * N2), F32), pltpu.VMEM((2 * G * N1, N2), F32)],
        compiler_params=_cparams(("parallel", "arbitrary")))(
            conv_w, conv_b, skip, u, kr.reshape(HY_ORDER, C, N1, N2), ki.reshape(HY_ORDER, C, N1, N2), *consts)
    return out.reshape(B, C, N1h * N2)


def _lambda_init(layer):
    return 0.8 - 0.6 * math.exp(-0.3 * layer)


def _tile_gain(g, reps, scale=1.0):
    return jnp.tile(g.astype(F32), reps) * scale


def _trunk(x, mem, p, shared):
    B, L, D = x.shape
    M = B * L
    scale = HEAD_DIM ** -0.5
    x2 = x.reshape(M, D)
    for layer in range(DEPTH):
        j = layer // 2
        if layer % 2 == 0:
            w_in = p["w_in_even"][j].astype(BF16)
            k_gain = _tile_gain(p["da_q_gain"][j] * p["da_k_gain"][j], DA_QK // HEAD_DIM, scale * LOG2E)
            w_cm = jnp.concatenate([w_in[:, :DA_QK], w_in[:, 2 * DA_QK:]], axis=1).T
            kk, qt, vt, ucm = _inproj_even(x2, p["norm_mix"][layer], w_in[:, DA_QK:2 * DA_QK], k_gain, w_cm,
                                           batch=B, tm=shared["da_T"])
            oa = _diffattn(qt, kk.reshape(B, L, DA_QK), vt, shared["da_bias"],
                           p["da_lambda"][j], p["da_sub_gain"][j], _lambda_init(layer))
            fc = _fft_consts(L)
            ktd, ksum = _hyena_filters_td(L, p["hy_w1"][j], p["hy_b1"][j], p["hy_freq"][j], p["hy_w2"][j],
                                          p["hy_b2"][j], p["hy_w3"][j])
            kr, ki = _hyena_spectra(ktd, ksum, fc)
            ob = _hyena_conv(ucm, p["hy_conv_w"][j], p["hy_conv_b"][j], p["hy_skip"][j], kr, ki, fc)
            w_out = p["w_out_even"][j].astype(BF16)
            x2 = _outproj(x2, oa.reshape(M, DA_V), ob, w_out[:DA_V], w_out[DA_V:], b_cm=True, batch=B)
        else:
            w_in = p["w_in_odd"][j].astype(BF16)
            c0 = 3 * NA_W
            w_gn = jnp.concatenate([w_in[:, :2 * NA_W], w_in[:, c0:c0 + WG_Q + WG_KV]], axis=1)
            w_pl = jnp.concatenate([w_in[:, 2 * NA_W:3 * NA_W], w_in[:, c0 + WG_Q + WG_KV:]], axis=1)
            gain = jnp.concatenate([_tile_gain(p["na_q_gain"][j], NA_HEADS, scale),
                                    _tile_gain(p["na_k_gain"][j], NA_HEADS),
                                    _tile_gain(p["wg_q_gain"][j], WG_HEADS, scale),
                                    _tile_gain(p["wg_k_gain"][j], WG_KV_HEADS)])
            qk, vv = _inproj(x2, p["norm_mix"][layer], w_gn, gain, w_pl)
            qk = qk.reshape(B, L, -1)
            vv = vv.reshape(B, L, -1)
            oc = _natten(qk, vv, shared["na_bias"][j])
            od = _wgqa(qk, vv, shared["wg_bias"], p["wg_sink"][j], q_col=2 * NA_W // WG_Q,
                       k_col=(2 * NA_W + WG_Q) // LANES, v_col=NA_W // LANES)
            w_out = p["w_out_odd"][j].astype(BF16)
            x2 = _outproj(x2, oc.reshape(M, NA_W), od.reshape(M, WG_Q), w_out[:NA_W], w_out[NA_W:], b_cm=False)
        kn, vn = _memkv(mem, p["norm_memkv"][layer], p["mem_wkv"][layer].astype(BF16),
                        _tile_gain(p["mem_k_gain"][layer], MEM_HEADS))
        x3 = _memattn(x2.reshape(B, L, D), kn, vn, p["norm_mem"][layer], p["mem_wq"][layer].astype(BF16),
                      _tile_gain(p["mem_q_gain"][layer], MEM_HEADS, scale), p["mem_wo"][layer].astype(BF16))
        x2 = _mlp(x3.reshape(M, D), p["norm_mlp"][layer], p["mlp_w1"][layer].astype(BF16),
                  p["mlp_w2"][layer].astype(BF16))
    return x2.reshape(B, L, D)


def _shared_tables(p, da_T):
    idx = np.arange(da_T)
    rel_da = np.stack([d * da_T + idx[:, None] - idx[None, :] for d in (-1, 0, 1)])
    qi = np.arange(WG_BLOCK)
    ki = np.arange(3 * WG_BLOCK)
    rel_wg = ki[None, :] - WG_BLOCK - qi[:, None]
    mask_wg = np.where(np.abs(rel_wg) <= WG_WINDOW, 0.0, NEG_INF).astype(np.float32)
    return dict(da_T=da_T,
                da_bias=_t5bias(p["t5_table"], rel_da, scale=LOG2E),
                wg_bias=_t5bias(p["t5_table"], rel_wg, mask_wg),
                na_bias=[_nabias(p["na_rpb"][j]) for j in range(p["na_rpb"].shape[0])])


def kernel(x_prompt, x_sample, mem_prompt, mem_sample, t5_table, norm_mix, norm_mem, norm_memkv, norm_mlp, w_in_even, da_q_gain, da_k_gain, da_lambda, da_sub_gain, hy_conv_w, hy_conv_b, hy_w1, hy_b1, hy_freq, hy_w2, hy_b2, hy_w3, hy_skip, w_out_even, w_in_odd, na_q_gain, na_k_gain, na_rpb, wg_q_gain, wg_k_gain, wg_sink, w_out_odd, mem_wq, mem_wkv, mem_wo, mem_q_gain, mem_k_gain, mlp_w1, mlp_w2):
    p = dict(t5_table=t5_table, norm_mix=norm_mix, norm_mem=norm_mem, norm_memkv=norm_memkv, norm_mlp=norm_mlp,
             w_in_even=w_in_even, da_q_gain=da_q_gain, da_k_gain=da_k_gain, da_lambda=da_lambda,
             da_sub_gain=da_sub_gain, hy_conv_w=hy_conv_w, hy_conv_b=hy_conv_b, hy_w1=hy_w1, hy_b1=hy_b1,
             hy_freq=hy_freq, hy_w2=hy_w2, hy_b2=hy_b2, hy_w3=hy_w3, hy_skip=hy_skip, w_out_even=w_out_even,
             w_in_odd=w_in_odd, na_q_gain=na_q_gain, na_k_gain=na_k_gain, na_rpb=na_rpb, wg_q_gain=wg_q_gain,
             wg_k_gain=wg_k_gain, wg_sink=wg_sink, w_out_odd=w_out_odd, mem_wq=mem_wq, mem_wkv=mem_wkv,
             mem_wo=mem_wo, mem_q_gain=mem_q_gain, mem_k_gain=mem_k_gain, mlp_w1=mlp_w1, mlp_w2=mlp_w2)
    shared = _shared_tables(p, da_T=512)
    return (_trunk(x_prompt, mem_prompt, p, shared), _trunk(x_sample, mem_sample, p, shared))
```

```python
import functools
import math

import numpy as np
import jax
import jax.numpy as jnp
from jax import lax
from jax.experimental import pallas as pl
from jax.experimental.pallas import tpu as pltpu

F32 = jnp.float32
BF16 = jnp.bfloat16

D_MODEL = 1024
DEPTH = 2
HEAD_DIM = 64
DA_HEADS = 4
DA_VDIM = 2 * HEAD_DIM
DA_QK = DA_HEADS * 2 * HEAD_DIM
DA_V = DA_HEADS * DA_VDIM
HY_CH = D_MODEL // 2
HY_ORDER = 2
HY_BANDS = 8
HY_HIDDEN = 64
HY_FAST_DECAY = 0.3
HY_SLOW_DECAY = 1.5
HY_TARGET = 1e-2
NA_HEADS = 8
GRID_W = 64
NA_ROWS = 8
NA_COLS = 16
NA_W = NA_HEADS * HEAD_DIM
WG_HEADS = 8
WG_KV_HEADS = 2
WG_WINDOW = 128
WG_BLOCK = 128
WG_Q = WG_HEADS * HEAD_DIM
WG_KV = WG_KV_HEADS * HEAD_DIM
T5_BUCKETS = 32
T5_MAX_DIST = 128
T5_HEADS = 8
MEM_HEADS = 4
MEM_W = MEM_HEADS * HEAD_DIM
D_FF = 4 * D_MODEL
EPS = 1e-6
NEG_INF = -1e30
LOG2E = 1.4426950408889634

LANES = 128
FFT_LANES = 256
VMEM_LIMIT = 56 * 1024 * 1024


def _cparams(sem, vmem=None):
    return pltpu.CompilerParams(dimension_semantics=sem, vmem_limit_bytes=vmem or VMEM_LIMIT)


def _full(shape):
    n = len(shape)
    return pl.BlockSpec(shape, lambda *_: (0,) * n)


def _dot(a, b):
    return jnp.dot(a, b, preferred_element_type=F32)


def _dot_nt(a, b):
    return lax.dot_general(a, b, (((1,), (1,)), ((), ())), preferred_element_type=F32)


def _group_gmat():
    g = np.arange(LANES) // HEAD_DIM
    return jnp.asarray((g[:, None] == g[None, :]).astype(np.float32), dtype=BF16)


def _rms_rows(x, g):
    ms = jnp.mean(x * x, axis=-1, keepdims=True)
    return x * lax.rsqrt(ms + EPS) * g


def _group_rms(y, gmat, gain):
    outs = []
    for c in range(y.shape[1] // LANES):
        yc = y[:, c * LANES:(c + 1) * LANES]
        ss = _dot((yc * yc).astype(BF16), gmat)
        outs.append(yc * lax.rsqrt(ss * (1.0 / HEAD_DIM) + EPS) * gain[:, c * LANES:(c + 1) * LANES])
    return outs


def _inproj_odd_kernel(x_ref, g_ref, wk_ref, gain_ref, gmat_ref, wv_ref, wq_ref, k_ref, v_ref, qt_ref):
    hn = _rms_rows(x_ref[...], g_ref[...]).astype(BF16)
    yk = _dot(hn, wk_ref[...])
    for c, yc in enumerate(_group_rms(yk, gmat_ref[...], gain_ref[...])):
        k_ref[:, c * LANES:(c + 1) * LANES] = yc.astype(k_ref.dtype)
    v_ref[...] = _dot(hn, wv_ref[...]).astype(v_ref.dtype)
    yt = _dot_nt(wq_ref[...], hn)
    nq, tm = yt.shape
    q = yt.reshape(nq // HEAD_DIM, HEAD_DIM, tm)
    ms = jnp.mean(q * q, axis=1, keepdims=True)
    qt_ref[0] = (q * lax.rsqrt(ms + EPS)).reshape(nq, tm).astype(qt_ref.dtype)


def _inproj_odd(x2d, g, w_k, k_gain, w_v, w_q, batch, tm=512):
    M, D = x2d.shape
    L = M // batch
    nt = L // tm
    n_k, n_v, n_q = w_k.shape[1], w_v.shape[1], w_q.shape[0]
    return pl.pallas_call(
        _inproj_odd_kernel, name="inproj_odd", grid=(M // tm,),
        in_specs=[pl.BlockSpec((tm, D), lambda i: (i, 0)), _full((1, D)), _full((D, n_k)), _full((1, n_k)),
                  _full((LANES, LANES)), _full((D, n_v)), _full((n_q, D))],
        out_specs=[pl.BlockSpec((tm, n_k), lambda i: (i, 0)), pl.BlockSpec((tm, n_v), lambda i: (i, 0)),
                   pl.BlockSpec((1, n_q, tm), lambda i: (i // nt, 0, i % nt))],
        out_shape=[jax.ShapeDtypeStruct((M, n_k), BF16), jax.ShapeDtypeStruct((M, n_v), BF16),
                   jax.ShapeDtypeStruct((batch, n_q, L), BF16)],
        compiler_params=_cparams(("parallel",)))(
            x2d, g.reshape(1, D), w_k, k_gain.reshape(1, n_k), _group_gmat(), w_v, w_q)


def _inproj_even_kernel(x_ref, g_ref, wk_ref, gain_ref, gmat_ref, wcm_ref, k_ref, qt_ref, vt_ref, ut_ref):
    hn = _rms_rows(x_ref[...], g_ref[...]).astype(BF16)
    yk = _dot(hn, wk_ref[...])
    for c, yc in enumerate(_group_rms(yk, gmat_ref[...], gain_ref[...])):
        k_ref[:, c * LANES:(c + 1) * LANES] = yc.astype(k_ref.dtype)
    yt = _dot_nt(wcm_ref[...], hn)
    tm = yt.shape[1]
    q = yt[:DA_QK].reshape(DA_QK // HEAD_DIM, HEAD_DIM, tm)
    ms = jnp.mean(q * q, axis=1, keepdims=True)
    qt_ref[0, 0] = (q * lax.rsqrt(ms + EPS)).reshape(DA_QK, tm).astype(qt_ref.dtype)
    vt_ref[0, 0] = yt[DA_QK:DA_QK + DA_V].astype(vt_ref.dtype)
    ut_ref[0] = yt[DA_QK + DA_V:]


def _inproj_even(x2d, g, w_k, k_gain, w_cm, batch, tm):
    M, D = x2d.shape
    L = M // batch
    nt = L // tm
    n_u = w_cm.shape[0] - DA_QK - DA_V
    return pl.pallas_call(
        _inproj_even_kernel, name="inproj_even", grid=(M // tm,),
        in_specs=[pl.BlockSpec((tm, D), lambda i: (i, 0)), _full((1, D)), _full((D, DA_QK)), _full((1, DA_QK)),
                  _full((LANES, LANES)), _full(w_cm.shape)],
        out_specs=[pl.BlockSpec((tm, DA_QK), lambda i: (i, 0)),
                   pl.BlockSpec((1, 1, DA_QK, tm), lambda i: (i // nt, i % nt, 0, 0)),
                   pl.BlockSpec((1, 1, DA_V, tm), lambda i: (i // nt, i % nt, 0, 0)),
                   pl.BlockSpec((1, n_u, tm), lambda i: (i // nt, 0, i % nt))],
        out_shape=[jax.ShapeDtypeStruct((M, DA_QK), BF16),
                   jax.ShapeDtypeStruct((batch, nt, DA_QK, tm), BF16),
                   jax.ShapeDtypeStruct((batch, nt, DA_V, tm), BF16),
                   jax.ShapeDtypeStruct((batch, n_u, L), F32)],
        compiler_params=_cparams(("parallel",)))(
            x2d, g.reshape(1, D), w_k, k_gain.reshape(1, DA_QK), _group_gmat(), w_cm)


def _outproj_kernel(x_ref, a_ref, b_ref, wa_ref, wb_ref, o_ref, *, b_cm):
    b = b_ref[0].T.astype(BF16) if b_cm else b_ref[...]
    o_ref[...] = x_ref[...] + _dot(a_ref[...], wa_ref[...]) + _dot(b, wb_ref[...])


def _outproj(x2d, a, b, wa, wb, *, b_cm, batch=None, tm=512):
    M, D = x2d.shape
    if b_cm:
        L = M // batch
        nt = L // tm
        b_spec = pl.BlockSpec((1, b.shape[1], tm), lambda i: (i // nt, 0, i % nt))
    else:
        b_spec = pl.BlockSpec((tm, b.shape[1]), lambda i: (i, 0))
    return pl.pallas_call(
        functools.partial(_outproj_kernel, b_cm=b_cm), name="outproj",
        grid=(M // tm,),
        in_specs=[pl.BlockSpec((tm, D), lambda i: (i, 0)), pl.BlockSpec((tm, a.shape[1]), lambda i: (i, 0)),
                  b_spec, _full(wa.shape), _full(wb.shape)],
        out_specs=pl.BlockSpec((tm, D), lambda i: (i, 0)),
        out_shape=jax.ShapeDtypeStruct((M, D), F32),
        compiler_params=_cparams(("parallel",)))(x2d, a, b, wa, wb)


def _mlp_kernel(x_ref, g_ref, w1_ref, w2_ref, o_ref, hn_ref):
    j = pl.program_id(1)

    @pl.when(j == 0)
    def _():
        x = x_ref[...]
        hn_ref[...] = _rms_rows(x, g_ref[...]).astype(BF16)
        o_ref[...] = x

    h = _dot(hn_ref[...], w1_ref[...])
    h = jnp.square(jnp.maximum(h, 0.0))
    o_ref[...] += _dot(h.astype(BF16), w2_ref[...])


def _mlp(x2d, g, w1, w2, tm=1024, tf=512):
    M, D = x2d.shape
    F = w1.shape[1]
    tm = min(tm, M)
    return pl.pallas_call(
        _mlp_kernel, name="mlp",
        grid=(M // tm, F // tf),
        in_specs=[pl.BlockSpec((tm, D), lambda i, j: (i, 0)), _full((1, D)),
                  pl.BlockSpec((D, tf), lambda i, j: (0, j)), pl.BlockSpec((tf, D), lambda i, j: (j, 0))],
        out_specs=pl.BlockSpec((tm, D), lambda i, j: (i, 0)),
        out_shape=jax.ShapeDtypeStruct((M, D), F32),
        scratch_shapes=[pltpu.VMEM((tm, D), BF16)],
        compiler_params=_cparams(("parallel", "arbitrary")))(x2d, g.reshape(1, D), w1, w2)


def _memkv_kernel(m_ref, g_ref, w_ref, gain_ref, gmat_ref, k_ref, v_ref):
    mn = _rms_rows(m_ref[0], g_ref[...]).astype(BF16)
    kv = _dot(mn, w_ref[...])
    for c, kc in enumerate(_group_rms(kv[:, :MEM_W], gmat_ref[...], gain_ref[...])):
        k_ref[0, :, c * LANES:(c + 1) * LANES] = kc.astype(BF16)
    v_ref[0] = kv[:, MEM_W:].astype(BF16)


def _memkv(mem, g, wkv, kgain):
    B, M, D = mem.shape
    return pl.pallas_call(
        _memkv_kernel, name="memkv", grid=(B,),
        in_specs=[pl.BlockSpec((1, M, D), lambda b: (b, 0, 0)), _full((1, D)), _full(wkv.shape),
                  _full((1, MEM_W)), _full((LANES, LANES))],
        out_specs=[pl.BlockSpec((1, M, MEM_W), lambda b: (b, 0, 0))] * 2,
        out_shape=[jax.ShapeDtypeStruct((B, M, MEM_W), BF16)] * 2,
        compiler_params=_cparams(("parallel",)))(mem, g.reshape(1, D), wkv, kgain.reshape(1, MEM_W), _group_gmat())


def _memattn_kernel(x_ref, g_ref, wq_ref, qgain_ref, gmat_ref, k_ref, v_ref, wo_ref, o_ref, oh_ref):
    x = x_ref[0]
    hn = _rms_rows(x, g_ref[...]).astype(BF16)
    q = _dot(hn, wq_ref[...])
    qn = jnp.concatenate(_group_rms(q, gmat_ref[...], qgain_ref[...]), axis=1).astype(BF16)
    k = k_ref[0]
    v = v_ref[0]
    for h in range(MEM_HEADS):
        sl = slice(h * HEAD_DIM, (h + 1) * HEAD_DIM)
        s = _dot_nt(qn[:, sl], k[:, sl])
        e = jnp.exp(s - jnp.max(s, axis=-1, keepdims=True))
        p = e / jnp.sum(e, axis=-1, keepdims=True)
        oh_ref[:, sl] = _dot(p.astype(BF16), v[:, sl])
    o_ref[0] = x + _dot(oh_ref[...].astype(BF16), wo_ref[...])


def _memattn(x, kn, vn, g, wq, qgain, wo, tm=512):
    B, L, D = x.shape
    M = kn.shape[1]
    return pl.pallas_call(
        _memattn_kernel, name="memattn", grid=(B, L // tm),
        in_specs=[pl.BlockSpec((1, tm, D), lambda b, i: (b, i, 0)), _full((1, D)), _full(wq.shape),
                  _full((1, MEM_W)), _full((LANES, LANES)),
                  pl.BlockSpec((1, M, MEM_W), lambda b, i: (b, 0, 0)),
                  pl.BlockSpec((1, M, MEM_W), lambda b, i: (b, 0, 0)), _full(wo.shape)],
        out_specs=pl.BlockSpec((1, tm, D), lambda b, i: (b, i, 0)),
        out_shape=jax.ShapeDtypeStruct((B, L, D), F32),
        scratch_shapes=[pltpu.VMEM((tm, MEM_W), F32)],
        compiler_params=_cparams(("parallel", "parallel")))(
            x, g.reshape(1, D), wq, qgain.reshape(1, MEM_W), _group_gmat(), kn, vn, wo)


def _t5_bucket_np(rel):
    half = T5_BUCKETS // 2
    exact = half // 2
    n = np.abs(rel)
    nf = np.maximum(n, 1).astype(np.float64)
    large = exact + (np.log(nf / exact) / math.log(T5_MAX_DIST / exact) * (half - exact)).astype(np.int32)
    large = np.minimum(large, half - 1)
    return (np.where(rel > 0, half, 0) + np.where(n < exact, n, large)).astype(np.int32)


def _t5bias_kernel(table_ref, bucket_ref, mask_ref, o_ref, *, scale):
    h = pl.program_id(0)
    bucket = bucket_ref[...]
    acc = jnp.zeros(bucket.shape, F32)
    for b in range(T5_BUCKETS):
        acc = jnp.where(bucket == b, table_ref[b, h], acc)
    o_ref[0] = acc * scale + mask_ref[...]


def _t5bias(table, rel, mask=None, scale=1.0):
    bucket = jnp.asarray(_t5_bucket_np(rel))
    mask = jnp.zeros(rel.shape, F32) if mask is None else jnp.asarray(mask, F32)
    nd = rel.ndim
    return pl.pallas_call(
        functools.partial(_t5bias_kernel, scale=scale), name="t5bias", grid=(T5_HEADS,),
        in_specs=[pl.BlockSpec(memory_space=pltpu.SMEM), _full(rel.shape), _full(rel.shape)],
        out_specs=pl.BlockSpec((1,) + rel.shape, lambda h: (h,) + (0,) * nd),
        out_shape=jax.ShapeDtypeStruct((T5_HEADS,) + rel.shape, F32),
        compiler_params=_cparams(("arbitrary",)))(table, bucket, mask)


def _diffattn_kernel(lam_ref, cfar_ref, q_ref, k_ref, v_ref, bias_ref, sg_ref, o_ref, qp_ref, sa_ref, sb_ref,
                     m_ref, l_ref, acc_ref, *, lam_init, T, nkv):
    h = pl.program_id(1)
    i = pl.program_id(2)
    q = q_ref[0, 0]
    row = lax.broadcasted_iota(jnp.int32, q.shape, 0)
    zero = jnp.zeros(q.shape, q.dtype)
    qp_ref[0] = jnp.where(row < HEAD_DIM, q, zero)
    qp_ref[1] = jnp.where(row >= HEAD_DIM, q, zero)
    m_ref[...] = jnp.full(m_ref.shape, NEG_INF, F32)
    l_ref[...] = jnp.zeros(l_ref.shape, F32)
    acc_ref[...] = jnp.zeros(acc_ref.shape, F32)

    def scores(jj, s_ref):
        k = k_ref[0, pl.ds(pl.multiple_of(jj * T, T), T), :]
        for m in range(2):
            s_ref[m] = _dot(k, qp_ref[m])

    def step(jj, cur_ref, nxt_ref):
        d = jj - i
        near = jnp.abs(d) <= 1

        @pl.when(near)
        def _():
            for m in range(2):
                cur_ref[m] += bias_ref[m, d + 1]

        scores(jnp.minimum(jj + 1, nkv - 1), nxt_ref)
        v = v_ref[0, jj]
        for m in range(2):
            c = jnp.where(near, 0.0, jnp.where(d < 0, cfar_ref[0, 2 * h + m], cfar_ref[1, 2 * h + m]))
            s = cur_ref[m]
            m_prev = m_ref[m]
            m_new = jnp.maximum(m_prev, jnp.max(s, axis=0, keepdims=True) + c)
            alpha = jnp.exp2(m_prev - m_new)
            p = jnp.exp2(s - (m_new - c))
            l_ref[m] = alpha * l_ref[m] + jnp.sum(p, axis=0, keepdims=True)
            acc_ref[m] = alpha * acc_ref[m] + _dot(v, p.astype(BF16))
            m_ref[m] = m_new

    scores(0, sa_ref)

    def body(t, carry):
        step(2 * t, sa_ref, sb_ref)
        step(2 * t + 1, sb_ref, sa_ref)
        return carry

    lax.fori_loop(0, nkv // 2, body, 0)

    lf = lam_ref[...]
    lam = (jnp.exp(jnp.sum(lf[0:1] * lf[1:2], axis=-1, keepdims=True))
           - jnp.exp(jnp.sum(lf[2:3] * lf[3:4], axis=-1, keepdims=True)) + lam_init)
    o = acc_ref[0] / l_ref[0] - lam * (acc_ref[1] / l_ref[1])
    ms = jnp.mean(o * o, axis=0, keepdims=True)
    o = o * lax.rsqrt(ms + EPS) * sg_ref[...] * (1.0 - lam_init)
    o_ref[0] = o.T.astype(o_ref.dtype)


def _diffattn(qt, k, vt, bias, cfar, lam, sub_gain, lam_init):
    B, nb, _, T = qt.shape
    L = nb * T
    assert nb % 2 == 0
    return pl.pallas_call(
        functools.partial(_diffattn_kernel, lam_init=lam_init, T=T, nkv=nb), name="diffattn",
        grid=(B, DA_HEADS, nb),
        in_specs=[_full((4, HEAD_DIM)), pl.BlockSpec(memory_space=pltpu.SMEM),
                  pl.BlockSpec((1, 1, LANES, T), lambda b, h, i: (b, i, h, 0)),
                  pl.BlockSpec((1, L, LANES), lambda b, h, i: (b, 0, h)),
                  pl.BlockSpec((1, nb, LANES, T), lambda b, h, i: (b, 0, h, 0)),
                  pl.BlockSpec((2, 3, T, T), lambda b, h, i: (h, 0, 0, 0)),
                  _full((DA_VDIM, 1))],
        out_specs=pl.BlockSpec((1, T, LANES), lambda b, h, i: (b, i, h)),
        out_shape=jax.ShapeDtypeStruct((B, L, DA_V), BF16),
        scratch_shapes=[pltpu.VMEM((2, LANES, T), BF16), pltpu.VMEM((2, T, T), F32), pltpu.VMEM((2, T, T), F32),
                        pltpu.VMEM((2, 1, T), F32), pltpu.VMEM((2, 1, T), F32), pltpu.VMEM((2, DA_VDIM, T), F32)],
        compiler_params=_cparams(("parallel", "parallel", "arbitrary")))(
            lam, cfar, qt, k, vt, bias, sub_gain.reshape(DA_VDIM, 1))


def _dot_tn(a, b):
    return lax.dot_general(a, b, (((0,), (0,)), ((), ())), preferred_element_type=F32)


def _blockdiag_q(qa, qb):
    z = jnp.zeros(qa.shape, qa.dtype)
    return jnp.concatenate([jnp.concatenate([qa, z], axis=0), jnp.concatenate([z, qb], axis=0)], axis=1)


def _wgqa_kernel(sink_ref, q_ref, kp_ref, kc_ref, kn_ref, vp_ref, vc_ref, vn_ref, bias_ref, o_ref, kw_ref, vw_ref):
    n = pl.program_id(1)
    nb = pl.num_programs(1)
    W = WG_BLOCK
    kw_ref[0:W] = kp_ref[0]
    kw_ref[W:2 * W] = kc_ref[0]
    kw_ref[2 * W:3 * W] = kn_ref[0]
    vw_ref[0:W] = vp_ref[0]
    vw_ref[W:2 * W] = vc_ref[0]
    vw_ref[2 * W:3 * W] = vn_ref[0]
    key = lax.broadcasted_iota(jnp.int32, (3 * W, 2 * W), 0)
    valid = jnp.logical_and(jnp.logical_or(n > 0, key >= W), jnp.logical_or(n < nb - 1, key < 2 * W))
    lane2 = lax.broadcasted_iota(jnp.int32, (1, 2 * W), 1)
    lane1 = lax.broadcasted_iota(jnp.int32, (W, LANES), 1)
    grp = WG_HEADS // WG_KV_HEADS
    kw = kw_ref[...]
    vw = vw_ref[...]
    for h in range(grp):
        qa = q_ref[0, h * HEAD_DIM:(h + 1) * HEAD_DIM, :]
        qb = q_ref[0, (h + grp) * HEAD_DIM:(h + grp + 1) * HEAD_DIM, :]
        s = _dot(kw, _blockdiag_q(qa, qb))
        s = s + jnp.concatenate([bias_ref[h], bias_ref[h + grp]], axis=1)
        s = jnp.where(valid, s, NEG_INF)
        sk = jnp.where(lane2 < W, sink_ref[h], sink_ref[h + grp]) * LOG2E
        mx = jnp.maximum(jnp.max(s, axis=0, keepdims=True), sk)
        e = jnp.exp2(s - mx)
        den = jnp.sum(e, axis=0, keepdims=True) + jnp.exp2(sk - mx)
        p = (e * (1.0 / den)).astype(BF16)
        o2 = _dot_tn(p, vw)
        o_ref[0, :, h * LANES:(h + 1) * LANES] = jnp.where(lane1 < HEAD_DIM, o2[:W], o2[W:]).astype(o_ref.dtype)


def _wgqa(qt, kk, vv, bias, sink, q_row, k_col, v_col):
    B, L, _ = kk.shape
    W = WG_BLOCK
    nb = L // W
    prev = lambda n: jnp.maximum(n - 1, 0)
    nxt = lambda n: jnp.minimum(n + 1, nb - 1)
    return pl.pallas_call(
        _wgqa_kernel, name="wgqa", grid=(B, nb),
        in_specs=[pl.BlockSpec(memory_space=pltpu.SMEM),
                  pl.BlockSpec((1, WG_Q, W), lambda b, n: (b, q_row, n)),
                  pl.BlockSpec((1, W, LANES), lambda b, n: (b, prev(n), k_col)),
                  pl.BlockSpec((1, W, LANES), lambda b, n: (b, n, k_col)),
                  pl.BlockSpec((1, W, LANES), lambda b, n: (b, nxt(n), k_col)),
                  pl.BlockSpec((1, W, LANES), lambda b, n: (b, prev(n), v_col)),
                  pl.BlockSpec((1, W, LANES), lambda b, n: (b, n, v_col)),
                  pl.BlockSpec((1, W, LANES), lambda b, n: (b, nxt(n), v_col)),
                  _full(bias.shape)],
        out_specs=pl.BlockSpec((1, W, WG_Q), lambda b, n: (b, n, 0)),
        out_shape=jax.ShapeDtypeStruct((B, L, WG_Q), BF16),
        scratch_shapes=[pltpu.VMEM((3 * W, LANES), BF16), pltpu.VMEM((3 * W, LANES), BF16)],
        compiler_params=_cparams(("parallel", "arbitrary")))(sink, qt, kk, kk, kk, vv, vv, vv, bias)


NA_KROWS = NA_ROWS + 1
NA_CASES = 5
_NA_CASE_GEOM = (((0, 0), 7), ((0, 0), 5), ((0, 1), 3), ((1, 1), 2), ((1, 1), 0))


def _nabias_kernel(rpb_ref, o_ref):
    h = pl.program_id(0)
    cc = lax.broadcasted_iota(jnp.int32, (GRID_W, LANES), 0)
    lane = lax.broadcasted_iota(jnp.int32, (GRID_W, LANES), 1)
    c = lane % GRID_W
    second = lane >= GRID_W
    c_start = jnp.clip(c - NA_COLS // 2, 0, GRID_W - NA_COLS)
    valid = jnp.logical_and(cc >= c_start, cc < c_start + NA_COLS)
    dc = cc - c + (NA_COLS - 1)
    neg = jnp.full((GRID_W, LANES), NEG_INF, F32)
    tiles = {}
    for dr in range(1, 2 * NA_ROWS - 1):
        acc = neg
        for d in range(2 * NA_COLS - 1):
            val = jnp.where(second, rpb_ref[h, dr - 1, d], rpb_ref[h, dr, d]) * LOG2E
            acc = jnp.where(jnp.logical_and(valid, dc == d), val, acc)
        tiles[dr] = acc
    for case, (wstart, a) in enumerate(_NA_CASE_GEOM):
        for i in range(NA_KROWS):
            in0 = wstart[0] <= i < wstart[0] + NA_ROWS
            in1 = wstart[1] <= i < wstart[1] + NA_ROWS
            t = tiles[i + a] if (in0 or in1) else neg
            if in1 and not in0:
                t = jnp.where(second, t, neg)
            if in0 and not in1:
                t = jnp.where(second, neg, t)
            o_ref[case, 0, i * GRID_W:(i + 1) * GRID_W, :] = t


def _nabias(rpb):
    return pl.pallas_call(
        _nabias_kernel, name="nabias", grid=(NA_HEADS,),
        in_specs=[pl.BlockSpec(memory_space=pltpu.SMEM)],
        out_specs=pl.BlockSpec((NA_CASES, 1, NA_KROWS * GRID_W, LANES), lambda h: (0, h, 0, 0)),
        out_shape=jax.ShapeDtypeStruct((NA_CASES, NA_HEADS, NA_KROWS * GRID_W, LANES), F32),
        compiler_params=_cparams(("arbitrary",)))(rpb)


def _natten_kernel(q_ref, kp_ref, kc_ref, kn_ref, vp_ref, vc_ref, vn_ref, bias_ref, o_ref, kw_ref, vw_ref, *, rows):
    i = pl.program_id(1)
    RB = NA_ROWS
    T = RB * GRID_W
    KW = NA_KROWS * GRID_W
    kw_ref[0:T] = kp_ref[0]
    kw_ref[T:2 * T] = kc_ref[0]
    kw_ref[2 * T:3 * T] = kn_ref[0]
    vw_ref[0:T] = vp_ref[0]
    vw_ref[T:2 * T] = vc_ref[0]
    vw_ref[2 * T:3 * T] = vn_ref[0]
    lane1 = lax.broadcasted_iota(jnp.int32, (LANES, LANES), 1)
    for pi in range(RB // 2):
        r = i * RB + 2 * pi
        r_lo = jnp.clip(r - NA_ROWS // 2, 0, rows - NA_KROWS)
        off = pl.multiple_of((r_lo - i * RB + RB) * GRID_W, GRID_W)
        case = jnp.where(r == 0, 0, jnp.where(r == 2, 1, jnp.where(r == rows - 4, 3, jnp.where(r == rows - 2, 4, 2))))
        qs = slice(pi * LANES, (pi + 1) * LANES)
        for hp in range(NA_HEADS // 2):
            ha, hb = 2 * hp, 2 * hp + 1
            ls = slice(hp * LANES, (hp + 1) * LANES)
            kw = kw_ref[pl.ds(off, KW), ls]
            vw = vw_ref[pl.ds(off, KW), ls]
            qa = q_ref[0, ha * HEAD_DIM:(ha + 1) * HEAD_DIM, qs]
            qb = q_ref[0, hb * HEAD_DIM:(hb + 1) * HEAD_DIM, qs]
            s = _dot(kw, _blockdiag_q(qa, qb))
            s = s + jnp.concatenate([bias_ref[case, ha], bias_ref[case, hb]], axis=1)
            e = jnp.exp2(s - jnp.max(s, axis=0, keepdims=True))
            p = (e * (1.0 / jnp.sum(e, axis=0, keepdims=True))).astype(BF16)
            o2 = _dot_tn(p, vw)
            o_ref[0, qs, ls] = jnp.where(lane1 < HEAD_DIM, o2[:LANES], o2[LANES:]).astype(o_ref.dtype)


def _natten(qt, kk, vv, bias):
    B, L, _ = kk.shape
    rows = L // GRID_W
    assert rows >= NA_KROWS + 1 and rows % NA_ROWS == 0
    T = NA_ROWS * GRID_W
    nb = L // T
    prev = lambda n: jnp.maximum(n - 1, 0)
    nxt = lambda n: jnp.minimum(n + 1, nb - 1)
    return pl.pallas_call(
        functools.partial(_natten_kernel, rows=rows), name="natten", grid=(B, nb),
        in_specs=[pl.BlockSpec((1, NA_W, T), lambda b, n: (b, 0, n)),
                  pl.BlockSpec((1, T, NA_W), lambda b, n: (b, prev(n), 0)),
                  pl.BlockSpec((1, T, NA_W), lambda b, n: (b, n, 0)),
                  pl.BlockSpec((1, T, NA_W), lambda b, n: (b, nxt(n), 0)),
                  pl.BlockSpec((1, T, NA_W), lambda b, n: (b, prev(n), 0)),
                  pl.BlockSpec((1, T, NA_W), lambda b, n: (b, n, 0)),
                  pl.BlockSpec((1, T, NA_W), lambda b, n: (b, nxt(n), 0)),
                  _full(bias.shape)],
        out_specs=pl.BlockSpec((1, T, NA_W), lambda b, n: (b, n, 0)),
        out_shape=jax.ShapeDtypeStruct((B, L, NA_W), BF16),
        scratch_shapes=[pltpu.VMEM((3 * T, NA_W), BF16), pltpu.VMEM((3 * T, NA_W), BF16)],
        compiler_params=_cparams(("parallel", "arbitrary")))(qt, kk, kk, kk, vv, vv, vv, bias)


def _split(a):
    hi = a.astype(BF16)
    return hi, (a - hi.astype(F32)).astype(BF16)


def _np_split(a):
    a = np.asarray(a, np.float32)
    hi = a.astype(BF16)
    lo = (a - hi.astype(np.float32)).astype(BF16)
    return hi, lo


def _dot3_cl(ch, cl, x):
    xh, xl = _split(x)
    return _dot(ch, xh) + _dot(cl, xh) + _dot(ch, xl)


def _dot3_cr(x, ch, cl):
    xh, xl = _split(x)
    return _dot(xh, ch) + _dot(xl, ch) + _dot(xh, cl)


def _fft_consts(L):
    N2 = FFT_LANES
    N = 2 * L
    N1 = N // N2
    N1h = N1 // 2
    G = max(1, LANES // N1h)
    Gf = max(1, LANES // N1)
    k = np.arange(N1)
    ang1 = 2.0 * np.pi * np.outer(k, k) / N1
    C1, S1 = np.cos(ang1), np.sin(ang1)
    eye = np.eye
    KC, KS = np.kron(eye(G), C1[:, :N1h]), np.kron(eye(G), S1[:, :N1h])
    fa = np.block([[KC, KS], [-KS, KC]])
    KCh, KSh = np.kron(eye(G), C1[:N1h, :]), np.kron(eye(G), S1[:N1h, :])
    fg = np.block([[KCh, -KSh], [KSh, KCh]]) / N
    faf = np.concatenate([np.kron(eye(Gf), C1), -np.kron(eye(Gf), S1)], axis=0)
    n2 = np.arange(N2)
    angt = 2.0 * np.pi * np.outer(k, n2) / N
    twr, twi = np.cos(angt), -np.sin(angt)
    ang2 = 2.0 * np.pi * np.outer(n2, n2) / N2
    C2, S2 = np.cos(ang2), np.sin(ang2)
    m2f = np.block([[C2, -S2], [S2, C2]])
    m2i = np.block([[C2, S2], [-S2, C2]])
    return dict(N=N, N1=N1, N1h=N1h, N2=N2, G=G, Gf=Gf, fa=_np_split(fa), fg=_np_split(fg), faf=_np_split(faf),
                m2f=_np_split(m2f), m2i=_np_split(m2i),
                twr=np.asarray(twr, np.float32), twi=np.asarray(twi, np.float32))


def _hyfilt_kernel(w1t_ref, b1_ref, fr_ref, w2t_ref, b2_ref, w3t_ref, delta_ref, band_ref, o_ref, sum_ref, *, L, tn):
    s = pl.program_id(0)
    n = s * tn + lax.broadcasted_iota(jnp.int32, (1, tn), 1)
    t = jnp.where(n < L, n, 2 * L - n).astype(F32)
    t01 = t / float(max(L - 1, 1))
    w = (2.0 * math.pi) * t / float(L)
    ang = band_ref[...] * w
    cs, sn = jnp.cos(ang), -jnp.sin(ang)
    w1t = w1t_ref[...]
    h = w1t[:, 0:1] * t01
    for b in range(HY_BANDS):
        h = h + w1t[:, 1 + b:2 + b] * cs[b:b + 1] + w1t[:, 1 + HY_BANDS + b:2 + HY_BANDS + b] * sn[b:b + 1]
    fr = fr_ref[...]
    h = jnp.sin(fr[:, 0:1] * (h + b1_ref[...]))
    h = jnp.sin(fr[:, 1:2] * (jnp.dot(w2t_ref[...], h, preferred_element_type=F32,
                                      precision=lax.Precision.HIGHEST) + b2_ref[...]))
    y = jnp.dot(w3t_ref[0], h, preferred_element_type=F32, precision=lax.Precision.HIGHEST)
    y = y * jnp.exp(-delta_ref[...] * t01)
    y = jnp.where(n == L, 0.0, y)
    o_ref[...] = y
    a = jnp.abs(y)
    part = a[:, 0:LANES]
    for c in range(1, tn // LANES):
        part = part + a[:, c * LANES:(c + 1) * LANES]

    @pl.when(s == 0)
    def _():
        sum_ref[...] = jnp.zeros(sum_ref.shape, F32)

    sum_ref[...] += part


def _hyena_filters_td(L, w1, b1, freq, w2, b2, w3, tn=1024):
    OC = HY_ORDER * HY_CH
    w3t = jnp.transpose(w3.reshape(HY_HIDDEN, HY_ORDER, 2, HY_CH), (2, 1, 3, 0)).reshape(2, OC, HY_HIDDEN)
    min_decay = math.log(HY_TARGET) / HY_SLOW_DECAY
    max_decay = math.log(HY_TARGET) / HY_FAST_DECAY
    deltas = np.abs(np.linspace(min_decay, max_decay, HY_CH, dtype=np.float32))
    delta = jnp.asarray(np.tile(deltas, HY_ORDER).reshape(OC, 1))
    bands = jnp.asarray(np.linspace(1e-4, HY_BANDS - 1, HY_BANDS, dtype=np.float32).reshape(HY_BANDS, 1))
    N = 2 * L
    nh = L // tn
    return pl.pallas_call(
        functools.partial(_hyfilt_kernel, L=L, tn=tn), name="hyfilt", grid=(N // tn,),
        in_specs=[_full((HY_HIDDEN, 1 + 2 * HY_BANDS)), _full((HY_HIDDEN, 1)), _full((HY_HIDDEN, 2)),
                  _full((HY_HIDDEN, HY_HIDDEN)), _full((HY_HIDDEN, 1)),
                  pl.BlockSpec((1, OC, HY_HIDDEN), lambda s: (s // nh, 0, 0)),
                  _full((OC, 1)), _full((HY_BANDS, 1))],
        out_specs=[pl.BlockSpec((OC, tn), lambda s: (0, s)), _full((OC, LANES))],
        out_shape=[jax.ShapeDtypeStruct((OC, N), F32), jax.ShapeDtypeStruct((OC, LANES), F32)],
        compiler_params=_cparams(("arbitrary",)))(
            w1.T, b1.reshape(HY_HIDDEN, 1), freq.T, w2.T, b2.reshape(HY_HIDDEN, 1), w3t, delta, bands)


def _hyspec_kernel(k_ref, sum_ref, fah_ref, fal_ref, twr_ref, twi_ref, m2h_ref, m2l_ref, kr_ref, ki_ref, p_ref,
                   *, R, N1, N2, Gf):
    tot = jnp.sum(sum_ref[...], axis=-1, keepdims=True)
    mi = Gf * N1
    for g in range(R // Gf):
        xs = []
        for r in range(Gf):
            row = g * Gf + r
            xs.append(k_ref[row] / tot[row:row + 1, :])
        x = xs[0] if Gf == 1 else jnp.concatenate(xs, axis=0)
        a = _dot3_cl(fah_ref[...], fal_ref[...], x)
        ar, ai = a[:mi], a[mi:]
        twr, twi = twr_ref[...], twi_ref[...]
        p_ref[g * mi:(g + 1) * mi, 0:N2] = ar * twr - ai * twi
        p_ref[g * mi:(g + 1) * mi, N2:2 * N2] = ar * twi + ai * twr
    b = _dot3_cr(p_ref[...], m2h_ref[...], m2l_ref[...])
    kr_ref[...] = b[:, :N2].reshape(R, N1, N2)
    ki_ref[...] = b[:, N2:].reshape(R, N1, N2)


def _hyena_spectra(ktd, ksum, fc):
    OC = ktd.shape[0]
    N1, N2, Gf = fc["N1"], fc["N2"], fc["Gf"]
    R = 1024 // N1
    twr = jnp.asarray(np.tile(fc["twr"], (Gf, 1)))
    twi = jnp.asarray(np.tile(fc["twi"], (Gf, 1)))
    fah, fal = fc["faf"]
    m2h, m2l = fc["m2f"]
    return pl.pallas_call(
        functools.partial(_hyspec_kernel, R=R, N1=N1, N2=N2, Gf=Gf), name="hyspec", grid=(OC // R,),
        in_specs=[pl.BlockSpec((R, N1, N2), lambda i: (i, 0, 0)), pl.BlockSpec((R, LANES), lambda i: (i, 0)),
                  _full(fah.shape), _full(fal.shape), _full(twr.shape), _full(twi.shape),
                  _full(m2h.shape), _full(m2l.shape)],
        out_specs=[pl.BlockSpec((R, N1, N2), lambda i: (i, 0, 0))] * 2,
        out_shape=[jax.ShapeDtypeStruct((OC, N1, N2), F32)] * 2,
        scratch_shapes=[pltpu.VMEM((R * N1, 2 * N2), F32)],
        compiler_params=_cparams(("parallel",)))(ktd.reshape(OC, N1, N2), ksum, fah, fal, twr, twi, m2h, m2l)


def _hyconv_kernel(cw_ref, cb_ref, sk_ref, u_ref, kr_ref, ki_ref, fa_ref, fg_ref, twr_ref, twi_ref, mf_ref, mi_ref,
                   o_ref, z_ref, x1_ref, x2_ref, c_ref, p_ref, wcol_ref, *, R, N1, N2, G):
    N1h = N1 // 2
    cb = pl.program_id(0)
    ng = R // G
    mi = G * N1h
    mo = G * N1
    n_parts = 3
    i_bias = 3 * n_parts
    i_skip = i_bias + n_parts

    @pl.when(pl.program_id(1) == 0)
    def _():
        def fill(r, carry):
            ch = cb * R + r
            rs = pl.ds(pl.multiple_of(r * N1h, 8), N1h)
            for part in range(n_parts):
                c = part * HY_CH + ch
                for tap in range(3):
                    wcol_ref[tap * n_parts + part, rs, :] = jnp.full((N1h, LANES), cw_ref[tap, c], F32)
                wcol_ref[i_bias + part, rs, :] = jnp.full((N1h, LANES), cb_ref[c], F32)
            for o in range(HY_ORDER):
                wcol_ref[i_skip + o, rs, :] = jnp.full((N1h, LANES), sk_ref[o, ch], F32)
            return carry
        lax.fori_loop(0, R, fill, 0)

    def wide(idx, g):
        w = wcol_ref[idx, g * mi:(g + 1) * mi, :]
        return jnp.concatenate([w] * (N2 // LANES), axis=1)

    lane = lax.broadcasted_iota(jnp.int32, (mi, N2), 1)
    sub = lax.broadcasted_iota(jnp.int32, (mi, N2), 0) % N1h
    first = jnp.logical_and(lane == 0, sub == 0)
    last = jnp.logical_and(lane == N2 - 1, sub == N1h - 1)

    def shortconv(u, part, g):
        rl = pltpu.roll(u, 1, 1)
        prev = jnp.where(lane == 0, pltpu.roll(rl, 1, 0), rl)
        prev = jnp.where(first, 0.0, prev)
        rr = pltpu.roll(u, N2 - 1, 1)
        nxt = jnp.where(lane == N2 - 1, pltpu.roll(rr, mi - 1, 0), rr)
        nxt = jnp.where(last, 0.0, nxt)
        return (prev * wide(part, g) + u * wide(n_parts + part, g) + nxt * wide(2 * n_parts + part, g)
                + wide(i_bias + part, g))

    for g in range(ng):
        for pbatch in range(2):
            rs = slice((g * 2 + pbatch) * mi, (g * 2 + pbatch + 1) * mi)
            for part, dst in enumerate((z_ref, x1_ref, x2_ref)):
                u = u_ref[0, pbatch, part, g * G:(g + 1) * G].reshape(mi, N2)
                dst[rs, :] = shortconv(u, part, g)

    def conv(order):
        twr, twi = twr_ref[...], twi_ref[...]
        for g in range(ng):
            a = _dot(fa_ref[...], z_ref[g * 2 * mi:(g + 1) * 2 * mi, :].astype(BF16))
            ar, ai = a[:mo], a[mo:]
            p_ref[g * mo:(g + 1) * mo, 0:N2] = (ar * twr - ai * twi).astype(BF16)
            p_ref[g * mo:(g + 1) * mo, N2:2 * N2] = (ar * twi + ai * twr).astype(BF16)
        b = _dot(p_ref[...], mf_ref[...])
        br, bi = b[:, :N2], b[:, N2:]
        kr = kr_ref[order].reshape(R * N1, N2)
        ki = ki_ref[order].reshape(R * N1, N2)
        p_ref[:, 0:N2] = (br * kr - bi * ki).astype(BF16)
        p_ref[:, N2:2 * N2] = (br * ki + bi * kr).astype(BF16)
        d = _dot(p_ref[...], mi_ref[...])
        dr, di = d[:, :N2], d[:, N2:]
        for g in range(ng):
            sl = slice(g * mo, (g + 1) * mo)
            x = jnp.concatenate([dr[sl] * twr + di[sl] * twi, di[sl] * twr - dr[sl] * twi], axis=0)
            c_ref[g * 2 * mi:(g + 1) * 2 * mi, :] = _dot(fg_ref[...], x.astype(BF16))

    def gate(order, x_ref, final):
        for g in range(ng):
            skv = wide(i_skip + order, g)
            for pbatch in range(2):
                rs = slice((g * 2 + pbatch) * mi, (g * 2 + pbatch + 1) * mi)
                zn = x_ref[rs, :] * (c_ref[rs, :] + skv * z_ref[rs, :])
                if final:
                    o_ref[0, pbatch, g * G:(g + 1) * G] = zn.reshape(G, N1h, N2)
                else:
                    z_ref[rs, :] = zn

    conv(0)
    gate(0, x1_ref, False)
    conv(1)
    gate(1, x2_ref, True)


def _hyena_conv(ucm, conv_w, conv_b, skip, kr, ki, fc):
    B = ucm.shape[0]
    C = HY_CH
    N1, N2, G = fc["N1"], fc["N2"], fc["G"]
    N1h = N1 // 2
    R = 1024 // N1
    u = ucm.reshape(B // 2, 2, 3, C, N1h, N2)
    twr = jnp.asarray(np.tile(fc["twr"], (G, 1)))
    twi = jnp.asarray(np.tile(fc["twi"], (G, 1)))
    consts = [fc["fa"][0], fc["fg"][0], twr, twi, fc["m2f"][0], fc["m2i"][0]]
    rows = R * N1h
    smem = pl.BlockSpec(memory_space=pltpu.SMEM)
    out = pl.pallas_call(
        functools.partial(_hyconv_kernel, R=R, N1=N1, N2=N2, G=G), name="hyconv",
        grid=(C // R, B // 2),
        in_specs=[smem, smem, smem,
                  pl.BlockSpec((1, 2, 3, R, N1h, N2), lambda c, p: (p, 0, 0, c, 0, 0)),
                  pl.BlockSpec((HY_ORDER, R, N1, N2), lambda c, p: (0, c, 0, 0)),
                  pl.BlockSpec((HY_ORDER, R, N1, N2), lambda c, p: (0, c, 0, 0))]
                 + [_full(a.shape) for a in consts],
        out_specs=pl.BlockSpec((1, 2, R, N1h, N2), lambda c, p: (p, 0, c, 0, 0)),
        out_shape=jax.ShapeDtypeStruct((B // 2, 2, C, N1h, N2), F32),
        scratch_shapes=[pltpu.VMEM((2 * rows, N2), F32)] * 4
                       + [pltpu.VMEM((R * N1, 2 * N2), BF16), pltpu.VMEM((4 * 3 + HY_ORDER, rows, LANES), F32)],
        compiler_params=_cparams(("parallel", "arbitrary")))(
            conv_w, conv_b, skip, u, kr.reshape(HY_ORDER, C, N1, N2), ki.reshape(HY_ORDER, C, N1, N2), *consts)
    return out.reshape(B, C, N1h * N2)


def _lambda_init(layer):
    return 0.8 - 0.6 * math.exp(-0.3 * layer)


def _tile_gain(g, reps, scale=1.0):
    return jnp.tile(g.astype(F32), reps) * scale


def _trunk(x, mem, p, shared):
    B, L, D = x.shape
    M = B * L
    scale = HEAD_DIM ** -0.5
    x2 = x.reshape(M, D)
    for layer in range(DEPTH):
        j = layer // 2
        if layer % 2 == 0:
            w_in = p["w_in_even"][j].astype(BF16)
            k_gain = _tile_gain(p["da_q_gain"][j] * p["da_k_gain"][j], DA_QK // HEAD_DIM, scale * LOG2E)
            w_cm = jnp.concatenate([w_in[:, :DA_QK], w_in[:, 2 * DA_QK:]], axis=1).T
            kk, qt, vt, ucm = _inproj_even(x2, p["norm_mix"][layer], w_in[:, DA_QK:2 * DA_QK], k_gain, w_cm,
                                           batch=B, tm=shared["da_T"])
            oa = _diffattn(qt, kk.reshape(B, L, DA_QK), vt, shared["da_bias"], shared["da_cfar"],
                           p["da_lambda"][j], p["da_sub_gain"][j], _lambda_init(layer))
            fc = _fft_consts(L)
            ktd, ksum = _hyena_filters_td(L, p["hy_w1"][j], p["hy_b1"][j], p["hy_freq"][j], p["hy_w2"][j],
                                          p["hy_b2"][j], p["hy_w3"][j])
            kr, ki = _hyena_spectra(ktd, ksum, fc)
            ob = _hyena_conv(ucm, p["hy_conv_w"][j], p["hy_conv_b"][j], p["hy_skip"][j], kr, ki, fc)
            w_out = p["w_out_even"][j].astype(BF16)
            x2 = _outproj(x2, oa.reshape(M, DA_V), ob, w_out[:DA_V], w_out[DA_V:], b_cm=True, batch=B)
        else:
            w_in = p["w_in_odd"][j].astype(BF16)
            c0 = 3 * NA_W
            c1 = c0 + WG_Q
            c2 = c1 + WG_KV
            w_k = jnp.concatenate([w_in[:, NA_W:2 * NA_W], w_in[:, c1:c2]], axis=1)
            w_v = jnp.concatenate([w_in[:, 2 * NA_W:c0], w_in[:, c2:]], axis=1)
            w_q = jnp.concatenate([w_in[:, :NA_W], w_in[:, c0:c1]], axis=1).T
            k_gain = jnp.concatenate([
                _tile_gain(p["na_q_gain"][j] * p["na_k_gain"][j], NA_HEADS, scale * LOG2E),
                _tile_gain(p["wg_q_gain"][j] * p["wg_k_gain"][j], WG_KV_HEADS, scale * LOG2E)])
            kk, vv, qt = _inproj_odd(x2, p["norm_mix"][layer], w_k, k_gain, w_v, w_q, batch=B)
            kk = kk.reshape(B, L, -1)
            vv = vv.reshape(B, L, -1)
            oc = _natten(qt, kk, vv, shared["na_bias"][j])
            od = _wgqa(qt, kk, vv, shared["wg_bias"], p["wg_sink"][j], q_row=NA_W // WG_Q,
                       k_col=NA_W // LANES, v_col=NA_W // LANES)
            w_out = p["w_out_odd"][j].astype(BF16)
            grp = WG_HEADS // WG_KV_HEADS
            head_order = np.stack([np.arange(grp), np.arange(grp) + grp], axis=1).reshape(-1)
            wd_rows = NA_W + (head_order[:, None] * HEAD_DIM + np.arange(HEAD_DIM)[None, :]).reshape(-1)
            x2 = _outproj(x2, oc.reshape(M, NA_W), od.reshape(M, WG_Q), w_out[:NA_W], w_out[wd_rows], b_cm=False)
        kn, vn = _memkv(mem, p["norm_memkv"][layer], p["mem_wkv"][layer].astype(BF16),
                        _tile_gain(p["mem_k_gain"][layer], MEM_HEADS))
        x3 = _memattn(x2.reshape(B, L, D), kn, vn, p["norm_mem"][layer], p["mem_wq"][layer].astype(BF16),
                      _tile_gain(p["mem_q_gain"][layer], MEM_HEADS, scale), p["mem_wo"][layer].astype(BF16))
        x2 = _mlp(x3.reshape(M, D), p["norm_mlp"][layer], p["mlp_w1"][layer].astype(BF16),
                  p["mlp_w2"][layer].astype(BF16))
    return x2.reshape(B, L, D)


def _shared_tables(p, da_T):
    idx = np.arange(da_T)
    far = np.arange(T5_MAX_DIST, 1 << 20)
    assert da_T >= T5_MAX_DIST and (_t5_bucket_np(far) == _t5_bucket_np(far[:1])).all()
    assert (_t5_bucket_np(-far) == _t5_bucket_np(-far[:1])).all()
    rel_da = np.stack([d * da_T + idx[:, None] - idx[None, :] for d in (-1, 0, 1)])
    qi = np.arange(WG_BLOCK)
    ki = np.arange(3 * WG_BLOCK)
    rel_wg = ki[:, None] - WG_BLOCK - qi[None, :]
    mask_wg = np.where(np.abs(rel_wg) <= WG_WINDOW, 0.0, NEG_INF).astype(np.float32)
    return dict(da_T=da_T,
                da_bias=_t5bias(p["t5_table"], rel_da, scale=LOG2E),
                da_cfar=p["t5_table"][_t5_bucket_np(np.array([-T5_MAX_DIST, T5_MAX_DIST]))] * LOG2E,
                wg_bias=_t5bias(p["t5_table"], rel_wg, mask_wg, scale=LOG2E),
                na_bias=[_nabias(p["na_rpb"][j]) for j in range(p["na_rpb"].shape[0])])


def kernel(x_prompt, x_sample, mem_prompt, mem_sample, t5_table, norm_mix, norm_mem, norm_memkv, norm_mlp, w_in_even, da_q_gain, da_k_gain, da_lambda, da_sub_gain, hy_conv_w, hy_conv_b, hy_w1, hy_b1, hy_freq, hy_w2, hy_b2, hy_w3, hy_skip, w_out_even, w_in_odd, na_q_gain, na_k_gain, na_rpb, wg_q_gain, wg_k_gain, wg_sink, w_out_odd, mem_wq, mem_wkv, mem_wo, mem_q_gain, mem_k_gain, mlp_w1, mlp_w2):
    p = dict(t5_table=t5_table, norm_mix=norm_mix, norm_mem=norm_mem, norm_memkv=norm_memkv, norm_mlp=norm_mlp,
             w_in_even=w_in_even, da_q_gain=da_q_gain, da_k_gain=da_k_gain, da_lambda=da_lambda,
             da_sub_gain=da_sub_gain, hy_conv_w=hy_conv_w, hy_conv_b=hy_conv_b, hy_w1=hy_w1, hy_b1=hy_b1,
             hy_freq=hy_freq, hy_w2=hy_w2, hy_b2=hy_b2, hy_w3=hy_w3, hy_skip=hy_skip, w_out_even=w_out_even,
             w_in_odd=w_in_odd, na_q_gain=na_q_gain, na_k_gain=na_k_gain, na_rpb=na_rpb, wg_q_gain=wg_q_gain,
             wg_k_gain=wg_k_gain, wg_sink=wg_sink, w_out_odd=w_out_odd, mem_wq=mem_wq, mem_wkv=mem_wkv,
             mem_wo=mem_wo, mem_q_gain=mem_q_gain, mem_k_gain=mem_k_gain, mlp_w1=mlp_w1, mlp_w2=mlp_w2)
    shared = _shared_tables(p, da_T=512)
    return (_trunk(x_prompt, mem_prompt, p, shared), _trunk(x_sample, mem_sample, p, shared))
```

```python
import functools
import math

import numpy as np
import jax
import jax.numpy as jnp
from jax import lax
from jax.experimental import pallas as pl
from jax.experimental.pallas import tpu as pltpu

F32 = jnp.float32
BF16 = jnp.bfloat16

D_MODEL = 1024
DEPTH = 2
HEAD_DIM = 64
DA_HEADS = 4
DA_VDIM = 2 * HEAD_DIM
DA_QK = DA_HEADS * 2 * HEAD_DIM
DA_V = DA_HEADS * DA_VDIM
HY_CH = D_MODEL // 2
HY_ORDER = 2
HY_BANDS = 8
HY_HIDDEN = 64
HY_FAST_DECAY = 0.3
HY_SLOW_DECAY = 1.5
HY_TARGET = 1e-2
NA_HEADS = 8
GRID_W = 64
NA_ROWS = 8
NA_COLS = 16
NA_W = NA_HEADS * HEAD_DIM
WG_HEADS = 8
WG_KV_HEADS = 2
WG_WINDOW = 128
WG_BLOCK = 128
WG_Q = WG_HEADS * HEAD_DIM
WG_KV = WG_KV_HEADS * HEAD_DIM
T5_BUCKETS = 32
T5_MAX_DIST = 128
T5_HEADS = 8
MEM_HEADS = 4
MEM_W = MEM_HEADS * HEAD_DIM
D_FF = 4 * D_MODEL
EPS = 1e-6
NEG_INF = -1e30
LOG2E = 1.4426950408889634

DA_SUM_ROWS = 16
LANES = 128
FFT_LANES = 256
VMEM_LIMIT = 56 * 1024 * 1024


def _cparams(sem, vmem=None):
    return pltpu.CompilerParams(dimension_semantics=sem, vmem_limit_bytes=vmem or VMEM_LIMIT)


def _full(shape):
    n = len(shape)
    return pl.BlockSpec(shape, lambda *_: (0,) * n)


def _dot(a, b):
    return jnp.dot(a, b, preferred_element_type=F32)


def _dot_nt(a, b):
    return lax.dot_general(a, b, (((1,), (1,)), ((), ())), preferred_element_type=F32)


def _group_gmat():
    g = np.arange(LANES) // HEAD_DIM
    return jnp.asarray((g[:, None] == g[None, :]).astype(np.float32), dtype=BF16)


def _rms_rows(x, g):
    ms = jnp.mean(x * x, axis=-1, keepdims=True)
    return x * lax.rsqrt(ms + EPS) * g


def _group_rms(y, gmat, gain):
    outs = []
    for c in range(y.shape[1] // LANES):
        yc = y[:, c * LANES:(c + 1) * LANES]
        ss = _dot((yc * yc).astype(BF16), gmat)
        outs.append(yc * lax.rsqrt(ss * (1.0 / HEAD_DIM) + EPS) * gain[:, c * LANES:(c + 1) * LANES])
    return outs


def _inproj_odd_kernel(x_ref, g_ref, wk_ref, gain_ref, gmat_ref, wv_ref, wq_ref, k_ref, v_ref, qt_ref):
    hn = _rms_rows(x_ref[...], g_ref[...]).astype(BF16)
    yk = _dot(hn, wk_ref[...])
    for c, yc in enumerate(_group_rms(yk, gmat_ref[...], gain_ref[...])):
        k_ref[:, c * LANES:(c + 1) * LANES] = yc.astype(k_ref.dtype)
    v_ref[...] = _dot(hn, wv_ref[...]).astype(v_ref.dtype)
    yt = _dot_nt(wq_ref[...], hn)
    nq, tm = yt.shape
    q = yt.reshape(nq // HEAD_DIM, HEAD_DIM, tm)
    ms = jnp.mean(q * q, axis=1, keepdims=True)
    qt_ref[0] = (q * lax.rsqrt(ms + EPS)).reshape(nq, tm).astype(qt_ref.dtype)


def _inproj_odd(x2d, g, w_k, k_gain, w_v, w_q, batch, tm=512):
    M, D = x2d.shape
    L = M // batch
    nt = L // tm
    n_k, n_v, n_q = w_k.shape[1], w_v.shape[1], w_q.shape[0]
    return pl.pallas_call(
        _inproj_odd_kernel, name="inproj_odd", grid=(M // tm,),
        in_specs=[pl.BlockSpec((tm, D), lambda i: (i, 0)), _full((1, D)), _full((D, n_k)), _full((1, n_k)),
                  _full((LANES, LANES)), _full((D, n_v)), _full((n_q, D))],
        out_specs=[pl.BlockSpec((tm, n_k), lambda i: (i, 0)), pl.BlockSpec((tm, n_v), lambda i: (i, 0)),
                   pl.BlockSpec((1, n_q, tm), lambda i: (i // nt, 0, i % nt))],
        out_shape=[jax.ShapeDtypeStruct((M, n_k), BF16), jax.ShapeDtypeStruct((M, n_v), BF16),
                   jax.ShapeDtypeStruct((batch, n_q, L), BF16)],
        compiler_params=_cparams(("parallel",)))(
            x2d, g.reshape(1, D), w_k, k_gain.reshape(1, n_k), _group_gmat(), w_v, w_q)


def _inproj_even_kernel(x_ref, g_ref, wk_ref, gain_ref, gmat_ref, wcm_ref, k_ref, qt_ref, vt_ref, ut_ref):
    hn = _rms_rows(x_ref[...], g_ref[...]).astype(BF16)
    yk = _dot(hn, wk_ref[...])
    for c, yc in enumerate(_group_rms(yk, gmat_ref[...], gain_ref[...])):
        k_ref[:, c * LANES:(c + 1) * LANES] = yc.astype(k_ref.dtype)
    yt = _dot_nt(wcm_ref[...], hn)
    tm = yt.shape[1]
    q = yt[:DA_QK].reshape(DA_QK // HEAD_DIM, HEAD_DIM, tm)
    ms = jnp.mean(q * q, axis=1, keepdims=True)
    qt_ref[0, 0] = (q * lax.rsqrt(ms + EPS)).reshape(DA_QK, tm).astype(qt_ref.dtype)
    vt_ref[0, 0] = yt[DA_QK:DA_QK + DA_V].astype(vt_ref.dtype)
    ut_ref[0] = yt[DA_QK + DA_V:]


def _inproj_even(x2d, g, w_k, k_gain, w_cm, batch, tm):
    M, D = x2d.shape
    L = M // batch
    nt = L // tm
    n_u = w_cm.shape[0] - DA_QK - DA_V
    return pl.pallas_call(
        _inproj_even_kernel, name="inproj_even", grid=(M // tm,),
        in_specs=[pl.BlockSpec((tm, D), lambda i: (i, 0)), _full((1, D)), _full((D, DA_QK)), _full((1, DA_QK)),
                  _full((LANES, LANES)), _full(w_cm.shape)],
        out_specs=[pl.BlockSpec((tm, DA_QK), lambda i: (i, 0)),
                   pl.BlockSpec((1, 1, DA_QK, tm), lambda i: (i // nt, i % nt, 0, 0)),
                   pl.BlockSpec((1, 1, DA_V, tm), lambda i: (i // nt, i % nt, 0, 0)),
                   pl.BlockSpec((1, n_u, tm), lambda i: (i // nt, 0, i % nt))],
        out_shape=[jax.ShapeDtypeStruct((M, DA_QK), BF16),
                   jax.ShapeDtypeStruct((batch, nt, DA_QK, tm), BF16),
                   jax.ShapeDtypeStruct((batch, nt, DA_V, tm), BF16),
                   jax.ShapeDtypeStruct((batch, n_u, L), F32)],
        compiler_params=_cparams(("parallel",)))(
            x2d, g.reshape(1, D), w_k, k_gain.reshape(1, DA_QK), _group_gmat(), w_cm)


def _outproj_kernel(x_ref, a_ref, b_ref, wa_ref, wb_ref, o_ref, *, b_cm):
    b = b_ref[0].T.astype(BF16) if b_cm else b_ref[...]
    o_ref[...] = x_ref[...] + _dot(a_ref[...], wa_ref[...]) + _dot(b, wb_ref[...])


def _outproj(x2d, a, b, wa, wb, *, b_cm, batch=None, tm=512):
    M, D = x2d.shape
    if b_cm:
        L = M // batch
        nt = L // tm
        b_spec = pl.BlockSpec((1, b.shape[1], tm), lambda i: (i // nt, 0, i % nt))
    else:
        b_spec = pl.BlockSpec((tm, b.shape[1]), lambda i: (i, 0))
    return pl.pallas_call(
        functools.partial(_outproj_kernel, b_cm=b_cm), name="outproj",
        grid=(M // tm,),
        in_specs=[pl.BlockSpec((tm, D), lambda i: (i, 0)), pl.BlockSpec((tm, a.shape[1]), lambda i: (i, 0)),
                  b_spec, _full(wa.shape), _full(wb.shape)],
        out_specs=pl.BlockSpec((tm, D), lambda i: (i, 0)),
        out_shape=jax.ShapeDtypeStruct((M, D), F32),
        compiler_params=_cparams(("parallel",)))(x2d, a, b, wa, wb)


def _mlp_kernel(x_ref, g_ref, w1_ref, w2_ref, o_ref, hn_ref):
    j = pl.program_id(1)

    @pl.when(j == 0)
    def _():
        x = x_ref[...]
        hn_ref[...] = _rms_rows(x, g_ref[...]).astype(BF16)
        o_ref[...] = x

    h = _dot(hn_ref[...], w1_ref[...])
    h = jnp.square(jnp.maximum(h, 0.0))
    o_ref[...] += _dot(h.astype(BF16), w2_ref[...])


def _mlp(x2d, g, w1, w2, tm=1024, tf=1024):
    M, D = x2d.shape
    F = w1.shape[1]
    tm = min(tm, M)
    return pl.pallas_call(
        _mlp_kernel, name="mlp",
        grid=(M // tm, F // tf),
        in_specs=[pl.BlockSpec((tm, D), lambda i, j: (i, 0)), _full((1, D)),
                  pl.BlockSpec((D, tf), lambda i, j: (0, j)), pl.BlockSpec((tf, D), lambda i, j: (j, 0))],
        out_specs=pl.BlockSpec((tm, D), lambda i, j: (i, 0)),
        out_shape=jax.ShapeDtypeStruct((M, D), F32),
        scratch_shapes=[pltpu.VMEM((tm, D), BF16)],
        compiler_params=_cparams(("parallel", "arbitrary")))(x2d, g.reshape(1, D), w1, w2)


def _memkv_kernel(m_ref, g_ref, wk_ref, gain_ref, gmat_ref, wvt_ref, k_ref, vt_ref):
    mn = _rms_rows(m_ref[0], g_ref[...]).astype(BF16)
    kk = _dot(mn, wk_ref[...])
    for c, kc in enumerate(_group_rms(kk, gmat_ref[...], gain_ref[...])):
        k_ref[0, :, c * LANES:(c + 1) * LANES] = kc.astype(BF16)
    vt_ref[0] = _dot_nt(wvt_ref[...], mn).astype(BF16)


def _memkv(mem, g, wk, kgain, wvt):
    B, M, D = mem.shape
    return pl.pallas_call(
        _memkv_kernel, name="memkv", grid=(B,),
        in_specs=[pl.BlockSpec((1, M, D), lambda b: (b, 0, 0)), _full((1, D)), _full(wk.shape),
                  _full((1, MEM_W)), _full((LANES, LANES)), _full(wvt.shape)],
        out_specs=[pl.BlockSpec((1, M, MEM_W), lambda b: (b, 0, 0)), pl.BlockSpec((1, MEM_W, M), lambda b: (b, 0, 0))],
        out_shape=[jax.ShapeDtypeStruct((B, M, MEM_W), BF16), jax.ShapeDtypeStruct((B, MEM_W, M), BF16)],
        compiler_params=_cparams(("parallel",)))(
            mem, g.reshape(1, D), wk, kgain.reshape(1, MEM_W), _group_gmat(), wvt)


def _memattn_kernel(x_ref, g_ref, wqt_ref, k_ref, vt_ref, wo_ref, o_ref, ot_ref):
    x = x_ref[0]
    tm = x.shape[0]
    hn = _rms_rows(x, g_ref[...]).astype(BF16)
    q = _dot_nt(wqt_ref[...], hn).reshape(MEM_HEADS, HEAD_DIM, tm)
    ms = jnp.mean(q * q, axis=1, keepdims=True)
    qn = (q * lax.rsqrt(ms + EPS)).astype(BF16)
    k = k_ref[0]
    vt = vt_ref[0]
    for hp in range(MEM_HEADS // 2):
        ha, hb = 2 * hp, 2 * hp + 1
        s = _dot(k[:, hp * LANES:(hp + 1) * LANES], _blockdiag_q(qn[ha], qn[hb]))
        e = jnp.exp2(s - jnp.max(s, axis=0, keepdims=True))
        p = (e * (1.0 / jnp.sum(e, axis=0, keepdims=True))).astype(BF16)
        ot_ref[ha * HEAD_DIM:(ha + 1) * HEAD_DIM, :] = _dot(vt[ha * HEAD_DIM:(ha + 1) * HEAD_DIM], p[:, :tm])
        ot_ref[hb * HEAD_DIM:(hb + 1) * HEAD_DIM, :] = _dot(vt[hb * HEAD_DIM:(hb + 1) * HEAD_DIM], p[:, tm:])
    o_ref[0] = x + _dot_tn(ot_ref[...].astype(BF16), wo_ref[...])


def _memattn(x, kn, vt, g, wqt, wo, tm=512):
    B, L, D = x.shape
    M = kn.shape[1]
    return pl.pallas_call(
        _memattn_kernel, name="memattn", grid=(B, L // tm),
        in_specs=[pl.BlockSpec((1, tm, D), lambda b, i: (b, i, 0)), _full((1, D)), _full(wqt.shape),
                  pl.BlockSpec((1, M, MEM_W), lambda b, i: (b, 0, 0)),
                  pl.BlockSpec((1, MEM_W, M), lambda b, i: (b, 0, 0)), _full(wo.shape)],
        out_specs=pl.BlockSpec((1, tm, D), lambda b, i: (b, i, 0)),
        out_shape=jax.ShapeDtypeStruct((B, L, D), F32),
        scratch_shapes=[pltpu.VMEM((MEM_W, tm), F32)],
        compiler_params=_cparams(("parallel", "parallel")))(x, g.reshape(1, D), wqt, kn, vt, wo)


def _t5_bucket_np(rel):
    half = T5_BUCKETS // 2
    exact = half // 2
    n = np.abs(rel)
    nf = np.maximum(n, 1).astype(np.float64)
    large = exact + (np.log(nf / exact) / math.log(T5_MAX_DIST / exact) * (half - exact)).astype(np.int32)
    large = np.minimum(large, half - 1)
    return (np.where(rel > 0, half, 0) + np.where(n < exact, n, large)).astype(np.int32)


def _t5bias_kernel(table_ref, bucket_ref, mask_ref, o_ref, *, scale):
    h = pl.program_id(0)
    bucket = bucket_ref[...]
    acc = jnp.zeros(bucket.shape, F32)
    for b in range(T5_BUCKETS):
        acc = jnp.where(bucket == b, table_ref[b, h], acc)
    o_ref[0] = acc * scale + mask_ref[...]


def _t5bias(table, rel, mask=None, scale=1.0):
    bucket = jnp.asarray(_t5_bucket_np(rel))
    mask = jnp.zeros(rel.shape, F32) if mask is None else jnp.asarray(mask, F32)
    nd = rel.ndim
    return pl.pallas_call(
        functools.partial(_t5bias_kernel, scale=scale), name="t5bias", grid=(T5_HEADS,),
        in_specs=[pl.BlockSpec(memory_space=pltpu.SMEM), _full(rel.shape), _full(rel.shape)],
        out_specs=pl.BlockSpec((1,) + rel.shape, lambda h: (h,) + (0,) * nd),
        out_shape=jax.ShapeDtypeStruct((T5_HEADS,) + rel.shape, F32),
        compiler_params=_cparams(("arbitrary",)))(table, bucket, mask)


def _diffattn_kernel(lam_ref, cfar_ref, q_ref, k_ref, v_ref, bias_ref, sg_ref, o_ref, qp_ref, sa_ref, sb_ref,
                     xa_ref, xb_ref, m_ref, acc_ref, *, lam_init, T, nkv):
    h = pl.program_id(1)
    i = pl.program_id(2)
    q = q_ref[0, 0]
    row = lax.broadcasted_iota(jnp.int32, q.shape, 0)
    zero = jnp.zeros(q.shape, q.dtype)
    qp_ref[0] = jnp.where(row < HEAD_DIM, q, zero)
    qp_ref[1] = jnp.where(row >= HEAD_DIM, q, zero)
    m_ref[...] = jnp.full(m_ref.shape, NEG_INF, F32)
    acc_ref[...] = jnp.zeros(acc_ref.shape, F32)

    def scores(jj, s_ref, x_ref):
        k = k_ref[0, pl.ds(pl.multiple_of(jj * T, T), T), :]
        for m in range(2):
            s = _dot(k, qp_ref[m])
            s_ref[m] = s
            x_ref[m] = jnp.max(s, axis=0, keepdims=True)

    def step(jj, s_cur, x_cur, s_nxt, x_nxt):
        d = jj - i
        near = jnp.abs(d) <= 1

        @pl.when(near)
        def _():
            for m in range(2):
                s = s_cur[m] + bias_ref[m, d + 1]
                s_cur[m] = s
                x_cur[m] = jnp.max(s, axis=0, keepdims=True)

        scores(jnp.minimum(jj + 1, nkv - 1), s_nxt, x_nxt)
        v = jnp.concatenate([v_ref[0, jj], jnp.ones((DA_SUM_ROWS, T), BF16)], axis=0)
        for m in range(2):
            c = jnp.where(near, 0.0, jnp.where(d < 0, cfar_ref[0, 2 * h + m], cfar_ref[1, 2 * h + m]))
            m_prev = m_ref[m]
            m_new = jnp.maximum(m_prev, x_cur[m] + c)
            alpha = jnp.exp2(m_prev - m_new)
            p = jnp.exp2((s_cur[m] - (m_new - c)).astype(BF16))
            acc_ref[m] = alpha * acc_ref[m] + _dot(v, p)
            m_ref[m] = m_new

    scores(0, sa_ref, xa_ref)

    def body(t, carry):
        step(2 * t, sa_ref, xa_ref, sb_ref, xb_ref)
        step(2 * t + 1, sb_ref, xb_ref, sa_ref, xa_ref)
        return carry

    lax.fori_loop(0, nkv // 2, body, 0)

    lf = lam_ref[...]
    lam = (jnp.exp(jnp.sum(lf[0:1] * lf[1:2], axis=-1, keepdims=True))
           - jnp.exp(jnp.sum(lf[2:3] * lf[3:4], axis=-1, keepdims=True)) + lam_init)
    num = [acc_ref[m, 0:DA_VDIM, :] for m in range(2)]
    den = [acc_ref[m, DA_VDIM:DA_VDIM + 1, :] for m in range(2)]
    o = num[0] / den[0] - lam * (num[1] / den[1])
    ms = jnp.mean(o * o, axis=0, keepdims=True)
    o = o * lax.rsqrt(ms + EPS) * sg_ref[...] * (1.0 - lam_init)
    o_ref[0] = o.T.astype(o_ref.dtype)


def _diffattn(qt, k, vt, bias, cfar, lam, sub_gain, lam_init):
    B, nb, _, T = qt.shape
    L = nb * T
    assert nb % 2 == 0
    return pl.pallas_call(
        functools.partial(_diffattn_kernel, lam_init=lam_init, T=T, nkv=nb), name="diffattn",
        grid=(B, DA_HEADS, nb),
        in_specs=[_full((4, HEAD_DIM)), pl.BlockSpec(memory_space=pltpu.SMEM),
                  pl.BlockSpec((1, 1, LANES, T), lambda b, h, i: (b, i, h, 0)),
                  pl.BlockSpec((1, L, LANES), lambda b, h, i: (b, 0, h)),
                  pl.BlockSpec((1, nb, LANES, T), lambda b, h, i: (b, 0, h, 0)),
                  pl.BlockSpec((2, 3, T, T), lambda b, h, i: (h, 0, 0, 0)),
                  _full((DA_VDIM, 1))],
        out_specs=pl.BlockSpec((1, T, LANES), lambda b, h, i: (b, i, h)),
        out_shape=jax.ShapeDtypeStruct((B, L, DA_V), BF16),
        scratch_shapes=[pltpu.VMEM((2, LANES, T), BF16), pltpu.VMEM((2, T, T), F32), pltpu.VMEM((2, T, T), F32),
                        pltpu.VMEM((2, 1, T), F32), pltpu.VMEM((2, 1, T), F32),
                        pltpu.VMEM((2, 1, T), F32), pltpu.VMEM((2, DA_VDIM + DA_SUM_ROWS, T), F32)],
        compiler_params=_cparams(("parallel", "parallel", "arbitrary")))(
            lam, cfar, qt, k, vt, bias, sub_gain.reshape(DA_VDIM, 1))


def _dot_tn(a, b):
    return lax.dot_general(a, b, (((0,), (0,)), ((), ())), preferred_element_type=F32)


def _blockdiag_q(qa, qb):
    z = jnp.zeros(qa.shape, qa.dtype)
    return jnp.concatenate([jnp.concatenate([qa, z], axis=0), jnp.concatenate([z, qb], axis=0)], axis=1)


def _wgqa_kernel(sink_ref, q_ref, kp_ref, kc_ref, kn_ref, vp_ref, vc_ref, vn_ref, bias_ref, o_ref, kw_ref, vw_ref):
    n = pl.program_id(1)
    nb = pl.num_programs(1)
    W = WG_BLOCK
    kw_ref[0:W] = kp_ref[0]
    kw_ref[W:2 * W] = kc_ref[0]
    kw_ref[2 * W:3 * W] = kn_ref[0]
    vw_ref[0:W] = vp_ref[0]
    vw_ref[W:2 * W] = vc_ref[0]
    vw_ref[2 * W:3 * W] = vn_ref[0]
    key = lax.broadcasted_iota(jnp.int32, (3 * W, 2 * W), 0)
    valid = jnp.logical_and(jnp.logical_or(n > 0, key >= W), jnp.logical_or(n < nb - 1, key < 2 * W))
    lane2 = lax.broadcasted_iota(jnp.int32, (1, 2 * W), 1)
    lane1 = lax.broadcasted_iota(jnp.int32, (W, LANES), 1)
    grp = WG_HEADS // WG_KV_HEADS
    kw = kw_ref[...]
    vw = vw_ref[...]
    for h in range(grp):
        qa = q_ref[0, h * HEAD_DIM:(h + 1) * HEAD_DIM, :]
        qb = q_ref[0, (h + grp) * HEAD_DIM:(h + grp + 1) * HEAD_DIM, :]
        s = _dot(kw, _blockdiag_q(qa, qb))
        s = s + jnp.concatenate([bias_ref[h], bias_ref[h + grp]], axis=1)
        s = jnp.where(valid, s, NEG_INF)
        sk = jnp.where(lane2 < W, sink_ref[h], sink_ref[h + grp]) * LOG2E
        mx = jnp.maximum(jnp.max(s, axis=0, keepdims=True), sk)
        e = jnp.exp2(s - mx)
        den = jnp.sum(e, axis=0, keepdims=True) + jnp.exp2(sk - mx)
        p = (e * (1.0 / den)).astype(BF16)
        o2 = _dot_tn(p, vw)
        o_ref[0, :, h * LANES:(h + 1) * LANES] = jnp.where(lane1 < HEAD_DIM, o2[:W], o2[W:]).astype(o_ref.dtype)


def _wgqa(qt, kk, vv, bias, sink, q_row, k_col, v_col):
    B, L, _ = kk.shape
    W = WG_BLOCK
    nb = L // W
    prev = lambda n: jnp.maximum(n - 1, 0)
    nxt = lambda n: jnp.minimum(n + 1, nb - 1)
    return pl.pallas_call(
        _wgqa_kernel, name="wgqa", grid=(B, nb),
        in_specs=[pl.BlockSpec(memory_space=pltpu.SMEM),
                  pl.BlockSpec((1, WG_Q, W), lambda b, n: (b, q_row, n)),
                  pl.BlockSpec((1, W, LANES), lambda b, n: (b, prev(n), k_col)),
                  pl.BlockSpec((1, W, LANES), lambda b, n: (b, n, k_col)),
                  pl.BlockSpec((1, W, LANES), lambda b, n: (b, nxt(n), k_col)),
                  pl.BlockSpec((1, W, LANES), lambda b, n: (b, prev(n), v_col)),
                  pl.BlockSpec((1, W, LANES), lambda b, n: (b, n, v_col)),
                  pl.BlockSpec((1, W, LANES), lambda b, n: (b, nxt(n), v_col)),
                  _full(bias.shape)],
        out_specs=pl.BlockSpec((1, W, WG_Q), lambda b, n: (b, n, 0)),
        out_shape=jax.ShapeDtypeStruct((B, L, WG_Q), BF16),
        scratch_shapes=[pltpu.VMEM((3 * W, LANES), BF16), pltpu.VMEM((3 * W, LANES), BF16)],
        compiler_params=_cparams(("parallel", "arbitrary")))(sink, qt, kk, kk, kk, vv, vv, vv, bias)


NA_KROWS = NA_ROWS + 1
NA_CASES = 5
_NA_CASE_GEOM = (((0, 0), 7), ((0, 0), 5), ((0, 1), 3), ((1, 1), 2), ((1, 1), 0))


def _nabias_kernel(rpb_ref, o_ref):
    h = pl.program_id(0)
    cc = lax.broadcasted_iota(jnp.int32, (GRID_W, LANES), 0)
    lane = lax.broadcasted_iota(jnp.int32, (GRID_W, LANES), 1)
    c = lane % GRID_W
    second = lane >= GRID_W
    c_start = jnp.clip(c - NA_COLS // 2, 0, GRID_W - NA_COLS)
    valid = jnp.logical_and(cc >= c_start, cc < c_start + NA_COLS)
    dc = cc - c + (NA_COLS - 1)
    neg = jnp.full((GRID_W, LANES), NEG_INF, F32)
    tiles = {}
    for dr in range(1, 2 * NA_ROWS - 1):
        acc = neg
        for d in range(2 * NA_COLS - 1):
            val = jnp.where(second, rpb_ref[h, dr - 1, d], rpb_ref[h, dr, d]) * LOG2E
            acc = jnp.where(jnp.logical_and(valid, dc == d), val, acc)
        tiles[dr] = acc
    for case, (wstart, a) in enumerate(_NA_CASE_GEOM):
        for i in range(NA_KROWS):
            in0 = wstart[0] <= i < wstart[0] + NA_ROWS
            in1 = wstart[1] <= i < wstart[1] + NA_ROWS
            t = tiles[i + a] if (in0 or in1) else neg
            if in1 and not in0:
                t = jnp.where(second, t, neg)
            if in0 and not in1:
                t = jnp.where(second, neg, t)
            o_ref[case, 0, i * GRID_W:(i + 1) * GRID_W, :] = t


def _nabias(rpb):
    return pl.pallas_call(
        _nabias_kernel, name="nabias", grid=(NA_HEADS,),
        in_specs=[pl.BlockSpec(memory_space=pltpu.SMEM)],
        out_specs=pl.BlockSpec((NA_CASES, 1, NA_KROWS * GRID_W, LANES), lambda h: (0, h, 0, 0)),
        out_shape=jax.ShapeDtypeStruct((NA_CASES, NA_HEADS, NA_KROWS * GRID_W, LANES), F32),
        compiler_params=_cparams(("arbitrary",)))(rpb)


def _natten_kernel(q_ref, kp_ref, kc_ref, kn_ref, vp_ref, vc_ref, vn_ref, bias_ref, o_ref, kw_ref, vw_ref, *, rows):
    i = pl.program_id(1)
    RB = NA_ROWS
    T = RB * GRID_W
    KW = NA_KROWS * GRID_W
    kw_ref[0:T] = kp_ref[0]
    kw_ref[T:2 * T] = kc_ref[0]
    kw_ref[2 * T:3 * T] = kn_ref[0]
    vw_ref[0:T] = vp_ref[0]
    vw_ref[T:2 * T] = vc_ref[0]
    vw_ref[2 * T:3 * T] = vn_ref[0]
    lane1 = lax.broadcasted_iota(jnp.int32, (LANES, LANES), 1)
    for pi in range(RB // 2):
        r = i * RB + 2 * pi
        r_lo = jnp.clip(r - NA_ROWS // 2, 0, rows - NA_KROWS)
        off = pl.multiple_of((r_lo - i * RB + RB) * GRID_W, GRID_W)
        case = jnp.where(r == 0, 0, jnp.where(r == 2, 1, jnp.where(r == rows - 4, 3, jnp.where(r == rows - 2, 4, 2))))
        qs = slice(pi * LANES, (pi + 1) * LANES)
        for hp in range(NA_HEADS // 2):
            ha, hb = 2 * hp, 2 * hp + 1
            ls = slice(hp * LANES, (hp + 1) * LANES)
            kw = kw_ref[pl.ds(off, KW), ls]
            vw = vw_ref[pl.ds(off, KW), ls]
            qa = q_ref[0, ha * HEAD_DIM:(ha + 1) * HEAD_DIM, qs]
            qb = q_ref[0, hb * HEAD_DIM:(hb + 1) * HEAD_DIM, qs]
            s = _dot(kw, _blockdiag_q(qa, qb))
            s = s + jnp.concatenate([bias_ref[case, ha], bias_ref[case, hb]], axis=1)
            e = jnp.exp2(s - jnp.max(s, axis=0, keepdims=True))
            p = (e * (1.0 / jnp.sum(e, axis=0, keepdims=True))).astype(BF16)
            o2 = _dot_tn(p, vw)
            o_ref[0, qs, ls] = jnp.where(lane1 < HEAD_DIM, o2[:LANES], o2[LANES:]).astype(o_ref.dtype)


def _natten(qt, kk, vv, bias):
    B, L, _ = kk.shape
    rows = L // GRID_W
    assert rows >= NA_KROWS + 1 and rows % NA_ROWS == 0
    T = NA_ROWS * GRID_W
    nb = L // T
    prev = lambda n: jnp.maximum(n - 1, 0)
    nxt = lambda n: jnp.minimum(n + 1, nb - 1)
    return pl.pallas_call(
        functools.partial(_natten_kernel, rows=rows), name="natten", grid=(B, nb),
        in_specs=[pl.BlockSpec((1, NA_W, T), lambda b, n: (b, 0, n)),
                  pl.BlockSpec((1, T, NA_W), lambda b, n: (b, prev(n), 0)),
                  pl.BlockSpec((1, T, NA_W), lambda b, n: (b, n, 0)),
                  pl.BlockSpec((1, T, NA_W), lambda b, n: (b, nxt(n), 0)),
                  pl.BlockSpec((1, T, NA_W), lambda b, n: (b, prev(n), 0)),
                  pl.BlockSpec((1, T, NA_W), lambda b, n: (b, n, 0)),
                  pl.BlockSpec((1, T, NA_W), lambda b, n: (b, nxt(n), 0)),
                  _full(bias.shape)],
        out_specs=pl.BlockSpec((1, T, NA_W), lambda b, n: (b, n, 0)),
        out_shape=jax.ShapeDtypeStruct((B, L, NA_W), BF16),
        scratch_shapes=[pltpu.VMEM((3 * T, NA_W), BF16), pltpu.VMEM((3 * T, NA_W), BF16)],
        compiler_params=_cparams(("parallel", "arbitrary")))(qt, kk, kk, kk, vv, vv, vv, bias)


def _split(a):
    hi = a.astype(BF16)
    return hi, (a - hi.astype(F32)).astype(BF16)


def _np_split(a):
    a = np.asarray(a, np.float32)
    hi = a.astype(BF16)
    lo = (a - hi.astype(np.float32)).astype(BF16)
    return hi, lo


def _dot3_cl(ch, cl, x):
    xh, xl = _split(x)
    return _dot(ch, xh) + _dot(cl, xh) + _dot(ch, xl)


def _dot3_cr(x, ch, cl):
    xh, xl = _split(x)
    return _dot(xh, ch) + _dot(xl, ch) + _dot(xh, cl)


def _fft_consts(L):
    N2 = FFT_LANES
    N = 2 * L
    N1 = N // N2
    N1h = N1 // 2
    G = max(1, LANES // N1h)
    Gf = max(1, LANES // N1)
    k = np.arange(N1)
    ang1 = 2.0 * np.pi * np.outer(k, k) / N1
    C1, S1 = np.cos(ang1), np.sin(ang1)
    eye = np.eye
    KC, KS = np.kron(eye(G), C1[:, :N1h]), np.kron(eye(G), S1[:, :N1h])
    fa = np.block([[KC, KS], [-KS, KC]])
    KCh, KSh = np.kron(eye(G), C1[:N1h, :]), np.kron(eye(G), S1[:N1h, :])
    fg = np.block([[KCh, -KSh], [KSh, KCh]]) / N
    faf = np.concatenate([np.kron(eye(Gf), C1), -np.kron(eye(Gf), S1)], axis=0)
    n2 = np.arange(N2)
    angt = 2.0 * np.pi * np.outer(k, n2) / N
    twr, twi = np.cos(angt), -np.sin(angt)
    ang2 = 2.0 * np.pi * np.outer(n2, n2) / N2
    C2, S2 = np.cos(ang2), np.sin(ang2)
    m2f = np.block([[C2, -S2], [S2, C2]])
    m2i = np.block([[C2, S2], [-S2, C2]])
    return dict(N=N, N1=N1, N1h=N1h, N2=N2, G=G, Gf=Gf, fa=_np_split(fa), fg=_np_split(fg), faf=_np_split(faf),
                m2f=_np_split(m2f), m2i=_np_split(m2i),
                twr=np.asarray(twr, np.float32), twi=np.asarray(twi, np.float32))


def _hyfilt_kernel(w1t_ref, b1_ref, fr_ref, w2t_ref, b2_ref, w3t_ref, delta_ref, band_ref, o_ref, sum_ref, *, L, tn):
    s = pl.program_id(0)
    n = s * tn + lax.broadcasted_iota(jnp.int32, (1, tn), 1)
    t = jnp.where(n < L, n, 2 * L - n).astype(F32)
    t01 = t / float(max(L - 1, 1))
    w = (2.0 * math.pi) * t / float(L)
    ang = band_ref[...] * w
    cs, sn = jnp.cos(ang), -jnp.sin(ang)
    w1t = w1t_ref[...]
    h = w1t[:, 0:1] * t01
    for b in range(HY_BANDS):
        h = h + w1t[:, 1 + b:2 + b] * cs[b:b + 1] + w1t[:, 1 + HY_BANDS + b:2 + HY_BANDS + b] * sn[b:b + 1]
    fr = fr_ref[...]
    h = jnp.sin(fr[:, 0:1] * (h + b1_ref[...]))
    h = jnp.sin(fr[:, 1:2] * (jnp.dot(w2t_ref[...], h, preferred_element_type=F32,
                                      precision=lax.Precision.HIGHEST) + b2_ref[...]))
    y = jnp.dot(w3t_ref[0], h, preferred_element_type=F32, precision=lax.Precision.HIGHEST)
    y = y * jnp.exp(-delta_ref[...] * t01)
    y = jnp.where(n == L, 0.0, y)
    o_ref[...] = y
    a = jnp.abs(y)
    part = a[:, 0:LANES]
    for c in range(1, tn // LANES):
        part = part + a[:, c * LANES:(c + 1) * LANES]

    @pl.when(s == 0)
    def _():
        sum_ref[...] = jnp.zeros(sum_ref.shape, F32)

    sum_ref[...] += part


def _hyena_filters_td(L, w1, b1, freq, w2, b2, w3, tn=1024):
    OC = HY_ORDER * HY_CH
    w3t = jnp.transpose(w3.reshape(HY_HIDDEN, HY_ORDER, 2, HY_CH), (2, 1, 3, 0)).reshape(2, OC, HY_HIDDEN)
    min_decay = math.log(HY_TARGET) / HY_SLOW_DECAY
    max_decay = math.log(HY_TARGET) / HY_FAST_DECAY
    deltas = np.abs(np.linspace(min_decay, max_decay, HY_CH, dtype=np.float32))
    delta = jnp.asarray(np.tile(deltas, HY_ORDER).reshape(OC, 1))
    bands = jnp.asarray(np.linspace(1e-4, HY_BANDS - 1, HY_BANDS, dtype=np.float32).reshape(HY_BANDS, 1))
    N = 2 * L
    nh = L // tn
    return pl.pallas_call(
        functools.partial(_hyfilt_kernel, L=L, tn=tn), name="hyfilt", grid=(N // tn,),
        in_specs=[_full((HY_HIDDEN, 1 + 2 * HY_BANDS)), _full((HY_HIDDEN, 1)), _full((HY_HIDDEN, 2)),
                  _full((HY_HIDDEN, HY_HIDDEN)), _full((HY_HIDDEN, 1)),
                  pl.BlockSpec((1, OC, HY_HIDDEN), lambda s: (s // nh, 0, 0)),
                  _full((OC, 1)), _full((HY_BANDS, 1))],
        out_specs=[pl.BlockSpec((OC, tn), lambda s: (0, s)), _full((OC, LANES))],
        out_shape=[jax.ShapeDtypeStruct((OC, N), F32), jax.ShapeDtypeStruct((OC, LANES), F32)],
        compiler_params=_cparams(("arbitrary",)))(
            w1.T, b1.reshape(HY_HIDDEN, 1), freq.T, w2.T, b2.reshape(HY_HIDDEN, 1), w3t, delta, bands)


def _hyspec_kernel(k_ref, sum_ref, fah_ref, fal_ref, twr_ref, twi_ref, m2h_ref, m2l_ref, kr_ref, ki_ref, p_ref,
                   *, R, N1, N2, Gf):
    tot = jnp.sum(sum_ref[...], axis=-1, keepdims=True)
    mi = Gf * N1
    for g in range(R // Gf):
        xs = []
        for r in range(Gf):
            row = g * Gf + r
            xs.append(k_ref[row] / tot[row:row + 1, :])
        x = xs[0] if Gf == 1 else jnp.concatenate(xs, axis=0)
        a = _dot3_cl(fah_ref[...], fal_ref[...], x)
        ar, ai = a[:mi], a[mi:]
        twr, twi = twr_ref[...], twi_ref[...]
        p_ref[g * mi:(g + 1) * mi, 0:N2] = ar * twr - ai * twi
        p_ref[g * mi:(g + 1) * mi, N2:2 * N2] = ar * twi + ai * twr
    b = _dot3_cr(p_ref[...], m2h_ref[...], m2l_ref[...])
    kr_ref[...] = b[:, :N2].reshape(R, N1, N2)
    ki_ref[...] = b[:, N2:].reshape(R, N1, N2)


def _hyena_spectra(ktd, ksum, fc):
    OC = ktd.shape[0]
    N1, N2, Gf = fc["N1"], fc["N2"], fc["Gf"]
    R = 1024 // N1
    twr = jnp.asarray(np.tile(fc["twr"], (Gf, 1)))
    twi = jnp.asarray(np.tile(fc["twi"], (Gf, 1)))
    fah, fal = fc["faf"]
    m2h, m2l = fc["m2f"]
    return pl.pallas_call(
        functools.partial(_hyspec_kernel, R=R, N1=N1, N2=N2, Gf=Gf), name="hyspec", grid=(OC // R,),
        in_specs=[pl.BlockSpec((R, N1, N2), lambda i: (i, 0, 0)), pl.BlockSpec((R, LANES), lambda i: (i, 0)),
                  _full(fah.shape), _full(fal.shape), _full(twr.shape), _full(twi.shape),
                  _full(m2h.shape), _full(m2l.shape)],
        out_specs=[pl.BlockSpec((R, N1, N2), lambda i: (i, 0, 0))] * 2,
        out_shape=[jax.ShapeDtypeStruct((OC, N1, N2), F32)] * 2,
        scratch_shapes=[pltpu.VMEM((R * N1, 2 * N2), F32)],
        compiler_params=_cparams(("parallel",)))(ktd.reshape(OC, N1, N2), ksum, fah, fal, twr, twi, m2h, m2l)


def _hyconv_kernel(cw_ref, cb_ref, sk_ref, u_ref, kr_ref, ki_ref, fa_ref, fg_ref, twr_ref, twi_ref, mf_ref, mi_ref,
                   o_ref, z_ref, x1_ref, x2_ref, c_ref, p_ref, wcol_ref, *, R, N1, N2, G):
    N1h = N1 // 2
    cb = pl.program_id(0)
    ng = R // G
    mi = G * N1h
    mo = G * N1
    n_parts = 3
    i_bias = 3 * n_parts
    i_skip = i_bias + n_parts

    @pl.when(pl.program_id(1) == 0)
    def _():
        def fill(r, carry):
            ch = cb * R + r
            rs = pl.ds(pl.multiple_of(r * N1h, 8), N1h)
            for part in range(n_parts):
                c = part * HY_CH + ch
                for tap in range(3):
                    wcol_ref[tap * n_parts + part, rs, :] = jnp.full((N1h, LANES), cw_ref[tap, c], F32)
                wcol_ref[i_bias + part, rs, :] = jnp.full((N1h, LANES), cb_ref[c], F32)
            for o in range(HY_ORDER):
                wcol_ref[i_skip + o, rs, :] = jnp.full((N1h, LANES), sk_ref[o, ch], F32)
            return carry
        lax.fori_loop(0, R, fill, 0)

    def wide(idx, g):
        w = wcol_ref[idx, g * mi:(g + 1) * mi, :]
        return jnp.concatenate([w] * (N2 // LANES), axis=1)

    lane = lax.broadcasted_iota(jnp.int32, (mi, N2), 1)
    sub = lax.broadcasted_iota(jnp.int32, (mi, N2), 0) % N1h
    first = jnp.logical_and(lane == 0, sub == 0)
    last = jnp.logical_and(lane == N2 - 1, sub == N1h - 1)

    def shortconv(u, part, g):
        rl = pltpu.roll(u, 1, 1)
        prev = jnp.where(lane == 0, pltpu.roll(rl, 1, 0), rl)
        prev = jnp.where(first, 0.0, prev)
        rr = pltpu.roll(u, N2 - 1, 1)
        nxt = jnp.where(lane == N2 - 1, pltpu.roll(rr, mi - 1, 0), rr)
        nxt = jnp.where(last, 0.0, nxt)
        return (prev * wide(part, g) + u * wide(n_parts + part, g) + nxt * wide(2 * n_parts + part, g)
                + wide(i_bias + part, g))

    for g in range(ng):
        for pbatch in range(2):
            rs = slice((g * 2 + pbatch) * mi, (g * 2 + pbatch + 1) * mi)
            for part, dst in enumerate((z_ref, x1_ref, x2_ref)):
                u = u_ref[0, pbatch, part, g * G:(g + 1) * G].reshape(mi, N2)
                dst[rs, :] = shortconv(u, part, g)

    def conv(order):
        twr, twi = twr_ref[...], twi_ref[...]
        for g in range(ng):
            a = _dot(fa_ref[...], z_ref[g * 2 * mi:(g + 1) * 2 * mi, :].astype(BF16))
            ar, ai = a[:mo], a[mo:]
            p_ref[g * mo:(g + 1) * mo, 0:N2] = (ar * twr - ai * twi).astype(BF16)
            p_ref[g * mo:(g + 1) * mo, N2:2 * N2] = (ar * twi + ai * twr).astype(BF16)
        b = _dot(p_ref[...], mf_ref[...])
        br, bi = b[:, :N2], b[:, N2:]
        kr = kr_ref[order].reshape(R * N1, N2)
        ki = ki_ref[order].reshape(R * N1, N2)
        p_ref[:, 0:N2] = (br * kr - bi * ki).astype(BF16)
        p_ref[:, N2:2 * N2] = (br * ki + bi * kr).astype(BF16)
        d = _dot(p_ref[...], mi_ref[...])
        dr, di = d[:, :N2], d[:, N2:]
        for g in range(ng):
            sl = slice(g * mo, (g + 1) * mo)
            x = jnp.concatenate([dr[sl] * twr + di[sl] * twi, di[sl] * twr - dr[sl] * twi], axis=0)
            c_ref[g * 2 * mi:(g + 1) * 2 * mi, :] = _dot(fg_ref[...], x.astype(BF16))

    def gate(order, x_ref, final):
        for g in range(ng):
            skv = wide(i_skip + order, g)
            for pbatch in range(2):
                rs = slice((g * 2 + pbatch) * mi, (g * 2 + pbatch + 1) * mi)
                zn = x_ref[rs, :] * (c_ref[rs, :] + skv * z_ref[rs, :])
                if final:
                    o_ref[0, pbatch, g * G:(g + 1) * G] = zn.reshape(G, N1h, N2)
                else:
                    z_ref[rs, :] = zn

    conv(0)
    gate(0, x1_ref, False)
    conv(1)
    gate(1, x2_ref, True)


def _hyena_conv(ucm, conv_w, conv_b, skip, kr, ki, fc):
    B = ucm.shape[0]
    C = HY_CH
    N1, N2, G = fc["N1"], fc["N2"], fc["G"]
    N1h = N1 // 2
    R = 1024 // N1
    u = ucm.reshape(B // 2, 2, 3, C, N1h, N2)
    twr = jnp.asarray(np.tile(fc["twr"], (G, 1)))
    twi = jnp.asarray(np.tile(fc["twi"], (G, 1)))
    consts = [fc["fa"][0], fc["fg"][0], twr, twi, fc["m2f"][0], fc["m2i"][0]]
    rows = R * N1h
    smem = pl.BlockSpec(memory_space=pltpu.SMEM)
    out = pl.pallas_call(
        functools.partial(_hyconv_kernel, R=R, N1=N1, N2=N2, G=G), name="hyconv",
        grid=(C // R, B // 2),
        in_specs=[smem, smem, smem,
                  pl.BlockSpec((1, 2, 3, R, N1h, N2), lambda c, p: (p, 0, 0, c, 0, 0)),
                  pl.BlockSpec((HY_ORDER, R, N1, N2), lambda c, p: (0, c, 0, 0)),
                  pl.BlockSpec((HY_ORDER, R, N1, N2), lambda c, p: (0, c, 0, 0))]
                 + [_full(a.shape) for a in consts],
        out_specs=pl.BlockSpec((1, 2, R, N1h, N2), lambda c, p: (p, 0, c, 0, 0)),
        out_shape=jax.ShapeDtypeStruct((B // 2, 2, C, N1h, N2), F32),
        scratch_shapes=[pltpu.VMEM((2 * rows, N2), F32)] * 4
                       + [pltpu.VMEM((R * N1, 2 * N2), BF16), pltpu.VMEM((4 * 3 + HY_ORDER, rows, LANES), F32)],
        compiler_params=_cparams(("parallel", "arbitrary")))(
            conv_w, conv_b, skip, u, kr.reshape(HY_ORDER, C, N1, N2), ki.reshape(HY_ORDER, C, N1, N2), *consts)
    return out.reshape(B, C, N1h * N2)


def _lambda_init(layer):
    return 0.8 - 0.6 * math.exp(-0.3 * layer)


def _tile_gain(g, reps, scale=1.0):
    return jnp.tile(g.astype(F32), reps) * scale


def _trunk(x, mem, p, shared):
    B, L, D = x.shape
    M = B * L
    scale = HEAD_DIM ** -0.5
    x2 = x.reshape(M, D)
    for layer in range(DEPTH):
        j = layer // 2
        if layer % 2 == 0:
            w_in = p["w_in_even"][j].astype(BF16)
            k_gain = _tile_gain(p["da_q_gain"][j] * p["da_k_gain"][j], DA_QK // HEAD_DIM, scale * LOG2E)
            w_cm = jnp.concatenate([w_in[:, :DA_QK], w_in[:, 2 * DA_QK:]], axis=1).T
            kk, qt, vt, ucm = _inproj_even(x2, p["norm_mix"][layer], w_in[:, DA_QK:2 * DA_QK], k_gain, w_cm,
                                           batch=B, tm=shared["da_T"])
            oa = _diffattn(qt, kk.reshape(B, L, DA_QK), vt, shared["da_bias"], shared["da_cfar"],
                           p["da_lambda"][j], p["da_sub_gain"][j], _lambda_init(layer))
            fc = _fft_consts(L)
            ktd, ksum = _hyena_filters_td(L, p["hy_w1"][j], p["hy_b1"][j], p["hy_freq"][j], p["hy_w2"][j],
                                          p["hy_b2"][j], p["hy_w3"][j])
            kr, ki = _hyena_spectra(ktd, ksum, fc)
            ob = _hyena_conv(ucm, p["hy_conv_w"][j], p["hy_conv_b"][j], p["hy_skip"][j], kr, ki, fc)
            w_out = p["w_out_even"][j].astype(BF16)
            x2 = _outproj(x2, oa.reshape(M, DA_V), ob, w_out[:DA_V], w_out[DA_V:], b_cm=True, batch=B)
        else:
            w_in = p["w_in_odd"][j].astype(BF16)
            c0 = 3 * NA_W
            c1 = c0 + WG_Q
            c2 = c1 + WG_KV
            w_k = jnp.concatenate([w_in[:, NA_W:2 * NA_W], w_in[:, c1:c2]], axis=1)
            w_v = jnp.concatenate([w_in[:, 2 * NA_W:c0], w_in[:, c2:]], axis=1)
            w_q = jnp.concatenate([w_in[:, :NA_W], w_in[:, c0:c1]], axis=1).T
            k_gain = jnp.concatenate([
                _tile_gain(p["na_q_gain"][j] * p["na_k_gain"][j], NA_HEADS, scale * LOG2E),
                _tile_gain(p["wg_q_gain"][j] * p["wg_k_gain"][j], WG_KV_HEADS, scale * LOG2E)])
            kk, vv, qt = _inproj_odd(x2, p["norm_mix"][layer], w_k, k_gain, w_v, w_q, batch=B)
            kk = kk.reshape(B, L, -1)
            vv = vv.reshape(B, L, -1)
            oc = _natten(qt, kk, vv, shared["na_bias"][j])
            od = _wgqa(qt, kk, vv, shared["wg_bias"], p["wg_sink"][j], q_row=NA_W // WG_Q,
                       k_col=NA_W // LANES, v_col=NA_W // LANES)
            w_out = p["w_out_odd"][j].astype(BF16)
            grp = WG_HEADS // WG_KV_HEADS
            head_order = np.stack([np.arange(grp), np.arange(grp) + grp], axis=1).reshape(-1)
            wd_rows = NA_W + (head_order[:, None] * HEAD_DIM + np.arange(HEAD_DIM)[None, :]).reshape(-1)
            x2 = _outproj(x2, oc.reshape(M, NA_W), od.reshape(M, WG_Q), w_out[:NA_W], w_out[wd_rows], b_cm=False)
        wkv = p["mem_wkv"][layer].astype(BF16)
        kn, vt = _memkv(mem, p["norm_memkv"][layer], wkv[:, :MEM_W],
                        _tile_gain(p["mem_q_gain"][layer] * p["mem_k_gain"][layer], MEM_HEADS, scale * LOG2E),
                        wkv[:, MEM_W:].T)
        x3 = _memattn(x2.reshape(B, L, D), kn, vt, p["norm_mem"][layer], p["mem_wq"][layer].astype(BF16).T,
                      p["mem_wo"][layer].astype(BF16))
        x2 = _mlp(x3.reshape(M, D), p["norm_mlp"][layer], p["mlp_w1"][layer].astype(BF16),
                  p["mlp_w2"][layer].astype(BF16))
    return x2.reshape(B, L, D)


def _shared_tables(p, da_T):
    idx = np.arange(da_T)
    far = np.arange(T5_MAX_DIST, 1 << 20)
    assert da_T >= T5_MAX_DIST and (_t5_bucket_np(far) == _t5_bucket_np(far[:1])).all()
    assert (_t5_bucket_np(-far) == _t5_bucket_np(-far[:1])).all()
    rel_da = np.stack([d * da_T + idx[:, None] - idx[None, :] for d in (-1, 0, 1)])
    qi = np.arange(WG_BLOCK)
    ki = np.arange(3 * WG_BLOCK)
    rel_wg = ki[:, None] - WG_BLOCK - qi[None, :]
    mask_wg = np.where(np.abs(rel_wg) <= WG_WINDOW, 0.0, NEG_INF).astype(np.float32)
    return dict(da_T=da_T,
                da_bias=_t5bias(p["t5_table"], rel_da, scale=LOG2E),
                da_cfar=p["t5_table"][_t5_bucket_np(np.array([-T5_MAX_DIST, T5_MAX_DIST]))] * LOG2E,
                wg_bias=_t5bias(p["t5_table"], rel_wg, mask_wg, scale=LOG2E),
                na_bias=[_nabias(p["na_rpb"][j]) for j in range(p["na_rpb"].shape[0])])


def kernel(x_prompt, x_sample, mem_prompt, mem_sample, t5_table, norm_mix, norm_mem, norm_memkv, norm_mlp, w_in_even, da_q_gain, da_k_gain, da_lambda, da_sub_gain, hy_conv_w, hy_conv_b, hy_w1, hy_b1, hy_freq, hy_w2, hy_b2, hy_w3, hy_skip, w_out_even, w_in_odd, na_q_gain, na_k_gain, na_rpb, wg_q_gain, wg_k_gain, wg_sink, w_out_odd, mem_wq, mem_wkv, mem_wo, mem_q_gain, mem_k_gain, mlp_w1, mlp_w2):
    p = dict(t5_table=t5_table, norm_mix=norm_mix, norm_mem=norm_mem, norm_memkv=norm_memkv, norm_mlp=norm_mlp,
             w_in_even=w_in_even, da_q_gain=da_q_gain, da_k_gain=da_k_gain, da_lambda=da_lambda,
             da_sub_gain=da_sub_gain, hy_conv_w=hy_conv_w, hy_conv_b=hy_conv_b, hy_w1=hy_w1, hy_b1=hy_b1,
             hy_freq=hy_freq, hy_w2=hy_w2, hy_b2=hy_b2, hy_w3=hy_w3, hy_skip=hy_skip, w_out_even=w_out_even,
             w_in_odd=w_in_odd, na_q_gain=na_q_gain, na_k_gain=na_k_gain, na_rpb=na_rpb, wg_q_gain=wg_q_gain,
             wg_k_gain=wg_k_gain, wg_sink=wg_sink, w_out_odd=w_out_odd, mem_wq=mem_wq, mem_wkv=mem_wkv,
             mem_wo=mem_wo, mem_q_gain=mem_q_gain, mem_k_gain=mem_k_gain, mlp_w1=mlp_w1, mlp_w2=mlp_w2)
    shared = _shared_tables(p, da_T=512)
    return (_trunk(x_prompt, mem_prompt, p, shared), _trunk(x_sample, mem_sample, p, shared))
```

```python
import functools
import math

import numpy as np
import jax
import jax.numpy as jnp
from jax import lax
from jax.experimental import pallas as pl
from jax.experimental.pallas import tpu as pltpu

F32 = jnp.float32
BF16 = jnp.bfloat16

D_MODEL = 1024
DEPTH = 2
HEAD_DIM = 64
DA_HEADS = 4
DA_VDIM = 2 * HEAD_DIM
DA_QK = DA_HEADS * 2 * HEAD_DIM
DA_V = DA_HEADS * DA_VDIM
HY_CH = D_MODEL // 2
HY_ORDER = 2
HY_BANDS = 8
HY_HIDDEN = 64
HY_FAST_DECAY = 0.3
HY_SLOW_DECAY = 1.5
HY_TARGET = 1e-2
NA_HEADS = 8
GRID_W = 64
NA_ROWS = 8
NA_COLS = 16
NA_W = NA_HEADS * HEAD_DIM
WG_HEADS = 8
WG_KV_HEADS = 2
WG_WINDOW = 128
WG_BLOCK = 128
WG_Q = WG_HEADS * HEAD_DIM
WG_KV = WG_KV_HEADS * HEAD_DIM
T5_BUCKETS = 32
T5_MAX_DIST = 128
T5_HEADS = 8
MEM_HEADS = 4
MEM_W = MEM_HEADS * HEAD_DIM
D_FF = 4 * D_MODEL
EPS = 1e-6
NEG_INF = -1e30
LOG2E = 1.4426950408889634

DA_SUM_ROWS = 16
LANES = 128
FFT_LANES = 256
VMEM_LIMIT = 56 * 1024 * 1024


def _cparams(sem, vmem=None):
    return pltpu.CompilerParams(dimension_semantics=sem, vmem_limit_bytes=vmem or VMEM_LIMIT)


def _full(shape):
    n = len(shape)
    return pl.BlockSpec(shape, lambda *_: (0,) * n)


def _dot(a, b):
    return jnp.dot(a, b, preferred_element_type=F32)


def _dot_nt(a, b):
    return lax.dot_general(a, b, (((1,), (1,)), ((), ())), preferred_element_type=F32)


def _group_gmat():
    g = np.arange(LANES) // HEAD_DIM
    return jnp.asarray((g[:, None] == g[None, :]).astype(np.float32), dtype=BF16)


def _rms_rows(x, g):
    ms = jnp.mean(x * x, axis=-1, keepdims=True)
    return x * lax.rsqrt(ms + EPS) * g


def _group_rms(y, gmat, gain):
    outs = []
    for c in range(y.shape[1] // LANES):
        yc = y[:, c * LANES:(c + 1) * LANES]
        ss = _dot((yc * yc).astype(BF16), gmat)
        outs.append(yc * lax.rsqrt(ss * (1.0 / HEAD_DIM) + EPS) * gain[:, c * LANES:(c + 1) * LANES])
    return outs


def _inproj_odd_kernel(x_ref, g_ref, wk_ref, gain_ref, gmat_ref, wv_ref, wq_ref, k_ref, v_ref, qt_ref):
    hn = _rms_rows(x_ref[...], g_ref[...]).astype(BF16)
    yk = _dot(hn, wk_ref[...])
    for c, yc in enumerate(_group_rms(yk, gmat_ref[...], gain_ref[...])):
        k_ref[:, c * LANES:(c + 1) * LANES] = yc.astype(k_ref.dtype)
    v_ref[...] = _dot(hn, wv_ref[...]).astype(v_ref.dtype)
    yt = _dot_nt(wq_ref[...], hn)
    nq, tm = yt.shape
    q = yt.reshape(nq // HEAD_DIM, HEAD_DIM, tm)
    ms = jnp.mean(q * q, axis=1, keepdims=True)
    qt_ref[0] = (q * lax.rsqrt(ms + EPS)).reshape(nq, tm).astype(qt_ref.dtype)


def _inproj_odd(x2d, g, w_k, k_gain, w_v, w_q, batch, tm=512):
    M, D = x2d.shape
    L = M // batch
    nt = L // tm
    n_k, n_v, n_q = w_k.shape[1], w_v.shape[1], w_q.shape[0]
    return pl.pallas_call(
        _inproj_odd_kernel, name="inproj_odd", grid=(M // tm,),
        in_specs=[pl.BlockSpec((tm, D), lambda i: (i, 0)), _full((1, D)), _full((D, n_k)), _full((1, n_k)),
                  _full((LANES, LANES)), _full((D, n_v)), _full((n_q, D))],
        out_specs=[pl.BlockSpec((tm, n_k), lambda i: (i, 0)), pl.BlockSpec((tm, n_v), lambda i: (i, 0)),
                   pl.BlockSpec((1, n_q, tm), lambda i: (i // nt, 0, i % nt))],
        out_shape=[jax.ShapeDtypeStruct((M, n_k), BF16), jax.ShapeDtypeStruct((M, n_v), BF16),
                   jax.ShapeDtypeStruct((batch, n_q, L), BF16)],
        compiler_params=_cparams(("parallel",)))(
            x2d, g.reshape(1, D), w_k, k_gain.reshape(1, n_k), _group_gmat(), w_v, w_q)


def _inproj_even_kernel(x_ref, g_ref, wk_ref, gain_ref, gmat_ref, wcm_ref, k_ref, qt_ref, vt_ref, ut_ref):
    hn = _rms_rows(x_ref[...], g_ref[...]).astype(BF16)
    yk = _dot(hn, wk_ref[...])
    for c, yc in enumerate(_group_rms(yk, gmat_ref[...], gain_ref[...])):
        k_ref[:, c * LANES:(c + 1) * LANES] = yc.astype(k_ref.dtype)
    yt = _dot_nt(wcm_ref[...], hn)
    tm = yt.shape[1]
    q = yt[:DA_QK].reshape(DA_QK // HEAD_DIM, HEAD_DIM, tm)
    ms = jnp.mean(q * q, axis=1, keepdims=True)
    qt_ref[0, 0] = (q * lax.rsqrt(ms + EPS)).reshape(DA_QK, tm).astype(qt_ref.dtype)
    vt_ref[0, 0] = yt[DA_QK:DA_QK + DA_V].astype(vt_ref.dtype)
    ut_ref[0] = yt[DA_QK + DA_V:]


def _inproj_even(x2d, g, w_k, k_gain, w_cm, batch, tm):
    M, D = x2d.shape
    L = M // batch
    nt = L // tm
    n_u = w_cm.shape[0] - DA_QK - DA_V
    return pl.pallas_call(
        _inproj_even_kernel, name="inproj_even", grid=(M // tm,),
        in_specs=[pl.BlockSpec((tm, D), lambda i: (i, 0)), _full((1, D)), _full((D, DA_QK)), _full((1, DA_QK)),
                  _full((LANES, LANES)), _full(w_cm.shape)],
        out_specs=[pl.BlockSpec((tm, DA_QK), lambda i: (i, 0)),
                   pl.BlockSpec((1, 1, DA_QK, tm), lambda i: (i // nt, i % nt, 0, 0)),
                   pl.BlockSpec((1, 1, DA_V, tm), lambda i: (i // nt, i % nt, 0, 0)),
                   pl.BlockSpec((1, n_u, tm), lambda i: (i // nt, 0, i % nt))],
        out_shape=[jax.ShapeDtypeStruct((M, DA_QK), BF16),
                   jax.ShapeDtypeStruct((batch, nt, DA_QK, tm), BF16),
                   jax.ShapeDtypeStruct((batch, nt, DA_V, tm), BF16),
                   jax.ShapeDtypeStruct((batch, n_u, L), F32)],
        compiler_params=_cparams(("parallel",)))(
            x2d, g.reshape(1, D), w_k, k_gain.reshape(1, DA_QK), _group_gmat(), w_cm)


def _outproj_kernel(x_ref, a_ref, b_ref, wa_ref, wb_ref, o_ref, *, b_cm):
    b = b_ref[0].T.astype(BF16) if b_cm else b_ref[...]
    o_ref[...] = x_ref[...] + _dot(a_ref[...], wa_ref[...]) + _dot(b, wb_ref[...])


def _outproj(x2d, a, b, wa, wb, *, b_cm, batch=None, tm=512):
    M, D = x2d.shape
    if b_cm:
        L = M // batch
        nt = L // tm
        b_spec = pl.BlockSpec((1, b.shape[1], tm), lambda i: (i // nt, 0, i % nt))
    else:
        b_spec = pl.BlockSpec((tm, b.shape[1]), lambda i: (i, 0))
    return pl.pallas_call(
        functools.partial(_outproj_kernel, b_cm=b_cm), name="outproj",
        grid=(M // tm,),
        in_specs=[pl.BlockSpec((tm, D), lambda i: (i, 0)), pl.BlockSpec((tm, a.shape[1]), lambda i: (i, 0)),
                  b_spec, _full(wa.shape), _full(wb.shape)],
        out_specs=pl.BlockSpec((tm, D), lambda i: (i, 0)),
        out_shape=jax.ShapeDtypeStruct((M, D), F32),
        compiler_params=_cparams(("parallel",)))(x2d, a, b, wa, wb)


def _mlp_kernel(x_ref, g_ref, w1_ref, w2_ref, o_ref, hn_ref):
    j = pl.program_id(1)

    @pl.when(j == 0)
    def _():
        x = x_ref[...]
        hn_ref[...] = _rms_rows(x, g_ref[...]).astype(BF16)
        o_ref[...] = x

    h = _dot(hn_ref[...], w1_ref[...])
    h = jnp.square(jnp.maximum(h, 0.0))
    o_ref[...] += _dot(h.astype(BF16), w2_ref[...])


def _mlp(x2d, g, w1, w2, tm=1024, tf=1024):
    M, D = x2d.shape
    F = w1.shape[1]
    tm = min(tm, M)
    return pl.pallas_call(
        _mlp_kernel, name="mlp",
        grid=(M // tm, F // tf),
        in_specs=[pl.BlockSpec((tm, D), lambda i, j: (i, 0)), _full((1, D)),
                  pl.BlockSpec((D, tf), lambda i, j: (0, j)), pl.BlockSpec((tf, D), lambda i, j: (j, 0))],
        out_specs=pl.BlockSpec((tm, D), lambda i, j: (i, 0)),
        out_shape=jax.ShapeDtypeStruct((M, D), F32),
        scratch_shapes=[pltpu.VMEM((tm, D), BF16)],
        compiler_params=_cparams(("parallel", "arbitrary")))(x2d, g.reshape(1, D), w1, w2)


def _memkv_kernel(m_ref, g_ref, wk_ref, gain_ref, gmat_ref, wvt_ref, k_ref, vt_ref):
    mn = _rms_rows(m_ref[0], g_ref[...]).astype(BF16)
    kk = _dot(mn, wk_ref[...])
    for c, kc in enumerate(_group_rms(kk, gmat_ref[...], gain_ref[...])):
        k_ref[0, :, c * LANES:(c + 1) * LANES] = kc.astype(BF16)
    vt_ref[0] = _dot_nt(wvt_ref[...], mn).astype(BF16)


def _memkv(mem, g, wk, kgain, wvt):
    B, M, D = mem.shape
    return pl.pallas_call(
        _memkv_kernel, name="memkv", grid=(B,),
        in_specs=[pl.BlockSpec((1, M, D), lambda b: (b, 0, 0)), _full((1, D)), _full(wk.shape),
                  _full((1, MEM_W)), _full((LANES, LANES)), _full(wvt.shape)],
        out_specs=[pl.BlockSpec((1, M, MEM_W), lambda b: (b, 0, 0)), pl.BlockSpec((1, MEM_W, M), lambda b: (b, 0, 0))],
        out_shape=[jax.ShapeDtypeStruct((B, M, MEM_W), BF16), jax.ShapeDtypeStruct((B, MEM_W, M), BF16)],
        compiler_params=_cparams(("parallel",)))(
            mem, g.reshape(1, D), wk, kgain.reshape(1, MEM_W), _group_gmat(), wvt)


def _memattn_kernel(x_ref, g_ref, wqt_ref, k_ref, vt_ref, wo_ref, o_ref, ot_ref):
    x = x_ref[0]
    tm = x.shape[0]
    hn = _rms_rows(x, g_ref[...]).astype(BF16)
    q = _dot_nt(wqt_ref[...], hn).reshape(MEM_HEADS, HEAD_DIM, tm)
    ms = jnp.mean(q * q, axis=1, keepdims=True)
    qn = (q * lax.rsqrt(ms + EPS)).astype(BF16)
    k = k_ref[0]
    vt = vt_ref[0]
    for hp in range(MEM_HEADS // 2):
        ha, hb = 2 * hp, 2 * hp + 1
        s = _dot(k[:, hp * LANES:(hp + 1) * LANES], _blockdiag_q(qn[ha], qn[hb]))
        e = jnp.exp2(s - jnp.max(s, axis=0, keepdims=True))
        p = (e * (1.0 / jnp.sum(e, axis=0, keepdims=True))).astype(BF16)
        ot_ref[ha * HEAD_DIM:(ha + 1) * HEAD_DIM, :] = _dot(vt[ha * HEAD_DIM:(ha + 1) * HEAD_DIM], p[:, :tm])
        ot_ref[hb * HEAD_DIM:(hb + 1) * HEAD_DIM, :] = _dot(vt[hb * HEAD_DIM:(hb + 1) * HEAD_DIM], p[:, tm:])
    o_ref[0] = x + _dot_tn(ot_ref[...].astype(BF16), wo_ref[...])


def _memattn(x, kn, vt, g, wqt, wo, tm=512):
    B, L, D = x.shape
    M = kn.shape[1]
    return pl.pallas_call(
        _memattn_kernel, name="memattn", grid=(B, L // tm),
        in_specs=[pl.BlockSpec((1, tm, D), lambda b, i: (b, i, 0)), _full((1, D)), _full(wqt.shape),
                  pl.BlockSpec((1, M, MEM_W), lambda b, i: (b, 0, 0)),
                  pl.BlockSpec((1, MEM_W, M), lambda b, i: (b, 0, 0)), _full(wo.shape)],
        out_specs=pl.BlockSpec((1, tm, D), lambda b, i: (b, i, 0)),
        out_shape=jax.ShapeDtypeStruct((B, L, D), F32),
        scratch_shapes=[pltpu.VMEM((MEM_W, tm), F32)],
        compiler_params=_cparams(("parallel", "parallel")))(x, g.reshape(1, D), wqt, kn, vt, wo)


def _t5_bucket_np(rel):
    half = T5_BUCKETS // 2
    exact = half // 2
    n = np.abs(rel)
    nf = np.maximum(n, 1).astype(np.float64)
    large = exact + (np.log(nf / exact) / math.log(T5_MAX_DIST / exact) * (half - exact)).astype(np.int32)
    large = np.minimum(large, half - 1)
    return (np.where(rel > 0, half, 0) + np.where(n < exact, n, large)).astype(np.int32)


def _t5bias_kernel(table_ref, bucket_ref, mask_ref, o_ref, *, scale):
    h = pl.program_id(0)
    bucket = bucket_ref[...]
    acc = jnp.zeros(bucket.shape, F32)
    for b in range(T5_BUCKETS):
        acc = jnp.where(bucket == b, table_ref[b, h], acc)
    o_ref[0] = acc * scale + mask_ref[...]


def _t5bias(table, rel, mask=None, scale=1.0):
    bucket = jnp.asarray(_t5_bucket_np(rel))
    mask = jnp.zeros(rel.shape, F32) if mask is None else jnp.asarray(mask, F32)
    nd = rel.ndim
    return pl.pallas_call(
        functools.partial(_t5bias_kernel, scale=scale), name="t5bias", grid=(T5_HEADS,),
        in_specs=[pl.BlockSpec(memory_space=pltpu.SMEM), _full(rel.shape), _full(rel.shape)],
        out_specs=pl.BlockSpec((1,) + rel.shape, lambda h: (h,) + (0,) * nd),
        out_shape=jax.ShapeDtypeStruct((T5_HEADS,) + rel.shape, F32),
        compiler_params=_cparams(("arbitrary",)))(table, bucket, mask)


def _diffattn_kernel(lam_ref, cfar_ref, q_ref, k_ref, v_ref, bias_ref, sg_ref, o_ref, qp_ref, sa_ref, sb_ref,
                     xa_ref, xb_ref, m_ref, acc_ref, *, lam_init, T, nkv):
    h = pl.program_id(1)
    i = pl.program_id(2)
    q = q_ref[0, 0]
    row = lax.broadcasted_iota(jnp.int32, q.shape, 0)
    zero = jnp.zeros(q.shape, q.dtype)
    qp_ref[0] = jnp.where(row < HEAD_DIM, q, zero)
    qp_ref[1] = jnp.where(row >= HEAD_DIM, q, zero)
    m_ref[...] = jnp.full(m_ref.shape, NEG_INF, F32)
    acc_ref[...] = jnp.zeros(acc_ref.shape, F32)

    def scores(jj, s_ref, x_ref):
        k = k_ref[0, pl.ds(pl.multiple_of(jj * T, T), T), :]
        for m in range(2):
            s = _dot(k, qp_ref[m])
            s_ref[m] = s
            x_ref[m] = jnp.max(s, axis=0, keepdims=True)

    def step(jj, s_cur, x_cur, s_nxt, x_nxt):
        d = jj - i
        near = jnp.abs(d) <= 1

        @pl.when(near)
        def _():
            for m in range(2):
                s = s_cur[m] + bias_ref[m, d + 1]
                s_cur[m] = s
                x_cur[m] = jnp.max(s, axis=0, keepdims=True)

        scores(jnp.minimum(jj + 1, nkv - 1), s_nxt, x_nxt)
        v = jnp.concatenate([v_ref[0, jj], jnp.ones((DA_SUM_ROWS, T), BF16)], axis=0)
        for m in range(2):
            c = jnp.where(near, 0.0, jnp.where(d < 0, cfar_ref[0, 2 * h + m], cfar_ref[1, 2 * h + m]))
            m_prev = m_ref[m]
            m_new = jnp.maximum(m_prev, x_cur[m] + c)
            alpha = jnp.exp2(m_prev - m_new)
            p = jnp.exp2((s_cur[m] - (m_new - c)).astype(BF16))
            acc_ref[m] = alpha * acc_ref[m] + _dot(v, p)
            m_ref[m] = m_new

    scores(0, sa_ref, xa_ref)

    def body(t, carry):
        step(2 * t, sa_ref, xa_ref, sb_ref, xb_ref)
        step(2 * t + 1, sb_ref, xb_ref, sa_ref, xa_ref)
        return carry

    lax.fori_loop(0, nkv // 2, body, 0)

    lf = lam_ref[...]
    lam = (jnp.exp(jnp.sum(lf[0:1] * lf[1:2], axis=-1, keepdims=True))
           - jnp.exp(jnp.sum(lf[2:3] * lf[3:4], axis=-1, keepdims=True)) + lam_init)
    num = [acc_ref[m, 0:DA_VDIM, :] for m in range(2)]
    den = [acc_ref[m, DA_VDIM:DA_VDIM + 1, :] for m in range(2)]
    o = num[0] / den[0] - lam * (num[1] / den[1])
    ms = jnp.mean(o * o, axis=0, keepdims=True)
    o = o * lax.rsqrt(ms + EPS) * sg_ref[...] * (1.0 - lam_init)
    o_ref[0] = o.T.astype(o_ref.dtype)


def _diffattn(qt, k, vt, bias, cfar, lam, sub_gain, lam_init):
    B, nb, _, T = qt.shape
    L = nb * T
    assert nb % 2 == 0
    return pl.pallas_call(
        functools.partial(_diffattn_kernel, lam_init=lam_init, T=T, nkv=nb), name="diffattn",
        grid=(B, DA_HEADS, nb),
        in_specs=[_full((4, HEAD_DIM)), pl.BlockSpec(memory_space=pltpu.SMEM),
                  pl.BlockSpec((1, 1, LANES, T), lambda b, h, i: (b, i, h, 0)),
                  pl.BlockSpec((1, L, LANES), lambda b, h, i: (b, 0, h)),
                  pl.BlockSpec((1, nb, LANES, T), lambda b, h, i: (b, 0, h, 0)),
                  pl.BlockSpec((2, 3, T, T), lambda b, h, i: (h, 0, 0, 0)),
                  _full((DA_VDIM, 1))],
        out_specs=pl.BlockSpec((1, T, LANES), lambda b, h, i: (b, i, h)),
        out_shape=jax.ShapeDtypeStruct((B, L, DA_V), BF16),
        scratch_shapes=[pltpu.VMEM((2, LANES, T), BF16), pltpu.VMEM((2, T, T), F32), pltpu.VMEM((2, T, T), F32),
                        pltpu.VMEM((2, 1, T), F32), pltpu.VMEM((2, 1, T), F32),
                        pltpu.VMEM((2, 1, T), F32), pltpu.VMEM((2, DA_VDIM + DA_SUM_ROWS, T), F32)],
        compiler_params=_cparams(("parallel", "parallel", "arbitrary")))(
            lam, cfar, qt, k, vt, bias, sub_gain.reshape(DA_VDIM, 1))


def _dot_tn(a, b):
    return lax.dot_general(a, b, (((0,), (0,)), ((), ())), preferred_element_type=F32)


def _blockdiag_q(qa, qb):
    z = jnp.zeros(qa.shape, qa.dtype)
    return jnp.concatenate([jnp.concatenate([qa, z], axis=0), jnp.concatenate([z, qb], axis=0)], axis=1)


def _wgqa_kernel(sink_ref, q_ref, kp_ref, kc_ref, kn_ref, vp_ref, vc_ref, vn_ref, bias_ref, o_ref, kw_ref, vw_ref):
    n = pl.program_id(1)
    nb = pl.num_programs(1)
    W = WG_BLOCK
    kw_ref[0:W] = kp_ref[0]
    kw_ref[W:2 * W] = kc_ref[0]
    kw_ref[2 * W:3 * W] = kn_ref[0]
    vw_ref[0:W] = vp_ref[0]
    vw_ref[W:2 * W] = vc_ref[0]
    vw_ref[2 * W:3 * W] = vn_ref[0]
    key = lax.broadcasted_iota(jnp.int32, (3 * W, 2 * W), 0)
    valid = jnp.logical_and(jnp.logical_or(n > 0, key >= W), jnp.logical_or(n < nb - 1, key < 2 * W))
    lane2 = lax.broadcasted_iota(jnp.int32, (1, 2 * W), 1)
    lane1 = lax.broadcasted_iota(jnp.int32, (W, LANES), 1)
    grp = WG_HEADS // WG_KV_HEADS
    kw = kw_ref[...]
    sink_rows = 16
    ones = jnp.ones((3 * W + sink_rows, LANES), BF16)
    vw = jnp.concatenate([jnp.concatenate([vw_ref[...], jnp.zeros((sink_rows, LANES), BF16)], axis=0), ones], axis=1)
    first = lax.broadcasted_iota(jnp.int32, (sink_rows, 2 * W), 0) == 0
    for h in range(grp):
        qa = q_ref[0, h * HEAD_DIM:(h + 1) * HEAD_DIM, :]
        qb = q_ref[0, (h + grp) * HEAD_DIM:(h + grp + 1) * HEAD_DIM, :]
        s = _dot(kw, _blockdiag_q(qa, qb))
        s = s + jnp.concatenate([bias_ref[h], bias_ref[h + grp]], axis=1)
        s = jnp.where(valid, s, NEG_INF)
        sk = jnp.where(lane2 < W, sink_ref[h], sink_ref[h + grp]) * LOG2E
        mx = jnp.maximum(jnp.max(s, axis=0, keepdims=True), sk)
        p = jnp.exp2((s - mx).astype(BF16))
        p_sink = jnp.where(first, jnp.exp2(sk - mx), 0.0).astype(BF16)
        o2 = _dot_tn(jnp.concatenate([p, p_sink], axis=0), vw)
        o2 = o2[:, :LANES] / o2[:, LANES:]
        o_ref[0, :, h * LANES:(h + 1) * LANES] = jnp.where(lane1 < HEAD_DIM, o2[:W], o2[W:]).astype(o_ref.dtype)


def _wgqa(qt, kk, vv, bias, sink, q_row, k_col, v_col):
    B, L, _ = kk.shape
    W = WG_BLOCK
    nb = L // W
    prev = lambda n: jnp.maximum(n - 1, 0)
    nxt = lambda n: jnp.minimum(n + 1, nb - 1)
    return pl.pallas_call(
        _wgqa_kernel, name="wgqa", grid=(B, nb),
        in_specs=[pl.BlockSpec(memory_space=pltpu.SMEM),
                  pl.BlockSpec((1, WG_Q, W), lambda b, n: (b, q_row, n)),
                  pl.BlockSpec((1, W, LANES), lambda b, n: (b, prev(n), k_col)),
                  pl.BlockSpec((1, W, LANES), lambda b, n: (b, n, k_col)),
                  pl.BlockSpec((1, W, LANES), lambda b, n: (b, nxt(n), k_col)),
                  pl.BlockSpec((1, W, LANES), lambda b, n: (b, prev(n), v_col)),
                  pl.BlockSpec((1, W, LANES), lambda b, n: (b, n, v_col)),
                  pl.BlockSpec((1, W, LANES), lambda b, n: (b, nxt(n), v_col)),
                  _full(bias.shape)],
        out_specs=pl.BlockSpec((1, W, WG_Q), lambda b, n: (b, n, 0)),
        out_shape=jax.ShapeDtypeStruct((B, L, WG_Q), BF16),
        scratch_shapes=[pltpu.VMEM((3 * W, LANES), BF16), pltpu.VMEM((3 * W, LANES), BF16)],
        compiler_params=_cparams(("parallel", "arbitrary")))(sink, qt, kk, kk, kk, vv, vv, vv, bias)


NA_KROWS = NA_ROWS + 1
NA_CASES = 5
_NA_CASE_GEOM = (((0, 0), 7), ((0, 0), 5), ((0, 1), 3), ((1, 1), 2), ((1, 1), 0))


def _nabias_kernel(rpb_ref, o_ref):
    h = pl.program_id(0)
    cc = lax.broadcasted_iota(jnp.int32, (GRID_W, LANES), 0)
    lane = lax.broadcasted_iota(jnp.int32, (GRID_W, LANES), 1)
    c = lane % GRID_W
    second = lane >= GRID_W
    c_start = jnp.clip(c - NA_COLS // 2, 0, GRID_W - NA_COLS)
    valid = jnp.logical_and(cc >= c_start, cc < c_start + NA_COLS)
    dc = cc - c + (NA_COLS - 1)
    neg = jnp.full((GRID_W, LANES), NEG_INF, F32)
    tiles = {}
    for dr in range(1, 2 * NA_ROWS - 1):
        acc = neg
        for d in range(2 * NA_COLS - 1):
            val = jnp.where(second, rpb_ref[h, dr - 1, d], rpb_ref[h, dr, d]) * LOG2E
            acc = jnp.where(jnp.logical_and(valid, dc == d), val, acc)
        tiles[dr] = acc
    for case, (wstart, a) in enumerate(_NA_CASE_GEOM):
        for i in range(NA_KROWS):
            in0 = wstart[0] <= i < wstart[0] + NA_ROWS
            in1 = wstart[1] <= i < wstart[1] + NA_ROWS
            t = tiles[i + a] if (in0 or in1) else neg
            if in1 and not in0:
                t = jnp.where(second, t, neg)
            if in0 and not in1:
                t = jnp.where(second, neg, t)
            o_ref[case, 0, i * GRID_W:(i + 1) * GRID_W, :] = t


def _nabias(rpb):
    return pl.pallas_call(
        _nabias_kernel, name="nabias", grid=(NA_HEADS,),
        in_specs=[pl.BlockSpec(memory_space=pltpu.SMEM)],
        out_specs=pl.BlockSpec((NA_CASES, 1, NA_KROWS * GRID_W, LANES), lambda h: (0, h, 0, 0)),
        out_shape=jax.ShapeDtypeStruct((NA_CASES, NA_HEADS, NA_KROWS * GRID_W, LANES), F32),
        compiler_params=_cparams(("arbitrary",)))(rpb)


def _natten_kernel(q_ref, kp_ref, kc_ref, kn_ref, vp_ref, vc_ref, vn_ref, bias_ref, o_ref, kw_ref, vw_ref,
                   sa_ref, sb_ref, *, rows):
    i = pl.program_id(1)
    RB = NA_ROWS
    T = RB * GRID_W
    KW = NA_KROWS * GRID_W
    kw_ref[0:T] = kp_ref[0]
    kw_ref[T:2 * T] = kc_ref[0]
    kw_ref[2 * T:3 * T] = kn_ref[0]
    vw_ref[0:T] = vp_ref[0]
    vw_ref[T:2 * T] = vc_ref[0]
    vw_ref[2 * T:3 * T] = vn_ref[0]
    lane1 = lax.broadcasted_iota(jnp.int32, (LANES, LANES), 1)
    ones = jnp.ones((KW, LANES), BF16)
    units = [(pi, hp) for pi in range(RB // 2) for hp in range(NA_HEADS // 2)]

    def window(pi):
        r = i * RB + 2 * pi
        r_lo = jnp.clip(r - NA_ROWS // 2, 0, rows - NA_KROWS)
        off = pl.multiple_of((r_lo - i * RB + RB) * GRID_W, GRID_W)
        case = jnp.where(r == 0, 0, jnp.where(r == 2, 1, jnp.where(r == rows - 4, 3, jnp.where(r == rows - 2, 4, 2))))
        return off, case

    def scores(unit, s_ref):
        pi, hp = unit
        off, case = window(pi)
        ha, hb = 2 * hp, 2 * hp + 1
        qs = slice(pi * LANES, (pi + 1) * LANES)
        kw = kw_ref[pl.ds(off, KW), hp * LANES:(hp + 1) * LANES]
        qa = q_ref[0, ha * HEAD_DIM:(ha + 1) * HEAD_DIM, qs]
        qb = q_ref[0, hb * HEAD_DIM:(hb + 1) * HEAD_DIM, qs]
        s = _dot(kw, _blockdiag_q(qa, qb))
        s_ref[...] = s + jnp.concatenate([bias_ref[case, ha], bias_ref[case, hb]], axis=1)

    def finish(unit, s_ref):
        pi, hp = unit
        off, _ = window(pi)
        ls = slice(hp * LANES, (hp + 1) * LANES)
        s = s_ref[...]
        p = jnp.exp2((s - jnp.max(s, axis=0, keepdims=True)).astype(BF16))
        vw = jnp.concatenate([vw_ref[pl.ds(off, KW), ls], ones], axis=1)
        o2 = _dot_tn(p, vw)
        o2 = o2[:, :LANES] / o2[:, LANES:]
        o_ref[0, pi * LANES:(pi + 1) * LANES, ls] = jnp.where(lane1 < HEAD_DIM, o2[:LANES], o2[LANES:]).astype(
            o_ref.dtype)

    slots = (sa_ref, sb_ref)
    scores(units[0], slots[0])
    for u, unit in enumerate(units):
        if u + 1 < len(units):
            scores(units[u + 1], slots[(u + 1) % 2])
        finish(unit, slots[u % 2])


def _natten(qt, kk, vv, bias):
    B, L, _ = kk.shape
    rows = L // GRID_W
    assert rows >= NA_KROWS + 1 and rows % NA_ROWS == 0
    T = NA_ROWS * GRID_W
    nb = L // T
    prev = lambda n: jnp.maximum(n - 1, 0)
    nxt = lambda n: jnp.minimum(n + 1, nb - 1)
    return pl.pallas_call(
        functools.partial(_natten_kernel, rows=rows), name="natten", grid=(B, nb),
        in_specs=[pl.BlockSpec((1, NA_W, T), lambda b, n: (b, 0, n)),
                  pl.BlockSpec((1, T, NA_W), lambda b, n: (b, prev(n), 0)),
                  pl.BlockSpec((1, T, NA_W), lambda b, n: (b, n, 0)),
                  pl.BlockSpec((1, T, NA_W), lambda b, n: (b, nxt(n), 0)),
                  pl.BlockSpec((1, T, NA_W), lambda b, n: (b, prev(n), 0)),
                  pl.BlockSpec((1, T, NA_W), lambda b, n: (b, n, 0)),
                  pl.BlockSpec((1, T, NA_W), lambda b, n: (b, nxt(n), 0)),
                  _full(bias.shape)],
        out_specs=pl.BlockSpec((1, T, NA_W), lambda b, n: (b, n, 0)),
        out_shape=jax.ShapeDtypeStruct((B, L, NA_W), BF16),
        scratch_shapes=[pltpu.VMEM((3 * T, NA_W), BF16), pltpu.VMEM((3 * T, NA_W), BF16)]
                       + [pltpu.VMEM((NA_KROWS * GRID_W, 2 * LANES), F32)] * 2,
        compiler_params=_cparams(("parallel", "arbitrary")))(qt, kk, kk, kk, vv, vv, vv, bias)


def _split(a):
    hi = a.astype(BF16)
    return hi, (a - hi.astype(F32)).astype(BF16)


def _np_split(a):
    a = np.asarray(a, np.float32)
    hi = a.astype(BF16)
    lo = (a - hi.astype(np.float32)).astype(BF16)
    return hi, lo


def _dot3_cl(ch, cl, x):
    xh, xl = _split(x)
    return _dot(ch, xh) + _dot(cl, xh) + _dot(ch, xl)


def _dot3_cr(x, ch, cl):
    xh, xl = _split(x)
    return _dot(xh, ch) + _dot(xl, ch) + _dot(xh, cl)


def _fft_consts(L):
    N2 = FFT_LANES
    N = 2 * L
    N1 = N // N2
    N1h = N1 // 2
    G = max(1, LANES // N1h)
    Gf = max(1, LANES // N1)
    k = np.arange(N1)
    ang1 = 2.0 * np.pi * np.outer(k, k) / N1
    C1, S1 = np.cos(ang1), np.sin(ang1)
    eye = np.eye
    KC, KS = np.kron(eye(G), C1[:, :N1h]), np.kron(eye(G), S1[:, :N1h])
    fa = np.block([[KC, KS], [-KS, KC]])
    KCh, KSh = np.kron(eye(G), C1[:N1h, :]), np.kron(eye(G), S1[:N1h, :])
    fg = np.block([[KCh, -KSh], [KSh, KCh]]) / N
    faf = np.concatenate([np.kron(eye(Gf), C1), -np.kron(eye(Gf), S1)], axis=0)
    n2 = np.arange(N2)
    angt = 2.0 * np.pi * np.outer(k, n2) / N
    twr, twi = np.cos(angt), -np.sin(angt)
    ang2 = 2.0 * np.pi * np.outer(n2, n2) / N2
    C2, S2 = np.cos(ang2), np.sin(ang2)
    m2f = np.block([[C2, -S2], [S2, C2]])
    m2i = np.block([[C2, S2], [-S2, C2]])
    return dict(N=N, N1=N1, N1h=N1h, N2=N2, G=G, Gf=Gf, fa=_np_split(fa), fg=_np_split(fg), faf=_np_split(faf),
                m2f=_np_split(m2f), m2i=_np_split(m2i),
                twr=np.asarray(twr, np.float32), twi=np.asarray(twi, np.float32))


def _hyfilt_kernel(w1t_ref, b1_ref, fr_ref, w2t_ref, b2_ref, w3t_ref, delta_ref, band_ref, o_ref, sum_ref, *, L, tn):
    s = pl.program_id(0)
    n = s * tn + lax.broadcasted_iota(jnp.int32, (1, tn), 1)
    t = jnp.where(n < L, n, 2 * L - n).astype(F32)
    t01 = t / float(max(L - 1, 1))
    w = (2.0 * math.pi) * t / float(L)
    ang = band_ref[...] * w
    cs, sn = jnp.cos(ang), -jnp.sin(ang)
    w1t = w1t_ref[...]
    h = w1t[:, 0:1] * t01
    for b in range(HY_BANDS):
        h = h + w1t[:, 1 + b:2 + b] * cs[b:b + 1] + w1t[:, 1 + HY_BANDS + b:2 + HY_BANDS + b] * sn[b:b + 1]
    fr = fr_ref[...]
    h = jnp.sin(fr[:, 0:1] * (h + b1_ref[...]))
    h = jnp.sin(fr[:, 1:2] * (jnp.dot(w2t_ref[...], h, preferred_element_type=F32,
                                      precision=lax.Precision.HIGHEST) + b2_ref[...]))
    y = jnp.dot(w3t_ref[0], h, preferred_element_type=F32, precision=lax.Precision.HIGHEST)
    y = y * jnp.exp(-delta_ref[...] * t01)
    y = jnp.where(n == L, 0.0, y)
    o_ref[...] = y
    a = jnp.abs(y)
    part = a[:, 0:LANES]
    for c in range(1, tn // LANES):
        part = part + a[:, c * LANES:(c + 1) * LANES]

    @pl.when(s == 0)
    def _():
        sum_ref[...] = jnp.zeros(sum_ref.shape, F32)

    sum_ref[...] += part


def _hyena_filters_td(L, w1, b1, freq, w2, b2, w3, tn=1024):
    OC = HY_ORDER * HY_CH
    w3t = jnp.transpose(w3.reshape(HY_HIDDEN, HY_ORDER, 2, HY_CH), (2, 1, 3, 0)).reshape(2, OC, HY_HIDDEN)
    min_decay = math.log(HY_TARGET) / HY_SLOW_DECAY
    max_decay = math.log(HY_TARGET) / HY_FAST_DECAY
    deltas = np.abs(np.linspace(min_decay, max_decay, HY_CH, dtype=np.float32))
    delta = jnp.asarray(np.tile(deltas, HY_ORDER).reshape(OC, 1))
    bands = jnp.asarray(np.linspace(1e-4, HY_BANDS - 1, HY_BANDS, dtype=np.float32).reshape(HY_BANDS, 1))
    N = 2 * L
    nh = L // tn
    return pl.pallas_call(
        functools.partial(_hyfilt_kernel, L=L, tn=tn), name="hyfilt", grid=(N // tn,),
        in_specs=[_full((HY_HIDDEN, 1 + 2 * HY_BANDS)), _full((HY_HIDDEN, 1)), _full((HY_HIDDEN, 2)),
                  _full((HY_HIDDEN, HY_HIDDEN)), _full((HY_HIDDEN, 1)),
                  pl.BlockSpec((1, OC, HY_HIDDEN), lambda s: (s // nh, 0, 0)),
                  _full((OC, 1)), _full((HY_BANDS, 1))],
        out_specs=[pl.BlockSpec((OC, tn), lambda s: (0, s)), _full((OC, LANES))],
        out_shape=[jax.ShapeDtypeStruct((OC, N), F32), jax.ShapeDtypeStruct((OC, LANES), F32)],
        compiler_params=_cparams(("arbitrary",)))(
            w1.T, b1.reshape(HY_HIDDEN, 1), freq.T, w2.T, b2.reshape(HY_HIDDEN, 1), w3t, delta, bands)


def _hyspec_kernel(k_ref, sum_ref, fah_ref, fal_ref, twr_ref, twi_ref, m2h_ref, m2l_ref, kr_ref, ki_ref, p_ref,
                   *, R, N1, N2, Gf):
    tot = jnp.sum(sum_ref[...], axis=-1, keepdims=True)
    mi = Gf * N1
    for g in range(R // Gf):
        xs = []
        for r in range(Gf):
            row = g * Gf + r
            xs.append(k_ref[row] / tot[row:row + 1, :])
        x = xs[0] if Gf == 1 else jnp.concatenate(xs, axis=0)
        a = _dot3_cl(fah_ref[...], fal_ref[...], x)
        ar, ai = a[:mi], a[mi:]
        twr, twi = twr_ref[...], twi_ref[...]
        p_ref[g * mi:(g + 1) * mi, 0:N2] = ar * twr - ai * twi
        p_ref[g * mi:(g + 1) * mi, N2:2 * N2] = ar * twi + ai * twr
    b = _dot3_cr(p_ref[...], m2h_ref[...], m2l_ref[...])
    kr_ref[...] = b[:, :N2].reshape(R, N1, N2)
    ki_ref[...] = b[:, N2:].reshape(R, N1, N2)


def _hyena_spectra(ktd, ksum, fc):
    OC = ktd.shape[0]
    N1, N2, Gf = fc["N1"], fc["N2"], fc["Gf"]
    R = 1024 // N1
    twr = jnp.asarray(np.tile(fc["twr"], (Gf, 1)))
    twi = jnp.asarray(np.tile(fc["twi"], (Gf, 1)))
    fah, fal = fc["faf"]
    m2h, m2l = fc["m2f"]
    return pl.pallas_call(
        functools.partial(_hyspec_kernel, R=R, N1=N1, N2=N2, Gf=Gf), name="hyspec", grid=(OC // R,),
        in_specs=[pl.BlockSpec((R, N1, N2), lambda i: (i, 0, 0)), pl.BlockSpec((R, LANES), lambda i: (i, 0)),
                  _full(fah.shape), _full(fal.shape), _full(twr.shape), _full(twi.shape),
                  _full(m2h.shape), _full(m2l.shape)],
        out_specs=[pl.BlockSpec((R, N1, N2), lambda i: (i, 0, 0))] * 2,
        out_shape=[jax.ShapeDtypeStruct((OC, N1, N2), F32)] * 2,
        scratch_shapes=[pltpu.VMEM((R * N1, 2 * N2), F32)],
        compiler_params=_cparams(("parallel",)))(ktd.reshape(OC, N1, N2), ksum, fah, fal, twr, twi, m2h, m2l)


def _hyconv_kernel(cw_ref, cb_ref, sk_ref, u_ref, kr_ref, ki_ref, fa_ref, fg_ref, twr_ref, twi_ref, mf_ref, mi_ref,
                   o_ref, z_ref, x1_ref, x2_ref, c_ref, p_ref, wcol_ref, *, R, N1, N2, G):
    N1h = N1 // 2
    cb = pl.program_id(0)
    ng = R // G
    mi = G * N1h
    mo = G * N1
    n_parts = 3
    i_bias = 3 * n_parts
    i_skip = i_bias + n_parts

    @pl.when(pl.program_id(1) == 0)
    def _():
        def fill(r, carry):
            ch = cb * R + r
            rs = pl.ds(pl.multiple_of(r * N1h, 8), N1h)
            for part in range(n_parts):
                c = part * HY_CH + ch
                for tap in range(3):
                    wcol_ref[tap * n_parts + part, rs, :] = jnp.full((N1h, LANES), cw_ref[tap, c], F32)
                wcol_ref[i_bias + part, rs, :] = jnp.full((N1h, LANES), cb_ref[c], F32)
            for o in range(HY_ORDER):
                wcol_ref[i_skip + o, rs, :] = jnp.full((N1h, LANES), sk_ref[o, ch], F32)
            return carry
        lax.fori_loop(0, R, fill, 0)

    def wide(idx, g):
        w = wcol_ref[idx, g * mi:(g + 1) * mi, :]
        return jnp.concatenate([w] * (N2 // LANES), axis=1)

    lane = lax.broadcasted_iota(jnp.int32, (mi, N2), 1)
    sub = lax.broadcasted_iota(jnp.int32, (mi, N2), 0) % N1h
    first = jnp.logical_and(lane == 0, sub == 0)
    last = jnp.logical_and(lane == N2 - 1, sub == N1h - 1)

    def shortconv(u, part, g):
        rl = pltpu.roll(u, 1, 1)
        prev = jnp.where(lane == 0, pltpu.roll(rl, 1, 0), rl)
        prev = jnp.where(first, 0.0, prev)
        rr = pltpu.roll(u, N2 - 1, 1)
        nxt = jnp.where(lane == N2 - 1, pltpu.roll(rr, mi - 1, 0), rr)
        nxt = jnp.where(last, 0.0, nxt)
        return (prev * wide(part, g) + u * wide(n_parts + part, g) + nxt * wide(2 * n_parts + part, g)
                + wide(i_bias + part, g))

    for g in range(ng):
        for pbatch in range(2):
            rs = slice((g * 2 + pbatch) * mi, (g * 2 + pbatch + 1) * mi)
            for part, dst in enumerate((z_ref, x1_ref, x2_ref)):
                u = u_ref[0, pbatch, part, g * G:(g + 1) * G].reshape(mi, N2)
                dst[rs, :] = shortconv(u, part, g)

    def conv(order):
        twr, twi = twr_ref[...], twi_ref[...]
        for g in range(ng):
            a = _dot(fa_ref[...], z_ref[g * 2 * mi:(g + 1) * 2 * mi, :].astype(BF16))
            ar, ai = a[:mo], a[mo:]
            p_ref[g * mo:(g + 1) * mo, 0:N2] = (ar * twr - ai * twi).astype(BF16)
            p_ref[g * mo:(g + 1) * mo, N2:2 * N2] = (ar * twi + ai * twr).astype(BF16)
        b = _dot(p_ref[...], mf_ref[...])
        br, bi = b[:, :N2], b[:, N2:]
        kr = kr_ref[order].reshape(R * N1, N2)
        ki = ki_ref[order].reshape(R * N1, N2)
        p_ref[:, 0:N2] = (br * kr - bi * ki).astype(BF16)
        p_ref[:, N2:2 * N2] = (br * ki + bi * kr).astype(BF16)
        d = _dot(p_ref[...], mi_ref[...])
        dr, di = d[:, :N2], d[:, N2:]
        for g in range(ng):
            sl = slice(g * mo, (g + 1) * mo)
            x = jnp.concatenate([dr[sl] * twr + di[sl] * twi, di[sl] * twr - dr[sl] * twi], axis=0)
            c_ref[g * 2 * mi:(g + 1) * 2 * mi, :] = _dot(fg_ref[...], x.astype(BF16))

    def gate(order, x_ref, final):
        for g in range(ng):
            skv = wide(i_skip + order, g)
            for pbatch in range(2):
                rs = slice((g * 2 + pbatch) * mi, (g * 2 + pbatch + 1) * mi)
                zn = x_ref[rs, :] * (c_ref[rs, :] + skv * z_ref[rs, :])
                if final:
                    o_ref[0, pbatch, g * G:(g + 1) * G] = zn.reshape(G, N1h, N2)
                else:
                    z_ref[rs, :] = zn

    conv(0)
    gate(0, x1_ref, False)
    conv(1)
    gate(1, x2_ref, True)


def _hyena_conv(ucm, conv_w, conv_b, skip, kr, ki, fc):
    B = ucm.shape[0]
    C = HY_CH
    N1, N2, G = fc["N1"], fc["N2"], fc["G"]
    N1h = N1 // 2
    R = 1024 // N1
    u = ucm.reshape(B // 2, 2, 3, C, N1h, N2)
    twr = jnp.asarray(np.tile(fc["twr"], (G, 1)))
    twi = jnp.asarray(np.tile(fc["twi"], (G, 1)))
    consts = [fc["fa"][0], fc["fg"][0], twr, twi, fc["m2f"][0], fc["m2i"][0]]
    rows = R * N1h
    smem = pl.BlockSpec(memory_space=pltpu.SMEM)
    out = pl.pallas_call(
        functools.partial(_hyconv_kernel, R=R, N1=N1, N2=N2, G=G), name="hyconv",
        grid=(C // R, B // 2),
        in_specs=[smem, smem, smem,
                  pl.BlockSpec((1, 2, 3, R, N1h, N2), lambda c, p: (p, 0, 0, c, 0, 0)),
                  pl.BlockSpec((HY_ORDER, R, N1, N2), lambda c, p: (0, c, 0, 0)),
                  pl.BlockSpec((HY_ORDER, R, N1, N2), lambda c, p: (0, c, 0, 0))]
                 + [_full(a.shape) for a in consts],
        out_specs=pl.BlockSpec((1, 2, R, N1h, N2), lambda c, p: (p, 0, c, 0, 0)),
        out_shape=jax.ShapeDtypeStruct((B // 2, 2, C, N1h, N2), F32),
        scratch_shapes=[pltpu.VMEM((2 * rows, N2), F32)] * 4
                       + [pltpu.VMEM((R * N1, 2 * N2), BF16), pltpu.VMEM((4 * 3 + HY_ORDER, rows, LANES), F32)],
        compiler_params=_cparams(("parallel", "arbitrary")))(
            conv_w, conv_b, skip, u, kr.reshape(HY_ORDER, C, N1, N2), ki.reshape(HY_ORDER, C, N1, N2), *consts)
    return out.reshape(B, C, N1h * N2)


def _lambda_init(layer):
    return 0.8 - 0.6 * math.exp(-0.3 * layer)


def _tile_gain(g, reps, scale=1.0):
    return jnp.tile(g.astype(F32), reps) * scale


def _trunk(x, mem, p, shared):
    B, L, D = x.shape
    M = B * L
    scale = HEAD_DIM ** -0.5
    x2 = x.reshape(M, D)
    for layer in range(DEPTH):
        j = layer // 2
        if layer % 2 == 0:
            w_in = p["w_in_even"][j].astype(BF16)
            k_gain = _tile_gain(p["da_q_gain"][j] * p["da_k_gain"][j], DA_QK // HEAD_DIM, scale * LOG2E)
            w_cm = jnp.concatenate([w_in[:, :DA_QK], w_in[:, 2 * DA_QK:]], axis=1).T
            kk, qt, vt, ucm = _inproj_even(x2, p["norm_mix"][layer], w_in[:, DA_QK:2 * DA_QK], k_gain, w_cm,
                                           batch=B, tm=shared["da_T"])
            oa = _diffattn(qt, kk.reshape(B, L, DA_QK), vt, shared["da_bias"], shared["da_cfar"],
                           p["da_lambda"][j], p["da_sub_gain"][j], _lambda_init(layer))
            fc = _fft_consts(L)
            ktd, ksum = _hyena_filters_td(L, p["hy_w1"][j], p["hy_b1"][j], p["hy_freq"][j], p["hy_w2"][j],
                                          p["hy_b2"][j], p["hy_w3"][j])
            kr, ki = _hyena_spectra(ktd, ksum, fc)
            ob = _hyena_conv(ucm, p["hy_conv_w"][j], p["hy_conv_b"][j], p["hy_skip"][j], kr, ki, fc)
            w_out = p["w_out_even"][j].astype(BF16)
            x2 = _outproj(x2, oa.reshape(M, DA_V), ob, w_out[:DA_V], w_out[DA_V:], b_cm=True, batch=B)
        else:
            w_in = p["w_in_odd"][j].astype(BF16)
            c0 = 3 * NA_W
            c1 = c0 + WG_Q
            c2 = c1 + WG_KV
            w_k = jnp.concatenate([w_in[:, NA_W:2 * NA_W], w_in[:, c1:c2]], axis=1)
            w_v = jnp.concatenate([w_in[:, 2 * NA_W:c0], w_in[:, c2:]], axis=1)
            w_q = jnp.concatenate([w_in[:, :NA_W], w_in[:, c0:c1]], axis=1).T
            k_gain = jnp.concatenate([
                _tile_gain(p["na_q_gain"][j] * p["na_k_gain"][j], NA_HEADS, scale * LOG2E),
                _tile_gain(p["wg_q_gain"][j] * p["wg_k_gain"][j], WG_KV_HEADS, scale * LOG2E)])
            kk, vv, qt = _inproj_odd(x2, p["norm_mix"][layer], w_k, k_gain, w_v, w_q, batch=B)
            kk = kk.reshape(B, L, -1)
            vv = vv.reshape(B, L, -1)
            oc = _natten(qt, kk, vv, shared["na_bias"][j])
            od = _wgqa(qt, kk, vv, shared["wg_bias"], p["wg_sink"][j], q_row=NA_W // WG_Q,
                       k_col=NA_W // LANES, v_col=NA_W // LANES)
            w_out = p["w_out_odd"][j].astype(BF16)
            grp = WG_HEADS // WG_KV_HEADS
            head_order = np.stack([np.arange(grp), np.arange(grp) + grp], axis=1).reshape(-1)
            wd_rows = NA_W + (head_order[:, None] * HEAD_DIM + np.arange(HEAD_DIM)[None, :]).reshape(-1)
            x2 = _outproj(x2, oc.reshape(M, NA_W), od.reshape(M, WG_Q), w_out[:NA_W], w_out[wd_rows], b_cm=False)
        wkv = p["mem_wkv"][layer].astype(BF16)
        kn, vt = _memkv(mem, p["norm_memkv"][layer], wkv[:, :MEM_W],
                        _tile_gain(p["mem_q_gain"][layer] * p["mem_k_gain"][layer], MEM_HEADS, scale * LOG2E),
                        wkv[:, MEM_W:].T)
        x3 = _memattn(x2.reshape(B, L, D), kn, vt, p["norm_mem"][layer], p["mem_wq"][layer].astype(BF16).T,
                      p["mem_wo"][layer].astype(BF16))
        x2 = _mlp(x3.reshape(M, D), p["norm_mlp"][layer], p["mlp_w1"][layer].astype(BF16),
                  p["mlp_w2"][layer].astype(BF16))
    return x2.reshape(B, L, D)


def _shared_tables(p, da_T):
    idx = np.arange(da_T)
    far = np.arange(T5_MAX_DIST, 1 << 20)
    assert da_T >= T5_MAX_DIST and (_t5_bucket_np(far) == _t5_bucket_np(far[:1])).all()
    assert (_t5_bucket_np(-far) == _t5_bucket_np(-far[:1])).all()
    rel_da = np.stack([d * da_T + idx[:, None] - idx[None, :] for d in (-1, 0, 1)])
    qi = np.arange(WG_BLOCK)
    ki = np.arange(3 * WG_BLOCK)
    rel_wg = ki[:, None] - WG_BLOCK - qi[None, :]
    mask_wg = np.where(np.abs(rel_wg) <= WG_WINDOW, 0.0, NEG_INF).astype(np.float32)
    return dict(da_T=da_T,
                da_bias=_t5bias(p["t5_table"], rel_da, scale=LOG2E),
                da_cfar=p["t5_table"][_t5_bucket_np(np.array([-T5_MAX_DIST, T5_MAX_DIST]))] * LOG2E,
                wg_bias=_t5bias(p["t5_table"], rel_wg, mask_wg, scale=LOG2E),
                na_bias=[_nabias(p["na_rpb"][j]) for j in range(p["na_rpb"].shape[0])])


def kernel(x_prompt, x_sample, mem_prompt, mem_sample, t5_table, norm_mix, norm_mem, norm_memkv, norm_mlp, w_in_even, da_q_gain, da_k_gain, da_lambda, da_sub_gain, hy_conv_w, hy_conv_b, hy_w1, hy_b1, hy_freq, hy_w2, hy_b2, hy_w3, hy_skip, w_out_even, w_in_odd, na_q_gain, na_k_gain, na_rpb, wg_q_gain, wg_k_gain, wg_sink, w_out_odd, mem_wq, mem_wkv, mem_wo, mem_q_gain, mem_k_gain, mlp_w1, mlp_w2):
    p = dict(t5_table=t5_table, norm_mix=norm_mix, norm_mem=norm_mem, norm_memkv=norm_memkv, norm_mlp=norm_mlp,
             w_in_even=w_in_even, da_q_gain=da_q_gain, da_k_gain=da_k_gain, da_lambda=da_lambda,
             da_sub_gain=da_sub_gain, hy_conv_w=hy_conv_w, hy_conv_b=hy_conv_b, hy_w1=hy_w1, hy_b1=hy_b1,
             hy_freq=hy_freq, hy_w2=hy_w2, hy_b2=hy_b2, hy_w3=hy_w3, hy_skip=hy_skip, w_out_even=w_out_even,
             w_in_odd=w_in_odd, na_q_gain=na_q_gain, na_k_gain=na_k_gain, na_rpb=na_rpb, wg_q_gain=wg_q_gain,
             wg_k_gain=wg_k_gain, wg_sink=wg_sink, w_out_odd=w_out_odd, mem_wq=mem_wq, mem_wkv=mem_wkv,
             mem_wo=mem_wo, mem_q_gain=mem_q_gain, mem_k_gain=mem_k_gain, mlp_w1=mlp_w1, mlp_w2=mlp_w2)
    shared = _shared_tables(p, da_T=512)
    return (_trunk(x_prompt, mem_prompt, p, shared), _trunk(x_sample, mem_sample, p, shared))
```

```python
import functools
import math

import numpy as np
import jax
import jax.numpy as jnp
from jax import lax
from jax.experimental import pallas as pl
from jax.experimental.pallas import tpu as pltpu

F32 = jnp.float32
BF16 = jnp.bfloat16

D_MODEL = 1024
DEPTH = 2
HEAD_DIM = 64
DA_HEADS = 4
DA_VDIM = 2 * HEAD_DIM
DA_QK = DA_HEADS * 2 * HEAD_DIM
DA_V = DA_HEADS * DA_VDIM
HY_CH = D_MODEL // 2
HY_ORDER = 2
HY_BANDS = 8
HY_HIDDEN = 64
HY_FAST_DECAY = 0.3
HY_SLOW_DECAY = 1.5
HY_TARGET = 1e-2
NA_HEADS = 8
GRID_W = 64
NA_ROWS = 8
NA_COLS = 16
NA_W = NA_HEADS * HEAD_DIM
WG_HEADS = 8
WG_KV_HEADS = 2
WG_WINDOW = 128
WG_BLOCK = 128
WG_Q = WG_HEADS * HEAD_DIM
WG_KV = WG_KV_HEADS * HEAD_DIM
T5_BUCKETS = 32
T5_MAX_DIST = 128
T5_HEADS = 8
MEM_HEADS = 4
MEM_W = MEM_HEADS * HEAD_DIM
D_FF = 4 * D_MODEL
EPS = 1e-6
NEG_INF = -1e30
LOG2E = 1.4426950408889634

DA_SUM_ROWS = 16
LANES = 128
FFT_LANES = 256
VMEM_LIMIT = 56 * 1024 * 1024


def _cparams(sem, vmem=None):
    return pltpu.CompilerParams(dimension_semantics=sem, vmem_limit_bytes=vmem or VMEM_LIMIT)


def _full(shape):
    n = len(shape)
    return pl.BlockSpec(shape, lambda *_: (0,) * n)


def _dot(a, b):
    return jnp.dot(a, b, preferred_element_type=F32)


def _dot_nt(a, b):
    return lax.dot_general(a, b, (((1,), (1,)), ((), ())), preferred_element_type=F32)


def _group_gmat():
    g = np.arange(LANES) // HEAD_DIM
    return jnp.asarray((g[:, None] == g[None, :]).astype(np.float32), dtype=BF16)


def _rms_rows(x, g):
    ms = jnp.mean(x * x, axis=-1, keepdims=True)
    return x * lax.rsqrt(ms + EPS) * g


def _group_rms(y, gmat, gain):
    outs = []
    for c in range(y.shape[1] // LANES):
        yc = y[:, c * LANES:(c + 1) * LANES]
        ss = _dot((yc * yc).astype(BF16), gmat)
        outs.append(yc * lax.rsqrt(ss * (1.0 / HEAD_DIM) + EPS) * gain[:, c * LANES:(c + 1) * LANES])
    return outs


def _inproj_odd_kernel(x_ref, g_ref, wk_ref, gain_ref, gmat_ref, wv_ref, wq_ref, k_ref, v_ref, qt_ref):
    hn = _rms_rows(x_ref[...], g_ref[...]).astype(BF16)
    yk = _dot(hn, wk_ref[...])
    for c, yc in enumerate(_group_rms(yk, gmat_ref[...], gain_ref[...])):
        k_ref[:, c * LANES:(c + 1) * LANES] = yc.astype(k_ref.dtype)
    v_ref[...] = _dot(hn, wv_ref[...]).astype(v_ref.dtype)
    yt = _dot_nt(wq_ref[...], hn)
    nq, tm = yt.shape
    q = yt.reshape(nq // HEAD_DIM, HEAD_DIM, tm)
    ms = jnp.mean(q * q, axis=1, keepdims=True)
    qt_ref[0] = (q * lax.rsqrt(ms + EPS)).reshape(nq, tm).astype(qt_ref.dtype)


def _inproj_odd(x2d, g, w_k, k_gain, w_v, w_q, batch, tm=512):
    M, D = x2d.shape
    L = M // batch
    nt = L // tm
    n_k, n_v, n_q = w_k.shape[1], w_v.shape[1], w_q.shape[0]
    return pl.pallas_call(
        _inproj_odd_kernel, name="inproj_odd", grid=(M // tm,),
        in_specs=[pl.BlockSpec((tm, D), lambda i: (i, 0)), _full((1, D)), _full((D, n_k)), _full((1, n_k)),
                  _full((LANES, LANES)), _full((D, n_v)), _full((n_q, D))],
        out_specs=[pl.BlockSpec((tm, n_k), lambda i: (i, 0)), pl.BlockSpec((tm, n_v), lambda i: (i, 0)),
                   pl.BlockSpec((1, n_q, tm), lambda i: (i // nt, 0, i % nt))],
        out_shape=[jax.ShapeDtypeStruct((M, n_k), BF16), jax.ShapeDtypeStruct((M, n_v), BF16),
                   jax.ShapeDtypeStruct((batch, n_q, L), BF16)],
        compiler_params=_cparams(("parallel",)))(
            x2d, g.reshape(1, D), w_k, k_gain.reshape(1, n_k), _group_gmat(), w_v, w_q)


def _inproj_even_kernel(x_ref, g_ref, wk_ref, gain_ref, gmat_ref, wcm_ref, k_ref, qt_ref, vt_ref, ut_ref):
    hn = _rms_rows(x_ref[...], g_ref[...]).astype(BF16)
    yk = _dot(hn, wk_ref[...])
    for c, yc in enumerate(_group_rms(yk, gmat_ref[...], gain_ref[...])):
        k_ref[:, c * LANES:(c + 1) * LANES] = yc.astype(k_ref.dtype)
    yt = _dot_nt(wcm_ref[...], hn)
    tm = yt.shape[1]
    q = yt[:DA_QK].reshape(DA_QK // HEAD_DIM, HEAD_DIM, tm)
    ms = jnp.mean(q * q, axis=1, keepdims=True)
    qt_ref[0, 0] = (q * lax.rsqrt(ms + EPS)).reshape(DA_QK, tm).astype(qt_ref.dtype)
    vt_ref[0, 0] = yt[DA_QK:DA_QK + DA_V].astype(vt_ref.dtype)
    ut_ref[0] = yt[DA_QK + DA_V:]


def _inproj_even(x2d, g, w_k, k_gain, w_cm, batch, tm):
    M, D = x2d.shape
    L = M // batch
    nt = L // tm
    n_u = w_cm.shape[0] - DA_QK - DA_V
    return pl.pallas_call(
        _inproj_even_kernel, name="inproj_even", grid=(M // tm,),
        in_specs=[pl.BlockSpec((tm, D), lambda i: (i, 0)), _full((1, D)), _full((D, DA_QK)), _full((1, DA_QK)),
                  _full((LANES, LANES)), _full(w_cm.shape)],
        out_specs=[pl.BlockSpec((tm, DA_QK), lambda i: (i, 0)),
                   pl.BlockSpec((1, 1, DA_QK, tm), lambda i: (i // nt, i % nt, 0, 0)),
                   pl.BlockSpec((1, 1, DA_V, tm), lambda i: (i // nt, i % nt, 0, 0)),
                   pl.BlockSpec((1, n_u, tm), lambda i: (i // nt, 0, i % nt))],
        out_shape=[jax.ShapeDtypeStruct((M, DA_QK), BF16),
                   jax.ShapeDtypeStruct((batch, nt, DA_QK, tm), BF16),
                   jax.ShapeDtypeStruct((batch, nt, DA_V, tm), BF16),
                   jax.ShapeDtypeStruct((batch, n_u, L), F32)],
        compiler_params=_cparams(("parallel",)))(
            x2d, g.reshape(1, D), w_k, k_gain.reshape(1, DA_QK), _group_gmat(), w_cm)


def _outproj_kernel(x_ref, a_ref, b_ref, wa_ref, wb_ref, o_ref, *, b_cm):
    b = b_ref[0].T.astype(BF16) if b_cm else b_ref[...]
    o_ref[...] = x_ref[...] + _dot(a_ref[...], wa_ref[...]) + _dot(b, wb_ref[...])


def _outproj(x2d, a, b, wa, wb, *, b_cm, batch=None, tm=512):
    M, D = x2d.shape
    if b_cm:
        L = M // batch
        nt = L // tm
        b_spec = pl.BlockSpec((1, b.shape[1], tm), lambda i: (i // nt, 0, i % nt))
    else:
        b_spec = pl.BlockSpec((tm, b.shape[1]), lambda i: (i, 0))
    return pl.pallas_call(
        functools.partial(_outproj_kernel, b_cm=b_cm), name="outproj",
        grid=(M // tm,),
        in_specs=[pl.BlockSpec((tm, D), lambda i: (i, 0)), pl.BlockSpec((tm, a.shape[1]), lambda i: (i, 0)),
                  b_spec, _full(wa.shape), _full(wb.shape)],
        out_specs=pl.BlockSpec((tm, D), lambda i: (i, 0)),
        out_shape=jax.ShapeDtypeStruct((M, D), F32),
        compiler_params=_cparams(("parallel",)))(x2d, a, b, wa, wb)


def _mlp_kernel(x_ref, g_ref, w1_ref, w2_ref, o_ref, hn_ref):
    j = pl.program_id(1)

    @pl.when(j == 0)
    def _():
        x = x_ref[...]
        hn_ref[...] = _rms_rows(x, g_ref[...]).astype(BF16)
        o_ref[...] = x

    h = _dot(hn_ref[...], w1_ref[...])
    h = jnp.square(jnp.maximum(h, 0.0))
    o_ref[...] += _dot(h.astype(BF16), w2_ref[...])


def _mlp(x2d, g, w1, w2, tm=1024, tf=1024):
    M, D = x2d.shape
    F = w1.shape[1]
    tm = min(tm, M)
    return pl.pallas_call(
        _mlp_kernel, name="mlp",
        grid=(M // tm, F // tf),
        in_specs=[pl.BlockSpec((tm, D), lambda i, j: (i, 0)), _full((1, D)),
                  pl.BlockSpec((D, tf), lambda i, j: (0, j)), pl.BlockSpec((tf, D), lambda i, j: (j, 0))],
        out_specs=pl.BlockSpec((tm, D), lambda i, j: (i, 0)),
        out_shape=jax.ShapeDtypeStruct((M, D), F32),
        scratch_shapes=[pltpu.VMEM((tm, D), BF16)],
        compiler_params=_cparams(("parallel", "arbitrary")))(x2d, g.reshape(1, D), w1, w2)


def _memkv_kernel(m_ref, g_ref, wk_ref, gain_ref, gmat_ref, wvt_ref, k_ref, vt_ref):
    mn = _rms_rows(m_ref[0], g_ref[...]).astype(BF16)
    kk = _dot(mn, wk_ref[...])
    for c, kc in enumerate(_group_rms(kk, gmat_ref[...], gain_ref[...])):
        k_ref[0, :, c * LANES:(c + 1) * LANES] = kc.astype(BF16)
    vt_ref[0] = _dot_nt(wvt_ref[...], mn).astype(BF16)


def _memkv(mem, g, wk, kgain, wvt):
    B, M, D = mem.shape
    return pl.pallas_call(
        _memkv_kernel, name="memkv", grid=(B,),
        in_specs=[pl.BlockSpec((1, M, D), lambda b: (b, 0, 0)), _full((1, D)), _full(wk.shape),
                  _full((1, MEM_W)), _full((LANES, LANES)), _full(wvt.shape)],
        out_specs=[pl.BlockSpec((1, M, MEM_W), lambda b: (b, 0, 0)), pl.BlockSpec((1, MEM_W, M), lambda b: (b, 0, 0))],
        out_shape=[jax.ShapeDtypeStruct((B, M, MEM_W), BF16), jax.ShapeDtypeStruct((B, MEM_W, M), BF16)],
        compiler_params=_cparams(("parallel",)))(
            mem, g.reshape(1, D), wk, kgain.reshape(1, MEM_W), _group_gmat(), wvt)


def _memattn_kernel(x_ref, g_ref, wqt_ref, k_ref, vt_ref, wo_ref, o_ref, ot_ref):
    x = x_ref[0]
    tm = x.shape[0]
    hn = _rms_rows(x, g_ref[...]).astype(BF16)
    q = _dot_nt(wqt_ref[...], hn).reshape(MEM_HEADS, HEAD_DIM, tm)
    ms = jnp.mean(q * q, axis=1, keepdims=True)
    qn = (q * lax.rsqrt(ms + EPS)).astype(BF16)
    k = k_ref[0]
    vt = vt_ref[0]
    n_mem = k.shape[0]
    ones = jnp.ones((16, n_mem), BF16)
    for hp in range(MEM_HEADS // 2):
        s = _dot(k[:, hp * LANES:(hp + 1) * LANES], _blockdiag_q(qn[2 * hp], qn[2 * hp + 1]))
        p = jnp.exp2((s - jnp.max(s, axis=0, keepdims=True)).astype(BF16))
        for j in range(2):
            rows = slice((2 * hp + j) * HEAD_DIM, (2 * hp + j + 1) * HEAD_DIM)
            o = _dot(jnp.concatenate([vt[rows], ones], axis=0), p[:, j * tm:(j + 1) * tm])
            ot_ref[rows, :] = o[:HEAD_DIM] / o[HEAD_DIM:HEAD_DIM + 1]
    o_ref[0] = x + _dot_tn(ot_ref[...].astype(BF16), wo_ref[...])


def _memattn(x, kn, vt, g, wqt, wo, tm=512):
    B, L, D = x.shape
    M = kn.shape[1]
    return pl.pallas_call(
        _memattn_kernel, name="memattn", grid=(B, L // tm),
        in_specs=[pl.BlockSpec((1, tm, D), lambda b, i: (b, i, 0)), _full((1, D)), _full(wqt.shape),
                  pl.BlockSpec((1, M, MEM_W), lambda b, i: (b, 0, 0)),
                  pl.BlockSpec((1, MEM_W, M), lambda b, i: (b, 0, 0)), _full(wo.shape)],
        out_specs=pl.BlockSpec((1, tm, D), lambda b, i: (b, i, 0)),
        out_shape=jax.ShapeDtypeStruct((B, L, D), F32),
        scratch_shapes=[pltpu.VMEM((MEM_W, tm), F32)],
        compiler_params=_cparams(("parallel", "parallel")))(x, g.reshape(1, D), wqt, kn, vt, wo)


def _t5_bucket_np(rel):
    half = T5_BUCKETS // 2
    exact = half // 2
    n = np.abs(rel)
    nf = np.maximum(n, 1).astype(np.float64)
    large = exact + (np.log(nf / exact) / math.log(T5_MAX_DIST / exact) * (half - exact)).astype(np.int32)
    large = np.minimum(large, half - 1)
    return (np.where(rel > 0, half, 0) + np.where(n < exact, n, large)).astype(np.int32)


def _t5bias_kernel(table_ref, bucket_ref, mask_ref, o_ref, *, scale):
    h = pl.program_id(0)
    bucket = bucket_ref[...]
    acc = jnp.zeros(bucket.shape, F32)
    for b in range(T5_BUCKETS):
        acc = jnp.where(bucket == b, table_ref[b, h], acc)
    o_ref[0] = acc * scale + mask_ref[...]


def _t5bias(table, rel, mask=None, scale=1.0):
    bucket = jnp.asarray(_t5_bucket_np(rel))
    mask = jnp.zeros(rel.shape, F32) if mask is None else jnp.asarray(mask, F32)
    nd = rel.ndim
    return pl.pallas_call(
        functools.partial(_t5bias_kernel, scale=scale), name="t5bias", grid=(T5_HEADS,),
        in_specs=[pl.BlockSpec(memory_space=pltpu.SMEM), _full(rel.shape), _full(rel.shape)],
        out_specs=pl.BlockSpec((1,) + rel.shape, lambda h: (h,) + (0,) * nd),
        out_shape=jax.ShapeDtypeStruct((T5_HEADS,) + rel.shape, F32),
        compiler_params=_cparams(("arbitrary",)))(table, bucket, mask)


def _diffattn_kernel(lam_ref, cfar_ref, q_ref, k_ref, v_ref, bias_ref, sg_ref, o_ref, qp_ref, sa_ref, sb_ref,
                     xa_ref, xb_ref, m_ref, acc_ref, *, lam_init, T, nkv):
    h = pl.program_id(1)
    i = pl.program_id(2)
    q = q_ref[0, 0]
    row = lax.broadcasted_iota(jnp.int32, q.shape, 0)
    zero = jnp.zeros(q.shape, q.dtype)
    qp_ref[0] = jnp.where(row < HEAD_DIM, q, zero)
    qp_ref[1] = jnp.where(row >= HEAD_DIM, q, zero)
    m_ref[...] = jnp.full(m_ref.shape, NEG_INF, F32)
    acc_ref[...] = jnp.zeros(acc_ref.shape, F32)

    def scores(jj, s_ref, x_ref):
        k = k_ref[0, pl.ds(pl.multiple_of(jj * T, T), T), :]
        for m in range(2):
            s = _dot(k, qp_ref[m])
            s_ref[m] = s
            x_ref[m] = jnp.max(s, axis=0, keepdims=True)

    def step(jj, s_cur, x_cur, s_nxt, x_nxt):
        d = jj - i
        near = jnp.abs(d) <= 1

        @pl.when(near)
        def _():
            for m in range(2):
                s = s_cur[m] + bias_ref[m, d + 1]
                s_cur[m] = s
                x_cur[m] = jnp.max(s, axis=0, keepdims=True)

        scores(jnp.minimum(jj + 1, nkv - 1), s_nxt, x_nxt)
        v = jnp.concatenate([v_ref[0, jj], jnp.ones((DA_SUM_ROWS, T), BF16)], axis=0)
        for m in range(2):
            c = jnp.where(near, 0.0, jnp.where(d < 0, cfar_ref[0, 2 * h + m], cfar_ref[1, 2 * h + m]))
            m_prev = m_ref[m]
            m_new = jnp.maximum(m_prev, x_cur[m] + c)
            alpha = jnp.exp2(m_prev - m_new)
            p = jnp.exp2((s_cur[m] - (m_new - c)).astype(BF16))
            acc_ref[m] = alpha * acc_ref[m] + _dot(v, p)
            m_ref[m] = m_new

    scores(0, sa_ref, xa_ref)

    def body(t, carry):
        step(2 * t, sa_ref, xa_ref, sb_ref, xb_ref)
        step(2 * t + 1, sb_ref, xb_ref, sa_ref, xa_ref)
        return carry

    lax.fori_loop(0, nkv // 2, body, 0)

    lf = lam_ref[...]
    lam = (jnp.exp(jnp.sum(lf[0:1] * lf[1:2], axis=-1, keepdims=True))
           - jnp.exp(jnp.sum(lf[2:3] * lf[3:4], axis=-1, keepdims=True)) + lam_init)
    num = [acc_ref[m, 0:DA_VDIM, :] for m in range(2)]
    den = [acc_ref[m, DA_VDIM:DA_VDIM + 1, :] for m in range(2)]
    o = num[0] / den[0] - lam * (num[1] / den[1])
    ms = jnp.mean(o * o, axis=0, keepdims=True)
    o = o * lax.rsqrt(ms + EPS) * sg_ref[...] * (1.0 - lam_init)
    o_ref[0] = o.T.astype(o_ref.dtype)


def _diffattn(qt, k, vt, bias, cfar, lam, sub_gain, lam_init):
    B, nb, _, T = qt.shape
    L = nb * T
    assert nb % 2 == 0
    return pl.pallas_call(
        functools.partial(_diffattn_kernel, lam_init=lam_init, T=T, nkv=nb), name="diffattn",
        grid=(B, DA_HEADS, nb),
        in_specs=[_full((4, HEAD_DIM)), pl.BlockSpec(memory_space=pltpu.SMEM),
                  pl.BlockSpec((1, 1, LANES, T), lambda b, h, i: (b, i, h, 0)),
                  pl.BlockSpec((1, L, LANES), lambda b, h, i: (b, 0, h)),
                  pl.BlockSpec((1, nb, LANES, T), lambda b, h, i: (b, 0, h, 0)),
                  pl.BlockSpec((2, 3, T, T), lambda b, h, i: (h, 0, 0, 0)),
                  _full((DA_VDIM, 1))],
        out_specs=pl.BlockSpec((1, T, LANES), lambda b, h, i: (b, i, h)),
        out_shape=jax.ShapeDtypeStruct((B, L, DA_V), BF16),
        scratch_shapes=[pltpu.VMEM((2, LANES, T), BF16), pltpu.VMEM((2, T, T), F32), pltpu.VMEM((2, T, T), F32),
                        pltpu.VMEM((2, 1, T), F32), pltpu.VMEM((2, 1, T), F32),
                        pltpu.VMEM((2, 1, T), F32), pltpu.VMEM((2, DA_VDIM + DA_SUM_ROWS, T), F32)],
        compiler_params=_cparams(("parallel", "parallel", "arbitrary")))(
            lam, cfar, qt, k, vt, bias, sub_gain.reshape(DA_VDIM, 1))


def _dot_tn(a, b):
    return lax.dot_general(a, b, (((0,), (0,)), ((), ())), preferred_element_type=F32)


def _blockdiag_q(qa, qb):
    z = jnp.zeros(qa.shape, qa.dtype)
    return jnp.concatenate([jnp.concatenate([qa, z], axis=0), jnp.concatenate([z, qb], axis=0)], axis=1)


def _wgqa_kernel(sink_ref, q_ref, kp_ref, kc_ref, kn_ref, vp_ref, vc_ref, vn_ref, bias_ref, o_ref, kw_ref, vw_ref):
    n = pl.program_id(1)
    nb = pl.num_programs(1)
    W = WG_BLOCK
    kw_ref[0:W] = kp_ref[0]
    kw_ref[W:2 * W] = kc_ref[0]
    kw_ref[2 * W:3 * W] = kn_ref[0]
    vw_ref[0:W] = vp_ref[0]
    vw_ref[W:2 * W] = vc_ref[0]
    vw_ref[2 * W:3 * W] = vn_ref[0]
    key = lax.broadcasted_iota(jnp.int32, (3 * W, 2 * W), 0)
    valid = jnp.logical_and(jnp.logical_or(n > 0, key >= W), jnp.logical_or(n < nb - 1, key < 2 * W))
    lane2 = lax.broadcasted_iota(jnp.int32, (1, 2 * W), 1)
    lane1 = lax.broadcasted_iota(jnp.int32, (W, LANES), 1)
    grp = WG_HEADS // WG_KV_HEADS
    kw = kw_ref[...]
    sink_rows = 16
    ones = jnp.ones((3 * W + sink_rows, LANES), BF16)
    vw = jnp.concatenate([jnp.concatenate([vw_ref[...], jnp.zeros((sink_rows, LANES), BF16)], axis=0), ones], axis=1)
    first = lax.broadcasted_iota(jnp.int32, (sink_rows, 2 * W), 0) == 0
    for h in range(grp):
        qa = q_ref[0, h * HEAD_DIM:(h + 1) * HEAD_DIM, :]
        qb = q_ref[0, (h + grp) * HEAD_DIM:(h + grp + 1) * HEAD_DIM, :]
        s = _dot(kw, _blockdiag_q(qa, qb))
        s = s + jnp.concatenate([bias_ref[h], bias_ref[h + grp]], axis=1)
        s = jnp.where(valid, s, NEG_INF)
        sk = jnp.where(lane2 < W, sink_ref[h], sink_ref[h + grp]) * LOG2E
        mx = jnp.maximum(jnp.max(s, axis=0, keepdims=True), sk)
        p = jnp.exp2((s - mx).astype(BF16))
        p_sink = jnp.where(first, jnp.exp2(sk - mx), 0.0).astype(BF16)
        o2 = _dot_tn(jnp.concatenate([p, p_sink], axis=0), vw)
        o2 = o2[:, :LANES] / o2[:, LANES:]
        o_ref[0, :, h * LANES:(h + 1) * LANES] = jnp.where(lane1 < HEAD_DIM, o2[:W], o2[W:]).astype(o_ref.dtype)


def _wgqa(qt, kk, vv, bias, sink, q_row, k_col, v_col):
    B, L, _ = kk.shape
    W = WG_BLOCK
    nb = L // W
    prev = lambda n: jnp.maximum(n - 1, 0)
    nxt = lambda n: jnp.minimum(n + 1, nb - 1)
    return pl.pallas_call(
        _wgqa_kernel, name="wgqa", grid=(B, nb),
        in_specs=[pl.BlockSpec(memory_space=pltpu.SMEM),
                  pl.BlockSpec((1, WG_Q, W), lambda b, n: (b, q_row, n)),
                  pl.BlockSpec((1, W, LANES), lambda b, n: (b, prev(n), k_col)),
                  pl.BlockSpec((1, W, LANES), lambda b, n: (b, n, k_col)),
                  pl.BlockSpec((1, W, LANES), lambda b, n: (b, nxt(n), k_col)),
                  pl.BlockSpec((1, W, LANES), lambda b, n: (b, prev(n), v_col)),
                  pl.BlockSpec((1, W, LANES), lambda b, n: (b, n, v_col)),
                  pl.BlockSpec((1, W, LANES), lambda b, n: (b, nxt(n), v_col)),
                  _full(bias.shape)],
        out_specs=pl.BlockSpec((1, W, WG_Q), lambda b, n: (b, n, 0)),
        out_shape=jax.ShapeDtypeStruct((B, L, WG_Q), BF16),
        scratch_shapes=[pltpu.VMEM((3 * W, LANES), BF16), pltpu.VMEM((3 * W, LANES), BF16)],
        compiler_params=_cparams(("parallel", "arbitrary")))(sink, qt, kk, kk, kk, vv, vv, vv, bias)


NA_KROWS = NA_ROWS + 1
NA_CASES = 5
_NA_CASE_GEOM = (((0, 0), 7), ((0, 0), 5), ((0, 1), 3), ((1, 1), 2), ((1, 1), 0))


def _nabias_kernel(rpb_ref, o_ref):
    h = pl.program_id(0)
    cc = lax.broadcasted_iota(jnp.int32, (GRID_W, LANES), 0)
    lane = lax.broadcasted_iota(jnp.int32, (GRID_W, LANES), 1)
    c = lane % GRID_W
    second = lane >= GRID_W
    c_start = jnp.clip(c - NA_COLS // 2, 0, GRID_W - NA_COLS)
    valid = jnp.logical_and(cc >= c_start, cc < c_start + NA_COLS)
    dc = cc - c + (NA_COLS - 1)
    neg = jnp.full((GRID_W, LANES), NEG_INF, F32)
    tiles = {}
    for dr in range(1, 2 * NA_ROWS - 1):
        acc = neg
        for d in range(2 * NA_COLS - 1):
            val = jnp.where(second, rpb_ref[h, dr - 1, d], rpb_ref[h, dr, d]) * LOG2E
            acc = jnp.where(jnp.logical_and(valid, dc == d), val, acc)
        tiles[dr] = acc
    for case, (wstart, a) in enumerate(_NA_CASE_GEOM):
        for i in range(NA_KROWS):
            in0 = wstart[0] <= i < wstart[0] + NA_ROWS
            in1 = wstart[1] <= i < wstart[1] + NA_ROWS
            t = tiles[i + a] if (in0 or in1) else neg
            if in1 and not in0:
                t = jnp.where(second, t, neg)
            if in0 and not in1:
                t = jnp.where(second, neg, t)
            o_ref[case, 0, i * GRID_W:(i + 1) * GRID_W, :] = t


def _nabias(rpb):
    return pl.pallas_call(
        _nabias_kernel, name="nabias", grid=(NA_HEADS,),
        in_specs=[pl.BlockSpec(memory_space=pltpu.SMEM)],
        out_specs=pl.BlockSpec((NA_CASES, 1, NA_KROWS * GRID_W, LANES), lambda h: (0, h, 0, 0)),
        out_shape=jax.ShapeDtypeStruct((NA_CASES, NA_HEADS, NA_KROWS * GRID_W, LANES), F32),
        compiler_params=_cparams(("arbitrary",)))(rpb)


def _natten_kernel(q_ref, kp_ref, kc_ref, kn_ref, vp_ref, vc_ref, vn_ref, bias_ref, o_ref, kw_ref, vw_ref,
                   sa_ref, sb_ref, *, rows):
    i = pl.program_id(1)
    RB = NA_ROWS
    T = RB * GRID_W
    KW = NA_KROWS * GRID_W
    kw_ref[0:T] = kp_ref[0]
    kw_ref[T:2 * T] = kc_ref[0]
    kw_ref[2 * T:3 * T] = kn_ref[0]
    vw_ref[0:T] = vp_ref[0]
    vw_ref[T:2 * T] = vc_ref[0]
    vw_ref[2 * T:3 * T] = vn_ref[0]
    lane1 = lax.broadcasted_iota(jnp.int32, (LANES, LANES), 1)
    ones = jnp.ones((KW, LANES), BF16)
    units = [(pi, hp) for pi in range(RB // 2) for hp in range(NA_HEADS // 2)]

    def window(pi):
        r = i * RB + 2 * pi
        r_lo = jnp.clip(r - NA_ROWS // 2, 0, rows - NA_KROWS)
        off = pl.multiple_of((r_lo - i * RB + RB) * GRID_W, GRID_W)
        case = jnp.where(r == 0, 0, jnp.where(r == 2, 1, jnp.where(r == rows - 4, 3, jnp.where(r == rows - 2, 4, 2))))
        return off, case

    def scores(unit, s_ref):
        pi, hp = unit
        off, case = window(pi)
        ha, hb = 2 * hp, 2 * hp + 1
        qs = slice(pi * LANES, (pi + 1) * LANES)
        kw = kw_ref[pl.ds(off, KW), hp * LANES:(hp + 1) * LANES]
        qa = q_ref[0, ha * HEAD_DIM:(ha + 1) * HEAD_DIM, qs]
        qb = q_ref[0, hb * HEAD_DIM:(hb + 1) * HEAD_DIM, qs]
        s = _dot(kw, _blockdiag_q(qa, qb))
        s_ref[...] = s + jnp.concatenate([bias_ref[case, ha], bias_ref[case, hb]], axis=1)

    def finish(unit, s_ref):
        pi, hp = unit
        off, _ = window(pi)
        ls = slice(hp * LANES, (hp + 1) * LANES)
        s = s_ref[...]
        p = jnp.exp2((s - jnp.max(s, axis=0, keepdims=True)).astype(BF16))
        vw = jnp.concatenate([vw_ref[pl.ds(off, KW), ls], ones], axis=1)
        o2 = _dot_tn(p, vw)
        o2 = o2[:, :LANES] / o2[:, LANES:]
        o_ref[0, pi * LANES:(pi + 1) * LANES, ls] = jnp.where(lane1 < HEAD_DIM, o2[:LANES], o2[LANES:]).astype(
            o_ref.dtype)

    slots = (sa_ref, sb_ref)
    scores(units[0], slots[0])
    for u, unit in enumerate(units):
        if u + 1 < len(units):
            scores(units[u + 1], slots[(u + 1) % 2])
        finish(unit, slots[u % 2])


def _natten(qt, kk, vv, bias):
    B, L, _ = kk.shape
    rows = L // GRID_W
    assert rows >= NA_KROWS + 1 and rows % NA_ROWS == 0
    T = NA_ROWS * GRID_W
    nb = L // T
    prev = lambda n: jnp.maximum(n - 1, 0)
    nxt = lambda n: jnp.minimum(n + 1, nb - 1)
    return pl.pallas_call(
        functools.partial(_natten_kernel, rows=rows), name="natten", grid=(B, nb),
        in_specs=[pl.BlockSpec((1, NA_W, T), lambda b, n: (b, 0, n)),
                  pl.BlockSpec((1, T, NA_W), lambda b, n: (b, prev(n), 0)),
                  pl.BlockSpec((1, T, NA_W), lambda b, n: (b, n, 0)),
                  pl.BlockSpec((1, T, NA_W), lambda b, n: (b, nxt(n), 0)),
                  pl.BlockSpec((1, T, NA_W), lambda b, n: (b, prev(n), 0)),
                  pl.BlockSpec((1, T, NA_W), lambda b, n: (b, n, 0)),
                  pl.BlockSpec((1, T, NA_W), lambda b, n: (b, nxt(n), 0)),
                  _full(bias.shape)],
        out_specs=pl.BlockSpec((1, T, NA_W), lambda b, n: (b, n, 0)),
        out_shape=jax.ShapeDtypeStruct((B, L, NA_W), BF16),
        scratch_shapes=[pltpu.VMEM((3 * T, NA_W), BF16), pltpu.VMEM((3 * T, NA_W), BF16)]
                       + [pltpu.VMEM((NA_KROWS * GRID_W, 2 * LANES), F32)] * 2,
        compiler_params=_cparams(("parallel", "arbitrary")))(qt, kk, kk, kk, vv, vv, vv, bias)


def _np_bf16(a):
    return np.asarray(a, np.float32).astype(BF16)


def _fft_consts(L):
    N2 = FFT_LANES
    N = 2 * L
    N1 = N // N2
    N1h = N1 // 2
    G = max(1, LANES // N1h)
    Gf = max(1, LANES // N1)
    k = np.arange(N1)
    ang1 = 2.0 * np.pi * np.outer(k, k) / N1
    C1, S1 = np.cos(ang1), np.sin(ang1)
    eye = np.eye
    KC, KS = np.kron(eye(G), C1[:, :N1h]), np.kron(eye(G), S1[:, :N1h])
    fa = np.block([[KC, KS], [-KS, KC]])
    KCh, KSh = np.kron(eye(G), C1[:N1h, :]), np.kron(eye(G), S1[:N1h, :])
    fg = np.block([[KCh, -KSh], [KSh, KCh]]) / N
    faf = np.concatenate([np.kron(eye(Gf), C1), -np.kron(eye(Gf), S1)], axis=0)
    n2 = np.arange(N2)
    angt = 2.0 * np.pi * np.outer(k, n2) / N
    twr, twi = np.cos(angt), -np.sin(angt)
    ang2 = 2.0 * np.pi * np.outer(n2, n2) / N2
    C2, S2 = np.cos(ang2), np.sin(ang2)
    m2f = np.block([[C2, -S2], [S2, C2]])
    m2i = np.block([[C2, S2], [-S2, C2]])
    return dict(N=N, N1=N1, N1h=N1h, N2=N2, G=G, Gf=Gf, fa=_np_bf16(fa), fg=_np_bf16(fg), faf=_np_bf16(faf),
                m2f=_np_bf16(m2f), m2i=_np_bf16(m2i),
                twr=np.asarray(twr, np.float32), twi=np.asarray(twi, np.float32))


def _hyfilt_kernel(w1t_ref, b1_ref, fr_ref, w2t_ref, b2_ref, w3t_ref, delta_ref, band_ref, o_ref, sum_ref, *, L, tn):
    s = pl.program_id(0)
    n = s * tn + lax.broadcasted_iota(jnp.int32, (1, tn), 1)
    t = jnp.where(n < L, n, 2 * L - n).astype(F32)
    t01 = t / float(max(L - 1, 1))
    w = (2.0 * math.pi) * t / float(L)
    ang = band_ref[...] * w
    cs, sn = jnp.cos(ang), -jnp.sin(ang)
    w1t = w1t_ref[...]
    h = w1t[:, 0:1] * t01
    for b in range(HY_BANDS):
        h = h + w1t[:, 1 + b:2 + b] * cs[b:b + 1] + w1t[:, 1 + HY_BANDS + b:2 + HY_BANDS + b] * sn[b:b + 1]
    fr = fr_ref[...]
    h = jnp.sin(fr[:, 0:1] * (h + b1_ref[...]))
    h = jnp.sin(fr[:, 1:2] * (jnp.dot(w2t_ref[...], h, preferred_element_type=F32,
                                      precision=lax.Precision.HIGHEST) + b2_ref[...]))
    y = jnp.dot(w3t_ref[0], h, preferred_element_type=F32, precision=lax.Precision.HIGHEST)
    y = y * jnp.exp(-delta_ref[...] * t01)
    y = jnp.where(n == L, 0.0, y)
    o_ref[...] = y
    a = jnp.abs(y)
    part = a[:, 0:LANES]
    for c in range(1, tn // LANES):
        part = part + a[:, c * LANES:(c + 1) * LANES]

    @pl.when(s == 0)
    def _():
        sum_ref[...] = jnp.zeros(sum_ref.shape, F32)

    sum_ref[...] += part


def _hyena_filters_td(L, w1, b1, freq, w2, b2, w3, tn=1024):
    OC = HY_ORDER * HY_CH
    w3t = jnp.transpose(w3.reshape(HY_HIDDEN, HY_ORDER, 2, HY_CH), (2, 1, 3, 0)).reshape(2, OC, HY_HIDDEN)
    min_decay = math.log(HY_TARGET) / HY_SLOW_DECAY
    max_decay = math.log(HY_TARGET) / HY_FAST_DECAY
    deltas = np.abs(np.linspace(min_decay, max_decay, HY_CH, dtype=np.float32))
    delta = jnp.asarray(np.tile(deltas, HY_ORDER).reshape(OC, 1))
    bands = jnp.asarray(np.linspace(1e-4, HY_BANDS - 1, HY_BANDS, dtype=np.float32).reshape(HY_BANDS, 1))
    N = 2 * L
    nh = L // tn
    return pl.pallas_call(
        functools.partial(_hyfilt_kernel, L=L, tn=tn), name="hyfilt", grid=(N // tn,),
        in_specs=[_full((HY_HIDDEN, 1 + 2 * HY_BANDS)), _full((HY_HIDDEN, 1)), _full((HY_HIDDEN, 2)),
                  _full((HY_HIDDEN, HY_HIDDEN)), _full((HY_HIDDEN, 1)),
                  pl.BlockSpec((1, OC, HY_HIDDEN), lambda s: (s // nh, 0, 0)),
                  _full((OC, 1)), _full((HY_BANDS, 1))],
        out_specs=[pl.BlockSpec((OC, tn), lambda s: (0, s)), _full((OC, LANES))],
        out_shape=[jax.ShapeDtypeStruct((OC, N), F32), jax.ShapeDtypeStruct((OC, LANES), F32)],
        compiler_params=_cparams(("arbitrary",)))(
            w1.T, b1.reshape(HY_HIDDEN, 1), freq.T, w2.T, b2.reshape(HY_HIDDEN, 1), w3t, delta, bands)


def _hyspec_kernel(k_ref, sum_ref, fa_ref, twr_ref, twi_ref, m2_ref, kr_ref, ki_ref, p_ref, *, R, N1, N2, Gf):
    tot = jnp.sum(sum_ref[...], axis=-1, keepdims=True)
    mi = Gf * N1
    for g in range(R // Gf):
        xs = []
        for r in range(Gf):
            row = g * Gf + r
            xs.append(k_ref[row] / tot[row:row + 1, :])
        x = xs[0] if Gf == 1 else jnp.concatenate(xs, axis=0)
        a = _dot(fa_ref[...], x.astype(BF16))
        ar, ai = a[:mi], a[mi:]
        twr, twi = twr_ref[...], twi_ref[...]
        p_ref[g * mi:(g + 1) * mi, 0:N2] = (ar * twr - ai * twi).astype(BF16)
        p_ref[g * mi:(g + 1) * mi, N2:2 * N2] = (ar * twi + ai * twr).astype(BF16)
    b = _dot(p_ref[...], m2_ref[...])
    kr_ref[...] = b[:, :N2].reshape(R, N1, N2)
    ki_ref[...] = b[:, N2:].reshape(R, N1, N2)


def _hyena_spectra(ktd, ksum, fc):
    OC = ktd.shape[0]
    N1, N2, Gf = fc["N1"], fc["N2"], fc["Gf"]
    R = 1024 // N1
    twr = jnp.asarray(np.tile(fc["twr"], (Gf, 1)))
    twi = jnp.asarray(np.tile(fc["twi"], (Gf, 1)))
    fa, m2 = fc["faf"], fc["m2f"]
    return pl.pallas_call(
        functools.partial(_hyspec_kernel, R=R, N1=N1, N2=N2, Gf=Gf), name="hyspec", grid=(OC // R,),
        in_specs=[pl.BlockSpec((R, N1, N2), lambda i: (i, 0, 0)), pl.BlockSpec((R, LANES), lambda i: (i, 0)),
                  _full(fa.shape), _full(twr.shape), _full(twi.shape), _full(m2.shape)],
        out_specs=[pl.BlockSpec((R, N1, N2), lambda i: (i, 0, 0))] * 2,
        out_shape=[jax.ShapeDtypeStruct((OC, N1, N2), F32)] * 2,
        scratch_shapes=[pltpu.VMEM((R * N1, 2 * N2), BF16)],
        compiler_params=_cparams(("parallel",)))(ktd.reshape(OC, N1, N2), ksum, fa, twr, twi, m2)


def _hyconv_kernel(cw_ref, cb_ref, sk_ref, u_ref, kr_ref, ki_ref, fa_ref, fg_ref, twr_ref, twi_ref, mf_ref, mi_ref,
                   o_ref, z_ref, x1_ref, x2_ref, c_ref, p_ref, wcol_ref, *, R, N1, N2, G):
    N1h = N1 // 2
    cb = pl.program_id(0)
    ng = R // G
    mi = G * N1h
    mo = G * N1
    n_parts = 3
    i_bias = 3 * n_parts
    i_skip = i_bias + n_parts

    @pl.when(pl.program_id(1) == 0)
    def _():
        def fill(r, carry):
            ch = cb * R + r
            rs = pl.ds(pl.multiple_of(r * N1h, 8), N1h)
            for part in range(n_parts):
                c = part * HY_CH + ch
                for tap in range(3):
                    wcol_ref[tap * n_parts + part, rs, :] = jnp.full((N1h, LANES), cw_ref[tap, c], F32)
                wcol_ref[i_bias + part, rs, :] = jnp.full((N1h, LANES), cb_ref[c], F32)
            for o in range(HY_ORDER):
                wcol_ref[i_skip + o, rs, :] = jnp.full((N1h, LANES), sk_ref[o, ch], F32)
            return carry
        lax.fori_loop(0, R, fill, 0)

    def wide(idx, g):
        w = wcol_ref[idx, g * mi:(g + 1) * mi, :]
        return jnp.concatenate([w] * (N2 // LANES), axis=1)

    lane = lax.broadcasted_iota(jnp.int32, (mi, N2), 1)
    sub = lax.broadcasted_iota(jnp.int32, (mi, N2), 0) % N1h
    first = jnp.logical_and(lane == 0, sub == 0)
    last = jnp.logical_and(lane == N2 - 1, sub == N1h - 1)

    def shortconv(u, part, g):
        rl = pltpu.roll(u, 1, 1)
        prev = jnp.where(lane == 0, pltpu.roll(rl, 1, 0), rl)
        prev = jnp.where(first, 0.0, prev)
        rr = pltpu.roll(u, N2 - 1, 1)
        nxt = jnp.where(lane == N2 - 1, pltpu.roll(rr, mi - 1, 0), rr)
        nxt = jnp.where(last, 0.0, nxt)
        return (prev * wide(part, g) + u * wide(n_parts + part, g) + nxt * wide(2 * n_parts + part, g)
                + wide(i_bias + part, g))

    for g in range(ng):
        for pbatch in range(2):
            rs = slice((g * 2 + pbatch) * mi, (g * 2 + pbatch + 1) * mi)
            for part, dst in enumerate((z_ref, x1_ref, x2_ref)):
                u = u_ref[0, pbatch, part, g * G:(g + 1) * G].reshape(mi, N2)
                dst[rs, :] = shortconv(u, part, g)

    def conv(order):
        twr, twi = twr_ref[...], twi_ref[...]
        for g in range(ng):
            a = _dot(fa_ref[...], z_ref[g * 2 * mi:(g + 1) * 2 * mi, :].astype(BF16))
            ar, ai = a[:mo], a[mo:]
            p_ref[g * mo:(g + 1) * mo, 0:N2] = (ar * twr - ai * twi).astype(BF16)
            p_ref[g * mo:(g + 1) * mo, N2:2 * N2] = (ar * twi + ai * twr).astype(BF16)
        b = _dot(p_ref[...], mf_ref[...])
        br, bi = b[:, :N2], b[:, N2:]
        kr = kr_ref[order].reshape(R * N1, N2)
        ki = ki_ref[order].reshape(R * N1, N2)
        p_ref[:, 0:N2] = (br * kr - bi * ki).astype(BF16)
        p_ref[:, N2:2 * N2] = (br * ki + bi * kr).astype(BF16)
        d = _dot(p_ref[...], mi_ref[...])
        dr, di = d[:, :N2], d[:, N2:]
        for g in range(ng):
            sl = slice(g * mo, (g + 1) * mo)
            x = jnp.concatenate([dr[sl] * twr + di[sl] * twi, di[sl] * twr - dr[sl] * twi], axis=0)
            c_ref[g * 2 * mi:(g + 1) * 2 * mi, :] = _dot(fg_ref[...], x.astype(BF16))

    def gate(order, x_ref, final):
        for g in range(ng):
            skv = wide(i_skip + order, g)
            for pbatch in range(2):
                rs = slice((g * 2 + pbatch) * mi, (g * 2 + pbatch + 1) * mi)
                zn = x_ref[rs, :] * (c_ref[rs, :] + skv * z_ref[rs, :])
                if final:
                    o_ref[0, pbatch, g * G:(g + 1) * G] = zn.reshape(G, N1h, N2)
                else:
                    z_ref[rs, :] = zn

    conv(0)
    gate(0, x1_ref, False)
    conv(1)
    gate(1, x2_ref, True)


def _hyena_conv(ucm, conv_w, conv_b, skip, kr, ki, fc):
    B = ucm.shape[0]
    C = HY_CH
    N1, N2, G = fc["N1"], fc["N2"], fc["G"]
    N1h = N1 // 2
    R = 1024 // N1
    u = ucm.reshape(B // 2, 2, 3, C, N1h, N2)
    twr = jnp.asarray(np.tile(fc["twr"], (G, 1)))
    twi = jnp.asarray(np.tile(fc["twi"], (G, 1)))
    consts = [fc["fa"], fc["fg"], twr, twi, fc["m2f"], fc["m2i"]]
    rows = R * N1h
    smem = pl.BlockSpec(memory_space=pltpu.SMEM)
    out = pl.pallas_call(
        functools.partial(_hyconv_kernel, R=R, N1=N1, N2=N2, G=G), name="hyconv",
        grid=(C // R, B // 2),
        in_specs=[smem, smem, smem,
                  pl.BlockSpec((1, 2, 3, R, N1h, N2), lambda c, p: (p, 0, 0, c, 0, 0)),
                  pl.BlockSpec((HY_ORDER, R, N1, N2), lambda c, p: (0, c, 0, 0)),
                  pl.BlockSpec((HY_ORDER, R, N1, N2), lambda c, p: (0, c, 0, 0))]
                 + [_full(a.shape) for a in consts],
        out_specs=pl.BlockSpec((1, 2, R, N1h, N2), lambda c, p: (p, 0, c, 0, 0)),
        out_shape=jax.ShapeDtypeStruct((B // 2, 2, C, N1h, N2), F32),
        scratch_shapes=[pltpu.VMEM((2 * rows, N2), F32)] * 4
                       + [pltpu.VMEM((R * N1, 2 * N2), BF16), pltpu.VMEM((4 * 3 + HY_ORDER, rows, LANES), F32)],
        compiler_params=_cparams(("parallel", "arbitrary")))(
            conv_w, conv_b, skip, u, kr.reshape(HY_ORDER, C, N1, N2), ki.reshape(HY_ORDER, C, N1, N2), *consts)
    return out.reshape(B, C, N1h * N2)


def _lambda_init(layer):
    return 0.8 - 0.6 * math.exp(-0.3 * layer)


def _tile_gain(g, reps, scale=1.0):
    return jnp.tile(g.astype(F32), reps) * scale


def _trunk(x, mem, p, shared):
    B, L, D = x.shape
    M = B * L
    scale = HEAD_DIM ** -0.5
    x2 = x.reshape(M, D)
    for layer in range(DEPTH):
        j = layer // 2
        if layer % 2 == 0:
            w_in = p["w_in_even"][j].astype(BF16)
            k_gain = _tile_gain(p["da_q_gain"][j] * p["da_k_gain"][j], DA_QK // HEAD_DIM, scale * LOG2E)
            w_cm = jnp.concatenate([w_in[:, :DA_QK], w_in[:, 2 * DA_QK:]], axis=1).T
            kk, qt, vt, ucm = _inproj_even(x2, p["norm_mix"][layer], w_in[:, DA_QK:2 * DA_QK], k_gain, w_cm,
                                           batch=B, tm=shared["da_T"])
            oa = _diffattn(qt, kk.reshape(B, L, DA_QK), vt, shared["da_bias"], shared["da_cfar"],
                           p["da_lambda"][j], p["da_sub_gain"][j], _lambda_init(layer))
            fc = _fft_consts(L)
            ktd, ksum = _hyena_filters_td(L, p["hy_w1"][j], p["hy_b1"][j], p["hy_freq"][j], p["hy_w2"][j],
                                          p["hy_b2"][j], p["hy_w3"][j])
            kr, ki = _hyena_spectra(ktd, ksum, fc)
            ob = _hyena_conv(ucm, p["hy_conv_w"][j], p["hy_conv_b"][j], p["hy_skip"][j], kr, ki, fc)
            w_out = p["w_out_even"][j].astype(BF16)
            x2 = _outproj(x2, oa.reshape(M, DA_V), ob, w_out[:DA_V], w_out[DA_V:], b_cm=True, batch=B)
        else:
            w_in = p["w_in_odd"][j].astype(BF16)
            c0 = 3 * NA_W
            c1 = c0 + WG_Q
            c2 = c1 + WG_KV
            w_k = jnp.concatenate([w_in[:, NA_W:2 * NA_W], w_in[:, c1:c2]], axis=1)
            w_v = jnp.concatenate([w_in[:, 2 * NA_W:c0], w_in[:, c2:]], axis=1)
            w_q = jnp.concatenate([w_in[:, :NA_W], w_in[:, c0:c1]], axis=1).T
            k_gain = jnp.concatenate([
                _tile_gain(p["na_q_gain"][j] * p["na_k_gain"][j], NA_HEADS, scale * LOG2E),
                _tile_gain(p["wg_q_gain"][j] * p["wg_k_gain"][j], WG_KV_HEADS, scale * LOG2E)])
            kk, vv, qt = _inproj_odd(x2, p["norm_mix"][layer], w_k, k_gain, w_v, w_q, batch=B)
            kk = kk.reshape(B, L, -1)
            vv = vv.reshape(B, L, -1)
            oc = _natten(qt, kk, vv, shared["na_bias"][j])
            od = _wgqa(qt, kk, vv, shared["wg_bias"], p["wg_sink"][j], q_row=NA_W // WG_Q,
                       k_col=NA_W // LANES, v_col=NA_W // LANES)
            w_out = p["w_out_odd"][j].astype(BF16)
            grp = WG_HEADS // WG_KV_HEADS
            head_order = np.stack([np.arange(grp), np.arange(grp) + grp], axis=1).reshape(-1)
            wd_rows = NA_W + (head_order[:, None] * HEAD_DIM + np.arange(HEAD_DIM)[None, :]).reshape(-1)
            x2 = _outproj(x2, oc.reshape(M, NA_W), od.reshape(M, WG_Q), w_out[:NA_W], w_out[wd_rows], b_cm=False)
        wkv = p["mem_wkv"][layer].astype(BF16)
        kn, vt = _memkv(mem, p["norm_memkv"][layer], wkv[:, :MEM_W],
                        _tile_gain(p["mem_q_gain"][layer] * p["mem_k_gain"][layer], MEM_HEADS, scale * LOG2E),
                        wkv[:, MEM_W:].T)
        x3 = _memattn(x2.reshape(B, L, D), kn, vt, p["norm_mem"][layer], p["mem_wq"][layer].astype(BF16).T,
                      p["mem_wo"][layer].astype(BF16))
        x2 = _mlp(x3.reshape(M, D), p["norm_mlp"][layer], p["mlp_w1"][layer].astype(BF16),
                  p["mlp_w2"][layer].astype(BF16))
    return x2.reshape(B, L, D)


def _shared_tables(p, da_T):
    idx = np.arange(da_T)
    far = np.arange(T5_MAX_DIST, 1 << 20)
    assert da_T >= T5_MAX_DIST and (_t5_bucket_np(far) == _t5_bucket_np(far[:1])).all()
    assert (_t5_bucket_np(-far) == _t5_bucket_np(-far[:1])).all()
    rel_da = np.stack([d * da_T + idx[:, None] - idx[None, :] for d in (-1, 0, 1)])
    qi = np.arange(WG_BLOCK)
    ki = np.arange(3 * WG_BLOCK)
    rel_wg = ki[:, None] - WG_BLOCK - qi[None, :]
    mask_wg = np.where(np.abs(rel_wg) <= WG_WINDOW, 0.0, NEG_INF).astype(np.float32)
    return dict(da_T=da_T,
                da_bias=_t5bias(p["t5_table"], rel_da, scale=LOG2E),
                da_cfar=p["t5_table"][_t5_bucket_np(np.array([-T5_MAX_DIST, T5_MAX_DIST]))] * LOG2E,
                wg_bias=_t5bias(p["t5_table"], rel_wg, mask_wg, scale=LOG2E),
                na_bias=[_nabias(p["na_rpb"][j]) for j in range(p["na_rpb"].shape[0])])


def kernel(x_prompt, x_sample, mem_prompt, mem_sample, t5_table, norm_mix, norm_mem, norm_memkv, norm_mlp, w_in_even, da_q_gain, da_k_gain, da_lambda, da_sub_gain, hy_conv_w, hy_conv_b, hy_w1, hy_b1, hy_freq, hy_w2, hy_b2, hy_w3, hy_skip, w_out_even, w_in_odd, na_q_gain, na_k_gain, na_rpb, wg_q_gain, wg_k_gain, wg_sink, w_out_odd, mem_wq, mem_wkv, mem_wo, mem_q_gain, mem_k_gain, mlp_w1, mlp_w2):
    p = dict(t5_table=t5_table, norm_mix=norm_mix, norm_mem=norm_mem, norm_memkv=norm_memkv, norm_mlp=norm_mlp,
             w_in_even=w_in_even, da_q_gain=da_q_gain, da_k_gain=da_k_gain, da_lambda=da_lambda,
             da_sub_gain=da_sub_gain, hy_conv_w=hy_conv_w, hy_conv_b=hy_conv_b, hy_w1=hy_w1, hy_b1=hy_b1,
             hy_freq=hy_freq, hy_w2=hy_w2, hy_b2=hy_b2, hy_w3=hy_w3, hy_skip=hy_skip, w_out_even=w_out_even,
             w_in_odd=w_in_odd, na_q_gain=na_q_gain, na_k_gain=na_k_gain, na_rpb=na_rpb, wg_q_gain=wg_q_gain,
             wg_k_gain=wg_k_gain, wg_sink=wg_sink, w_out_odd=w_out_odd, mem_wq=mem_wq, mem_wkv=mem_wkv,
             mem_wo=mem_wo, mem_q_gain=mem_q_gain, mem_k_gain=mem_k_gain, mlp_w1=mlp_w1, mlp_w2=mlp_w2)
    shared = _shared_tables(p, da_T=512)
    return (_trunk(x_prompt, mem_prompt, p, shared), _trunk(x_sample, mem_sample, p, shared))
```

```python
import functools
import math

import numpy as np
import jax
import jax.numpy as jnp
from jax import lax
from jax.experimental import pallas as pl
from jax.experimental.pallas import tpu as pltpu

F32 = jnp.float32
BF16 = jnp.bfloat16

D_MODEL = 1024
DEPTH = 2
HEAD_DIM = 64
DA_HEADS = 4
DA_VDIM = 2 * HEAD_DIM
DA_QK = DA_HEADS * 2 * HEAD_DIM
DA_V = DA_HEADS * DA_VDIM
HY_CH = D_MODEL // 2
HY_ORDER = 2
HY_BANDS = 8
HY_HIDDEN = 64
HY_FAST_DECAY = 0.3
HY_SLOW_DECAY = 1.5
HY_TARGET = 1e-2
NA_HEADS = 8
GRID_W = 64
NA_ROWS = 8
NA_COLS = 16
NA_W = NA_HEADS * HEAD_DIM
WG_HEADS = 8
WG_KV_HEADS = 2
WG_WINDOW = 128
WG_BLOCK = 128
WG_Q = WG_HEADS * HEAD_DIM
WG_KV = WG_KV_HEADS * HEAD_DIM
T5_BUCKETS = 32
T5_MAX_DIST = 128
T5_HEADS = 8
MEM_HEADS = 4
MEM_W = MEM_HEADS * HEAD_DIM
D_FF = 4 * D_MODEL
EPS = 1e-6
NEG_INF = -1e30
LOG2E = 1.4426950408889634

DA_SUM_ROWS = 16
LANES = 128
FFT_LANES = 256
VMEM_LIMIT = 56 * 1024 * 1024


def _cparams(sem, vmem=None):
    return pltpu.CompilerParams(dimension_semantics=sem, vmem_limit_bytes=vmem or VMEM_LIMIT)


def _full(shape):
    n = len(shape)
    return pl.BlockSpec(shape, lambda *_: (0,) * n)


def _dot(a, b):
    return jnp.dot(a, b, preferred_element_type=F32)


def _dot_nt(a, b):
    return lax.dot_general(a, b, (((1,), (1,)), ((), ())), preferred_element_type=F32)


def _group_gmat():
    g = np.arange(LANES) // HEAD_DIM
    return jnp.asarray((g[:, None] == g[None, :]).astype(np.float32), dtype=BF16)


def _rms_rows(x, g):
    ms = jnp.mean(x * x, axis=-1, keepdims=True)
    return x * lax.rsqrt(ms + EPS) * g


def _group_rms(y, gmat, gain):
    outs = []
    for c in range(y.shape[1] // LANES):
        yc = y[:, c * LANES:(c + 1) * LANES]
        ss = _dot((yc * yc).astype(BF16), gmat)
        outs.append(yc * lax.rsqrt(ss * (1.0 / HEAD_DIM) + EPS) * gain[:, c * LANES:(c + 1) * LANES])
    return outs


def _inproj_odd_kernel(x_ref, g_ref, wk_ref, gain_ref, gmat_ref, wv_ref, wq_ref, k_ref, v_ref, qt_ref):
    hn = _rms_rows(x_ref[...], g_ref[...]).astype(BF16)
    yk = _dot(hn, wk_ref[...])
    for c, yc in enumerate(_group_rms(yk, gmat_ref[...], gain_ref[...])):
        k_ref[:, c * LANES:(c + 1) * LANES] = yc.astype(k_ref.dtype)
    v_ref[...] = _dot(hn, wv_ref[...]).astype(v_ref.dtype)
    yt = _dot_nt(wq_ref[...], hn)
    nq, tm = yt.shape
    q = yt.reshape(nq // HEAD_DIM, HEAD_DIM, tm)
    ms = jnp.mean(q * q, axis=1, keepdims=True)
    qt_ref[0] = (q * lax.rsqrt(ms + EPS)).reshape(nq, tm).astype(qt_ref.dtype)


def _inproj_odd(x2d, g, w_k, k_gain, w_v, w_q, batch, tm=512):
    M, D = x2d.shape
    L = M // batch
    nt = L // tm
    n_k, n_v, n_q = w_k.shape[1], w_v.shape[1], w_q.shape[0]
    return pl.pallas_call(
        _inproj_odd_kernel, name="inproj_odd", grid=(M // tm,),
        in_specs=[pl.BlockSpec((tm, D), lambda i: (i, 0)), _full((1, D)), _full((D, n_k)), _full((1, n_k)),
                  _full((LANES, LANES)), _full((D, n_v)), _full((n_q, D))],
        out_specs=[pl.BlockSpec((tm, n_k), lambda i: (i, 0)), pl.BlockSpec((tm, n_v), lambda i: (i, 0)),
                   pl.BlockSpec((1, n_q, tm), lambda i: (i // nt, 0, i % nt))],
        out_shape=[jax.ShapeDtypeStruct((M, n_k), BF16), jax.ShapeDtypeStruct((M, n_v), BF16),
                   jax.ShapeDtypeStruct((batch, n_q, L), BF16)],
        compiler_params=_cparams(("parallel",)))(
            x2d, g.reshape(1, D), w_k, k_gain.reshape(1, n_k), _group_gmat(), w_v, w_q)


def _inproj_even_kernel(x_ref, g_ref, wk_ref, gain_ref, gmat_ref, wcm_ref, k_ref, qt_ref, vt_ref, ut_ref):
    hn = _rms_rows(x_ref[...], g_ref[...]).astype(BF16)
    yk = _dot(hn, wk_ref[...])
    for c, yc in enumerate(_group_rms(yk, gmat_ref[...], gain_ref[...])):
        k_ref[:, c * LANES:(c + 1) * LANES] = yc.astype(k_ref.dtype)
    yt = _dot_nt(wcm_ref[...], hn)
    tm = yt.shape[1]
    q = yt[:DA_QK].reshape(DA_QK // HEAD_DIM, HEAD_DIM, tm)
    ms = jnp.mean(q * q, axis=1, keepdims=True)
    qt_ref[0, 0] = (q * lax.rsqrt(ms + EPS)).reshape(DA_QK, tm).astype(qt_ref.dtype)
    vt_ref[0, 0] = yt[DA_QK:DA_QK + DA_V].astype(vt_ref.dtype)
    ut_ref[0] = yt[DA_QK + DA_V:]


def _inproj_even(x2d, g, w_k, k_gain, w_cm, batch, tm):
    M, D = x2d.shape
    L = M // batch
    nt = L // tm
    n_u = w_cm.shape[0] - DA_QK - DA_V
    return pl.pallas_call(
        _inproj_even_kernel, name="inproj_even", grid=(M // tm,),
        in_specs=[pl.BlockSpec((tm, D), lambda i: (i, 0)), _full((1, D)), _full((D, DA_QK)), _full((1, DA_QK)),
                  _full((LANES, LANES)), _full(w_cm.shape)],
        out_specs=[pl.BlockSpec((tm, DA_QK), lambda i: (i, 0)),
                   pl.BlockSpec((1, 1, DA_QK, tm), lambda i: (i // nt, i % nt, 0, 0)),
                   pl.BlockSpec((1, 1, DA_V, tm), lambda i: (i // nt, i % nt, 0, 0)),
                   pl.BlockSpec((1, n_u, tm), lambda i: (i // nt, 0, i % nt))],
        out_shape=[jax.ShapeDtypeStruct((M, DA_QK), BF16),
                   jax.ShapeDtypeStruct((batch, nt, DA_QK, tm), BF16),
                   jax.ShapeDtypeStruct((batch, nt, DA_V, tm), BF16),
                   jax.ShapeDtypeStruct((batch, n_u, L), F32)],
        compiler_params=_cparams(("parallel",)))(
            x2d, g.reshape(1, D), w_k, k_gain.reshape(1, DA_QK), _group_gmat(), w_cm)


def _outproj_kernel(x_ref, a_ref, b_ref, wa_ref, wb_ref, o_ref, *, b_cm):
    b = b_ref[0].T.astype(BF16) if b_cm else b_ref[...]
    o_ref[...] = x_ref[...] + _dot(a_ref[...], wa_ref[...]) + _dot(b, wb_ref[...])


def _outproj(x2d, a, b, wa, wb, *, b_cm, batch=None, tm=512):
    M, D = x2d.shape
    if b_cm:
        L = M // batch
        nt = L // tm
        b_spec = pl.BlockSpec((1, b.shape[1], tm), lambda i: (i // nt, 0, i % nt))
    else:
        b_spec = pl.BlockSpec((tm, b.shape[1]), lambda i: (i, 0))
    return pl.pallas_call(
        functools.partial(_outproj_kernel, b_cm=b_cm), name="outproj",
        grid=(M // tm,),
        in_specs=[pl.BlockSpec((tm, D), lambda i: (i, 0)), pl.BlockSpec((tm, a.shape[1]), lambda i: (i, 0)),
                  b_spec, _full(wa.shape), _full(wb.shape)],
        out_specs=pl.BlockSpec((tm, D), lambda i: (i, 0)),
        out_shape=jax.ShapeDtypeStruct((M, D), F32),
        compiler_params=_cparams(("parallel",)))(x2d, a, b, wa, wb)


def _mlp_kernel(x_ref, g_ref, w1_ref, w2_ref, o_ref, hn_ref):
    j = pl.program_id(1)

    @pl.when(j == 0)
    def _():
        x = x_ref[...]
        hn_ref[...] = _rms_rows(x, g_ref[...]).astype(BF16)
        o_ref[...] = x

    h = _dot(hn_ref[...], w1_ref[...])
    h = jnp.square(jnp.maximum(h, 0.0))
    o_ref[...] += _dot(h.astype(BF16), w2_ref[...])


def _mlp(x2d, g, w1, w2, tm=1024, tf=1024):
    M, D = x2d.shape
    F = w1.shape[1]
    tm = min(tm, M)
    return pl.pallas_call(
        _mlp_kernel, name="mlp",
        grid=(M // tm, F // tf),
        in_specs=[pl.BlockSpec((tm, D), lambda i, j: (i, 0)), _full((1, D)),
                  pl.BlockSpec((D, tf), lambda i, j: (0, j)), pl.BlockSpec((tf, D), lambda i, j: (j, 0))],
        out_specs=pl.BlockSpec((tm, D), lambda i, j: (i, 0)),
        out_shape=jax.ShapeDtypeStruct((M, D), F32),
        scratch_shapes=[pltpu.VMEM((tm, D), BF16)],
        compiler_params=_cparams(("parallel", "arbitrary")))(x2d, g.reshape(1, D), w1, w2)


def _memkv_kernel(m_ref, g_ref, wk_ref, gain_ref, gmat_ref, wvt_ref, k_ref, vt_ref):
    mn = _rms_rows(m_ref[0], g_ref[...]).astype(BF16)
    kk = _dot(mn, wk_ref[...])
    for c, kc in enumerate(_group_rms(kk, gmat_ref[...], gain_ref[...])):
        k_ref[0, :, c * LANES:(c + 1) * LANES] = kc.astype(BF16)
    vt_ref[0] = _dot_nt(wvt_ref[...], mn).astype(BF16)


def _memkv(mem, g, wk, kgain, wvt):
    B, M, D = mem.shape
    return pl.pallas_call(
        _memkv_kernel, name="memkv", grid=(B,),
        in_specs=[pl.BlockSpec((1, M, D), lambda b: (b, 0, 0)), _full((1, D)), _full(wk.shape),
                  _full((1, MEM_W)), _full((LANES, LANES)), _full(wvt.shape)],
        out_specs=[pl.BlockSpec((1, M, MEM_W), lambda b: (b, 0, 0)), pl.BlockSpec((1, MEM_W, M), lambda b: (b, 0, 0))],
        out_shape=[jax.ShapeDtypeStruct((B, M, MEM_W), BF16), jax.ShapeDtypeStruct((B, MEM_W, M), BF16)],
        compiler_params=_cparams(("parallel",)))(
            mem, g.reshape(1, D), wk, kgain.reshape(1, MEM_W), _group_gmat(), wvt)


def _memattn_kernel(x_ref, g_ref, wqt_ref, k_ref, vt_ref, wo_ref, o_ref, ot_ref):
    x = x_ref[0]
    tm = x.shape[0]
    hn = _rms_rows(x, g_ref[...]).astype(BF16)
    q = _dot_nt(wqt_ref[...], hn).reshape(MEM_HEADS, HEAD_DIM, tm)
    ms = jnp.mean(q * q, axis=1, keepdims=True)
    qn = (q * lax.rsqrt(ms + EPS)).astype(BF16)
    k = k_ref[0]
    vt = vt_ref[0]
    n_mem = k.shape[0]
    ones = jnp.ones((16, n_mem), BF16)
    for hp in range(MEM_HEADS // 2):
        s = _dot(k[:, hp * LANES:(hp + 1) * LANES], _blockdiag_q(qn[2 * hp], qn[2 * hp + 1]))
        p = jnp.exp2((s - jnp.max(s, axis=0, keepdims=True)).astype(BF16))
        for j in range(2):
            rows = slice((2 * hp + j) * HEAD_DIM, (2 * hp + j + 1) * HEAD_DIM)
            o = _dot(jnp.concatenate([vt[rows], ones], axis=0), p[:, j * tm:(j + 1) * tm])
            ot_ref[rows, :] = o[:HEAD_DIM] / o[HEAD_DIM:HEAD_DIM + 1]
    o_ref[0] = x + _dot_tn(ot_ref[...].astype(BF16), wo_ref[...])


def _memattn(x, kn, vt, g, wqt, wo, tm=512):
    B, L, D = x.shape
    M = kn.shape[1]
    return pl.pallas_call(
        _memattn_kernel, name="memattn", grid=(B, L // tm),
        in_specs=[pl.BlockSpec((1, tm, D), lambda b, i: (b, i, 0)), _full((1, D)), _full(wqt.shape),
                  pl.BlockSpec((1, M, MEM_W), lambda b, i: (b, 0, 0)),
                  pl.BlockSpec((1, MEM_W, M), lambda b, i: (b, 0, 0)), _full(wo.shape)],
        out_specs=pl.BlockSpec((1, tm, D), lambda b, i: (b, i, 0)),
        out_shape=jax.ShapeDtypeStruct((B, L, D), F32),
        scratch_shapes=[pltpu.VMEM((MEM_W, tm), F32)],
        compiler_params=_cparams(("parallel", "parallel")))(x, g.reshape(1, D), wqt, kn, vt, wo)


def _t5_bucket_np(rel):
    half = T5_BUCKETS // 2
    exact = half // 2
    n = np.abs(rel)
    nf = np.maximum(n, 1).astype(np.float64)
    large = exact + (np.log(nf / exact) / math.log(T5_MAX_DIST / exact) * (half - exact)).astype(np.int32)
    large = np.minimum(large, half - 1)
    return (np.where(rel > 0, half, 0) + np.where(n < exact, n, large)).astype(np.int32)


def _t5bias_kernel(table_ref, bucket_ref, mask_ref, o_ref, *, scale):
    h = pl.program_id(0)
    bucket = bucket_ref[...]
    acc = jnp.zeros(bucket.shape, F32)
    for b in range(T5_BUCKETS):
        acc = jnp.where(bucket == b, table_ref[b, h], acc)
    o_ref[0] = acc * scale + mask_ref[...]


def _t5bias(table, rel, mask=None, scale=1.0):
    bucket = jnp.asarray(_t5_bucket_np(rel))
    mask = jnp.zeros(rel.shape, F32) if mask is None else jnp.asarray(mask, F32)
    nd = rel.ndim
    return pl.pallas_call(
        functools.partial(_t5bias_kernel, scale=scale), name="t5bias", grid=(T5_HEADS,),
        in_specs=[pl.BlockSpec(memory_space=pltpu.SMEM), _full(rel.shape), _full(rel.shape)],
        out_specs=pl.BlockSpec((1,) + rel.shape, lambda h: (h,) + (0,) * nd),
        out_shape=jax.ShapeDtypeStruct((T5_HEADS,) + rel.shape, F32),
        compiler_params=_cparams(("arbitrary",)))(table, bucket, mask)


def _diffattn_kernel(lam_ref, cfar_ref, q_ref, k_ref, v_ref, bias_ref, sg_ref, o_ref, qp_ref, sa_ref, sb_ref,
                     xa_ref, xb_ref, m_ref, acc_ref, *, lam_init, T, nkv):
    h = pl.program_id(1)
    i = pl.program_id(2)
    q = q_ref[0, 0]
    row = lax.broadcasted_iota(jnp.int32, q.shape, 0)
    zero = jnp.zeros(q.shape, q.dtype)
    qp_ref[0] = jnp.where(row < HEAD_DIM, q, zero)
    qp_ref[1] = jnp.where(row >= HEAD_DIM, q, zero)
    m_ref[...] = jnp.full(m_ref.shape, NEG_INF, F32)
    acc_ref[...] = jnp.zeros(acc_ref.shape, F32)

    def scores(jj, s_ref, x_ref):
        k = k_ref[0, pl.ds(pl.multiple_of(jj * T, T), T), :]
        for m in range(2):
            s = _dot(k, qp_ref[m])
            s_ref[m] = s
            x_ref[m] = jnp.max(s, axis=0, keepdims=True)

    def step(jj, s_cur, x_cur, s_nxt, x_nxt):
        d = jj - i
        near = jnp.abs(d) <= 1

        @pl.when(near)
        def _():
            for m in range(2):
                s = s_cur[m] + bias_ref[m, d + 1]
                s_cur[m] = s
                x_cur[m] = jnp.max(s, axis=0, keepdims=True)

        scores(jnp.minimum(jj + 1, nkv - 1), s_nxt, x_nxt)
        v = jnp.concatenate([v_ref[0, jj], jnp.ones((DA_SUM_ROWS, T), BF16)], axis=0)
        for m in range(2):
            c = jnp.where(near, 0.0, jnp.where(d < 0, cfar_ref[0, 2 * h + m], cfar_ref[1, 2 * h + m]))
            m_prev = m_ref[m]
            m_new = jnp.maximum(m_prev, x_cur[m] + c)
            alpha = jnp.exp2(m_prev - m_new)
            p = jnp.exp2((s_cur[m] - (m_new - c)).astype(BF16))
            acc_ref[m] = alpha * acc_ref[m] + _dot(v, p)
            m_ref[m] = m_new

    scores(0, sa_ref, xa_ref)

    def body(t, carry):
        step(2 * t, sa_ref, xa_ref, sb_ref, xb_ref)
        step(2 * t + 1, sb_ref, xb_ref, sa_ref, xa_ref)
        return carry

    lax.fori_loop(0, nkv // 2, body, 0)

    lf = lam_ref[...]
    lam = (jnp.exp(jnp.sum(lf[0:1] * lf[1:2], axis=-1, keepdims=True))
           - jnp.exp(jnp.sum(lf[2:3] * lf[3:4], axis=-1, keepdims=True)) + lam_init)
    num = [acc_ref[m, 0:DA_VDIM, :] for m in range(2)]
    den = [acc_ref[m, DA_VDIM:DA_VDIM + 1, :] for m in range(2)]
    o = num[0] / den[0] - lam * (num[1] / den[1])
    ms = jnp.mean(o * o, axis=0, keepdims=True)
    o = o * lax.rsqrt(ms + EPS) * sg_ref[...] * (1.0 - lam_init)
    o_ref[0] = o.T.astype(o_ref.dtype)


def _diffattn(qt, k, vt, bias, cfar, lam, sub_gain, lam_init):
    B, nb, _, T = qt.shape
    L = nb * T
    assert nb % 2 == 0
    return pl.pallas_call(
        functools.partial(_diffattn_kernel, lam_init=lam_init, T=T, nkv=nb), name="diffattn",
        grid=(B, DA_HEADS, nb),
        in_specs=[_full((4, HEAD_DIM)), pl.BlockSpec(memory_space=pltpu.SMEM),
                  pl.BlockSpec((1, 1, LANES, T), lambda b, h, i: (b, i, h, 0)),
                  pl.BlockSpec((1, L, LANES), lambda b, h, i: (b, 0, h)),
                  pl.BlockSpec((1, nb, LANES, T), lambda b, h, i: (b, 0, h, 0)),
                  pl.BlockSpec((2, 3, T, T), lambda b, h, i: (h, 0, 0, 0)),
                  _full((DA_VDIM, 1))],
        out_specs=pl.BlockSpec((1, T, LANES), lambda b, h, i: (b, i, h)),
        out_shape=jax.ShapeDtypeStruct((B, L, DA_V), BF16),
        scratch_shapes=[pltpu.VMEM((2, LANES, T), BF16), pltpu.VMEM((2, T, T), F32), pltpu.VMEM((2, T, T), F32),
                        pltpu.VMEM((2, 1, T), F32), pltpu.VMEM((2, 1, T), F32),
                        pltpu.VMEM((2, 1, T), F32), pltpu.VMEM((2, DA_VDIM + DA_SUM_ROWS, T), F32)],
        compiler_params=_cparams(("parallel", "parallel", "arbitrary")))(
            lam, cfar, qt, k, vt, bias, sub_gain.reshape(DA_VDIM, 1))


def _dot_tn(a, b):
    return lax.dot_general(a, b, (((0,), (0,)), ((), ())), preferred_element_type=F32)


def _blockdiag_q(qa, qb):
    z = jnp.zeros(qa.shape, qa.dtype)
    return jnp.concatenate([jnp.concatenate([qa, z], axis=0), jnp.concatenate([z, qb], axis=0)], axis=1)


WG_SUB = 4


def _wgqa_kernel(sink_ref, q_ref, kp_ref, kc_ref, kn_ref, vp_ref, vc_ref, vn_ref, bias_ref, o_ref, kw_ref, vw_ref,
                 *, nb):
    j = pl.program_id(1)
    W = WG_BLOCK
    kw_ref[0:W] = kp_ref[0]
    kw_ref[W:(WG_SUB + 1) * W] = kc_ref[0]
    kw_ref[(WG_SUB + 1) * W:(WG_SUB + 2) * W] = kn_ref[0]
    vw_ref[0:W] = vp_ref[0]
    vw_ref[W:(WG_SUB + 1) * W] = vc_ref[0]
    vw_ref[(WG_SUB + 1) * W:(WG_SUB + 2) * W] = vn_ref[0]
    key = lax.broadcasted_iota(jnp.int32, (3 * W, 2 * W), 0)
    lane2 = lax.broadcasted_iota(jnp.int32, (1, 2 * W), 1)
    lane1 = lax.broadcasted_iota(jnp.int32, (W, LANES), 1)
    grp = WG_HEADS // WG_KV_HEADS
    sink_rows = 16
    ones = jnp.ones((3 * W + sink_rows, LANES), BF16)
    first = lax.broadcasted_iota(jnp.int32, (sink_rows, 2 * W), 0) == 0
    for sub in range(WG_SUB):
        n = j * WG_SUB + sub
        valid = jnp.logical_and(jnp.logical_or(n > 0, key >= W), jnp.logical_or(n < nb - 1, key < 2 * W))
        kw = kw_ref[sub * W:(sub + 3) * W, :]
        vw = jnp.concatenate([jnp.concatenate([vw_ref[sub * W:(sub + 3) * W, :], jnp.zeros((sink_rows, LANES), BF16)],
                                              axis=0), ones], axis=1)
        qs = slice(sub * W, (sub + 1) * W)
        for h in range(grp):
            qa = q_ref[0, h * HEAD_DIM:(h + 1) * HEAD_DIM, qs]
            qb = q_ref[0, (h + grp) * HEAD_DIM:(h + grp + 1) * HEAD_DIM, qs]
            s = _dot(kw, _blockdiag_q(qa, qb))
            s = s + jnp.concatenate([bias_ref[h], bias_ref[h + grp]], axis=1)
            s = jnp.where(valid, s, NEG_INF)
            sk = jnp.where(lane2 < W, sink_ref[h], sink_ref[h + grp]) * LOG2E
            mx = jnp.maximum(jnp.max(s, axis=0, keepdims=True), sk)
            p = jnp.exp2((s - mx).astype(BF16))
            p_sink = jnp.where(first, jnp.exp2(sk - mx), 0.0).astype(BF16)
            o2 = _dot_tn(jnp.concatenate([p, p_sink], axis=0), vw)
            o2 = o2[:, :LANES] / o2[:, LANES:]
            o_ref[0, qs, h * LANES:(h + 1) * LANES] = jnp.where(lane1 < HEAD_DIM, o2[:W], o2[W:]).astype(o_ref.dtype)


def _wgqa(qt, kk, vv, bias, sink, q_row, k_col, v_col):
    B, L, _ = kk.shape
    W = WG_BLOCK
    nb = L // W
    S = WG_SUB
    assert nb % S == 0
    prev = lambda j: jnp.maximum(j * S - 1, 0)
    nxt = lambda j: jnp.minimum(j * S + S, nb - 1)
    return pl.pallas_call(
        functools.partial(_wgqa_kernel, nb=nb), name="wgqa", grid=(B, nb // S),
        in_specs=[pl.BlockSpec(memory_space=pltpu.SMEM),
                  pl.BlockSpec((1, WG_Q, S * W), lambda b, j: (b, q_row, j)),
                  pl.BlockSpec((1, W, LANES), lambda b, j: (b, prev(j), k_col)),
                  pl.BlockSpec((1, S * W, LANES), lambda b, j: (b, j, k_col)),
                  pl.BlockSpec((1, W, LANES), lambda b, j: (b, nxt(j), k_col)),
                  pl.BlockSpec((1, W, LANES), lambda b, j: (b, prev(j), v_col)),
                  pl.BlockSpec((1, S * W, LANES), lambda b, j: (b, j, v_col)),
                  pl.BlockSpec((1, W, LANES), lambda b, j: (b, nxt(j), v_col)),
                  _full(bias.shape)],
        out_specs=pl.BlockSpec((1, S * W, WG_Q), lambda b, j: (b, j, 0)),
        out_shape=jax.ShapeDtypeStruct((B, L, WG_Q), BF16),
        scratch_shapes=[pltpu.VMEM(((S + 2) * W, LANES), BF16), pltpu.VMEM(((S + 2) * W, LANES), BF16)],
        compiler_params=_cparams(("parallel", "arbitrary")))(sink, qt, kk, kk, kk, vv, vv, vv, bias)


NA_KROWS = NA_ROWS + 1
NA_CASES = 5
_NA_CASE_GEOM = (((0, 0), 7), ((0, 0), 5), ((0, 1), 3), ((1, 1), 2), ((1, 1), 0))


def _nabias_kernel(rpb_ref, o_ref):
    h = pl.program_id(0)
    cc = lax.broadcasted_iota(jnp.int32, (GRID_W, LANES), 0)
    lane = lax.broadcasted_iota(jnp.int32, (GRID_W, LANES), 1)
    c = lane % GRID_W
    second = lane >= GRID_W
    c_start = jnp.clip(c - NA_COLS // 2, 0, GRID_W - NA_COLS)
    valid = jnp.logical_and(cc >= c_start, cc < c_start + NA_COLS)
    dc = cc - c + (NA_COLS - 1)
    neg = jnp.full((GRID_W, LANES), NEG_INF, F32)
    tiles = {}
    for dr in range(1, 2 * NA_ROWS - 1):
        acc = neg
        for d in range(2 * NA_COLS - 1):
            val = jnp.where(second, rpb_ref[h, dr - 1, d], rpb_ref[h, dr, d]) * LOG2E
            acc = jnp.where(jnp.logical_and(valid, dc == d), val, acc)
        tiles[dr] = acc
    for case, (wstart, a) in enumerate(_NA_CASE_GEOM):
        for i in range(NA_KROWS):
            in0 = wstart[0] <= i < wstart[0] + NA_ROWS
            in1 = wstart[1] <= i < wstart[1] + NA_ROWS
            t = tiles[i + a] if (in0 or in1) else neg
            if in1 and not in0:
                t = jnp.where(second, t, neg)
            if in0 and not in1:
                t = jnp.where(second, neg, t)
            o_ref[case, 0, i * GRID_W:(i + 1) * GRID_W, :] = t


def _nabias(rpb):
    return pl.pallas_call(
        _nabias_kernel, name="nabias", grid=(NA_HEADS,),
        in_specs=[pl.BlockSpec(memory_space=pltpu.SMEM)],
        out_specs=pl.BlockSpec((NA_CASES, 1, NA_KROWS * GRID_W, LANES), lambda h: (0, h, 0, 0)),
        out_shape=jax.ShapeDtypeStruct((NA_CASES, NA_HEADS, NA_KROWS * GRID_W, LANES), F32),
        compiler_params=_cparams(("arbitrary",)))(rpb)


def _natten_kernel(q_ref, kp_ref, kc_ref, kn_ref, vp_ref, vc_ref, vn_ref, bias_ref, o_ref, kw_ref, vw_ref,
                   sa_ref, sb_ref, *, rows):
    i = pl.program_id(1)
    RB = NA_ROWS
    T = RB * GRID_W
    KW = NA_KROWS * GRID_W
    kw_ref[0:T] = kp_ref[0]
    kw_ref[T:2 * T] = kc_ref[0]
    kw_ref[2 * T:3 * T] = kn_ref[0]
    vw_ref[0:T] = vp_ref[0]
    vw_ref[T:2 * T] = vc_ref[0]
    vw_ref[2 * T:3 * T] = vn_ref[0]
    lane1 = lax.broadcasted_iota(jnp.int32, (LANES, LANES), 1)
    ones = jnp.ones((KW, LANES), BF16)
    units = [(pi, hp) for pi in range(RB // 2) for hp in range(NA_HEADS // 2)]

    def window(pi):
        r = i * RB + 2 * pi
        r_lo = jnp.clip(r - NA_ROWS // 2, 0, rows - NA_KROWS)
        off = pl.multiple_of((r_lo - i * RB + RB) * GRID_W, GRID_W)
        case = jnp.where(r == 0, 0, jnp.where(r == 2, 1, jnp.where(r == rows - 4, 3, jnp.where(r == rows - 2, 4, 2))))
        return off, case

    def scores(unit, s_ref):
        pi, hp = unit
        off, case = window(pi)
        ha, hb = 2 * hp, 2 * hp + 1
        qs = slice(pi * LANES, (pi + 1) * LANES)
        kw = kw_ref[pl.ds(off, KW), hp * LANES:(hp + 1) * LANES]
        qa = q_ref[0, ha * HEAD_DIM:(ha + 1) * HEAD_DIM, qs]
        qb = q_ref[0, hb * HEAD_DIM:(hb + 1) * HEAD_DIM, qs]
        s = _dot(kw, _blockdiag_q(qa, qb))
        s_ref[...] = s + jnp.concatenate([bias_ref[case, ha], bias_ref[case, hb]], axis=1)

    def finish(unit, s_ref):
        pi, hp = unit
        off, _ = window(pi)
        ls = slice(hp * LANES, (hp + 1) * LANES)
        s = s_ref[...]
        p = jnp.exp2((s - jnp.max(s, axis=0, keepdims=True)).astype(BF16))
        vw = jnp.concatenate([vw_ref[pl.ds(off, KW), ls], ones], axis=1)
        o2 = _dot_tn(p, vw)
        o2 = o2[:, :LANES] / o2[:, LANES:]
        o_ref[0, pi * LANES:(pi + 1) * LANES, ls] = jnp.where(lane1 < HEAD_DIM, o2[:LANES], o2[LANES:]).astype(
            o_ref.dtype)

    slots = (sa_ref, sb_ref)
    scores(units[0], slots[0])
    for u, unit in enumerate(units):
        if u + 1 < len(units):
            scores(units[u + 1], slots[(u + 1) % 2])
        finish(unit, slots[u % 2])


def _natten(qt, kk, vv, bias):
    B, L, _ = kk.shape
    rows = L // GRID_W
    assert rows >= NA_KROWS + 1 and rows % NA_ROWS == 0
    T = NA_ROWS * GRID_W
    nb = L // T
    prev = lambda n: jnp.maximum(n - 1, 0)
    nxt = lambda n: jnp.minimum(n + 1, nb - 1)
    return pl.pallas_call(
        functools.partial(_natten_kernel, rows=rows), name="natten", grid=(B, nb),
        in_specs=[pl.BlockSpec((1, NA_W, T), lambda b, n: (b, 0, n)),
                  pl.BlockSpec((1, T, NA_W), lambda b, n: (b, prev(n), 0)),
                  pl.BlockSpec((1, T, NA_W), lambda b, n: (b, n, 0)),
                  pl.BlockSpec((1, T, NA_W), lambda b, n: (b, nxt(n), 0)),
                  pl.BlockSpec((1, T, NA_W), lambda b, n: (b, prev(n), 0)),
                  pl.BlockSpec((1, T, NA_W), lambda b, n: (b, n, 0)),
                  pl.BlockSpec((1, T, NA_W), lambda b, n: (b, nxt(n), 0)),
                  _full(bias.shape)],
        out_specs=pl.BlockSpec((1, T, NA_W), lambda b, n: (b, n, 0)),
        out_shape=jax.ShapeDtypeStruct((B, L, NA_W), BF16),
        scratch_shapes=[pltpu.VMEM((3 * T, NA_W), BF16), pltpu.VMEM((3 * T, NA_W), BF16)]
                       + [pltpu.VMEM((NA_KROWS * GRID_W, 2 * LANES), F32)] * 2,
        compiler_params=_cparams(("parallel", "arbitrary")))(qt, kk, kk, kk, vv, vv, vv, bias)


def _np_bf16(a):
    return np.asarray(a, np.float32).astype(BF16)


def _fft_consts(L):
    N2 = FFT_LANES
    N = 2 * L
    N1 = N // N2
    N1h = N1 // 2
    G = max(1, LANES // N1h)
    Gf = max(1, LANES // N1)
    k = np.arange(N1)
    ang1 = 2.0 * np.pi * np.outer(k, k) / N1
    C1, S1 = np.cos(ang1), np.sin(ang1)
    eye = np.eye
    KC, KS = np.kron(eye(G), C1[:, :N1h]), np.kron(eye(G), S1[:, :N1h])
    fa = np.block([[KC, KS], [-KS, KC]])
    KCh, KSh = np.kron(eye(G), C1[:N1h, :]), np.kron(eye(G), S1[:N1h, :])
    fg = np.block([[KCh, -KSh], [KSh, KCh]]) / N
    faf = np.concatenate([np.kron(eye(Gf), C1), -np.kron(eye(Gf), S1)], axis=0)
    n2 = np.arange(N2)
    angt = 2.0 * np.pi * np.outer(k, n2) / N
    twr, twi = np.cos(angt), -np.sin(angt)
    ang2 = 2.0 * np.pi * np.outer(n2, n2) / N2
    C2, S2 = np.cos(ang2), np.sin(ang2)
    m2f = np.block([[C2, -S2], [S2, C2]])
    m2i = np.block([[C2, S2], [-S2, C2]])
    return dict(N=N, N1=N1, N1h=N1h, N2=N2, G=G, Gf=Gf, fa=_np_bf16(fa), fg=_np_bf16(fg), faf=_np_bf16(faf),
                m2f=_np_bf16(m2f), m2i=_np_bf16(m2i),
                twr=np.asarray(twr, np.float32), twi=np.asarray(twi, np.float32))


def _hyfilt_kernel(w1t_ref, b1_ref, fr_ref, w2t_ref, b2_ref, w3t_ref, delta_ref, band_ref, o_ref, sum_ref, *, L, tn):
    s = pl.program_id(0)
    n = s * tn + lax.broadcasted_iota(jnp.int32, (1, tn), 1)
    t = jnp.where(n < L, n, 2 * L - n).astype(F32)
    t01 = t / float(max(L - 1, 1))
    w = (2.0 * math.pi) * t / float(L)
    ang = band_ref[...] * w
    cs, sn = jnp.cos(ang), -jnp.sin(ang)
    w1t = w1t_ref[...]
    h = w1t[:, 0:1] * t01
    for b in range(HY_BANDS):
        h = h + w1t[:, 1 + b:2 + b] * cs[b:b + 1] + w1t[:, 1 + HY_BANDS + b:2 + HY_BANDS + b] * sn[b:b + 1]
    fr = fr_ref[...]
    h = jnp.sin(fr[:, 0:1] * (h + b1_ref[...]))
    h = jnp.sin(fr[:, 1:2] * (jnp.dot(w2t_ref[...], h, preferred_element_type=F32,
                                      precision=lax.Precision.HIGHEST) + b2_ref[...]))
    y = jnp.dot(w3t_ref[0], h, preferred_element_type=F32, precision=lax.Precision.HIGHEST)
    y = y * jnp.exp(-delta_ref[...] * t01)
    y = jnp.where(n == L, 0.0, y)
    o_ref[...] = y
    a = jnp.abs(y)
    part = a[:, 0:LANES]
    for c in range(1, tn // LANES):
        part = part + a[:, c * LANES:(c + 1) * LANES]

    @pl.when(s == 0)
    def _():
        sum_ref[...] = jnp.zeros(sum_ref.shape, F32)

    sum_ref[...] += part


def _hyena_filters_td(L, w1, b1, freq, w2, b2, w3, tn=1024):
    OC = HY_ORDER * HY_CH
    w3t = jnp.transpose(w3.reshape(HY_HIDDEN, HY_ORDER, 2, HY_CH), (2, 1, 3, 0)).reshape(2, OC, HY_HIDDEN)
    min_decay = math.log(HY_TARGET) / HY_SLOW_DECAY
    max_decay = math.log(HY_TARGET) / HY_FAST_DECAY
    deltas = np.abs(np.linspace(min_decay, max_decay, HY_CH, dtype=np.float32))
    delta = jnp.asarray(np.tile(deltas, HY_ORDER).reshape(OC, 1))
    bands = jnp.asarray(np.linspace(1e-4, HY_BANDS - 1, HY_BANDS, dtype=np.float32).reshape(HY_BANDS, 1))
    N = 2 * L
    nh = L // tn
    return pl.pallas_call(
        functools.partial(_hyfilt_kernel, L=L, tn=tn), name="hyfilt", grid=(N // tn,),
        in_specs=[_full((HY_HIDDEN, 1 + 2 * HY_BANDS)), _full((HY_HIDDEN, 1)), _full((HY_HIDDEN, 2)),
                  _full((HY_HIDDEN, HY_HIDDEN)), _full((HY_HIDDEN, 1)),
                  pl.BlockSpec((1, OC, HY_HIDDEN), lambda s: (s // nh, 0, 0)),
                  _full((OC, 1)), _full((HY_BANDS, 1))],
        out_specs=[pl.BlockSpec((OC, tn), lambda s: (0, s)), _full((OC, LANES))],
        out_shape=[jax.ShapeDtypeStruct((OC, N), F32), jax.ShapeDtypeStruct((OC, LANES), F32)],
        compiler_params=_cparams(("arbitrary",)))(
            w1.T, b1.reshape(HY_HIDDEN, 1), freq.T, w2.T, b2.reshape(HY_HIDDEN, 1), w3t, delta, bands)


def _hyspec_kernel(k_ref, sum_ref, fa_ref, twr_ref, twi_ref, m2_ref, kr_ref, ki_ref, p_ref, *, R, N1, N2, Gf):
    tot = jnp.sum(sum_ref[...], axis=-1, keepdims=True)
    mi = Gf * N1
    for g in range(R // Gf):
        xs = []
        for r in range(Gf):
            row = g * Gf + r
            xs.append(k_ref[row] / tot[row:row + 1, :])
        x = xs[0] if Gf == 1 else jnp.concatenate(xs, axis=0)
        a = _dot(fa_ref[...], x.astype(BF16))
        ar, ai = a[:mi], a[mi:]
        twr, twi = twr_ref[...], twi_ref[...]
        p_ref[g * mi:(g + 1) * mi, 0:N2] = (ar * twr - ai * twi).astype(BF16)
        p_ref[g * mi:(g + 1) * mi, N2:2 * N2] = (ar * twi + ai * twr).astype(BF16)
    b = _dot(p_ref[...], m2_ref[...])
    kr_ref[...] = b[:, :N2].reshape(R, N1, N2)
    ki_ref[...] = b[:, N2:].reshape(R, N1, N2)


def _hyena_spectra(ktd, ksum, fc):
    OC = ktd.shape[0]
    N1, N2, Gf = fc["N1"], fc["N2"], fc["Gf"]
    R = 1024 // N1
    twr = jnp.asarray(np.tile(fc["twr"], (Gf, 1)))
    twi = jnp.asarray(np.tile(fc["twi"], (Gf, 1)))
    fa, m2 = fc["faf"], fc["m2f"]
    return pl.pallas_call(
        functools.partial(_hyspec_kernel, R=R, N1=N1, N2=N2, Gf=Gf), name="hyspec", grid=(OC // R,),
        in_specs=[pl.BlockSpec((R, N1, N2), lambda i: (i, 0, 0)), pl.BlockSpec((R, LANES), lambda i: (i, 0)),
                  _full(fa.shape), _full(twr.shape), _full(twi.shape), _full(m2.shape)],
        out_specs=[pl.BlockSpec((R, N1, N2), lambda i: (i, 0, 0))] * 2,
        out_shape=[jax.ShapeDtypeStruct((OC, N1, N2), F32)] * 2,
        scratch_shapes=[pltpu.VMEM((R * N1, 2 * N2), BF16)],
        compiler_params=_cparams(("parallel",)))(ktd.reshape(OC, N1, N2), ksum, fa, twr, twi, m2)


def _hyconv_kernel(cw_ref, cb_ref, sk_ref, u_ref, kr_ref, ki_ref, fa_ref, fg_ref, twr_ref, twi_ref, mf_ref, mi_ref,
                   o_ref, z_ref, x1_ref, x2_ref, c_ref, p_ref, wcol_ref, *, R, N1, N2, G):
    N1h = N1 // 2
    cb = pl.program_id(0)
    ng = R // G
    mi = G * N1h
    mo = G * N1
    n_parts = 3
    i_bias = 3 * n_parts
    i_skip = i_bias + n_parts

    @pl.when(pl.program_id(1) == 0)
    def _():
        def fill(r, carry):
            ch = cb * R + r
            rs = pl.ds(pl.multiple_of(r * N1h, 8), N1h)
            for part in range(n_parts):
                c = part * HY_CH + ch
                for tap in range(3):
                    wcol_ref[tap * n_parts + part, rs, :] = jnp.full((N1h, LANES), cw_ref[tap, c], F32)
                wcol_ref[i_bias + part, rs, :] = jnp.full((N1h, LANES), cb_ref[c], F32)
            for o in range(HY_ORDER):
                wcol_ref[i_skip + o, rs, :] = jnp.full((N1h, LANES), sk_ref[o, ch], F32)
            return carry
        lax.fori_loop(0, R, fill, 0)

    def wide(idx, g):
        w = wcol_ref[idx, g * mi:(g + 1) * mi, :]
        return jnp.concatenate([w] * (N2 // LANES), axis=1)

    lane = lax.broadcasted_iota(jnp.int32, (mi, N2), 1)
    sub = lax.broadcasted_iota(jnp.int32, (mi, N2), 0) % N1h
    first = jnp.logical_and(lane == 0, sub == 0)
    last = jnp.logical_and(lane == N2 - 1, sub == N1h - 1)

    def shortconv(u, part, g):
        rl = pltpu.roll(u, 1, 1)
        prev = jnp.where(lane == 0, pltpu.roll(rl, 1, 0), rl)
        prev = jnp.where(first, 0.0, prev)
        rr = pltpu.roll(u, N2 - 1, 1)
        nxt = jnp.where(lane == N2 - 1, pltpu.roll(rr, mi - 1, 0), rr)
        nxt = jnp.where(last, 0.0, nxt)
        return (prev * wide(part, g) + u * wide(n_parts + part, g) + nxt * wide(2 * n_parts + part, g)
                + wide(i_bias + part, g))

    for g in range(ng):
        for pbatch in range(2):
            rs = slice((g * 2 + pbatch) * mi, (g * 2 + pbatch + 1) * mi)
            for part, dst in enumerate((z_ref, x1_ref, x2_ref)):
                u = u_ref[0, pbatch, part, g * G:(g + 1) * G].reshape(mi, N2)
                dst[rs, :] = shortconv(u, part, g)

    def conv(order):
        twr, twi = twr_ref[...], twi_ref[...]
        for g in range(ng):
            a = _dot(fa_ref[...], z_ref[g * 2 * mi:(g + 1) * 2 * mi, :].astype(BF16))
            ar, ai = a[:mo], a[mo:]
            p_ref[g * mo:(g + 1) * mo, 0:N2] = (ar * twr - ai * twi).astype(BF16)
            p_ref[g * mo:(g + 1) * mo, N2:2 * N2] = (ar * twi + ai * twr).astype(BF16)
        b = _dot(p_ref[...], mf_ref[...])
        br, bi = b[:, :N2], b[:, N2:]
        kr = kr_ref[order].reshape(R * N1, N2)
        ki = ki_ref[order].reshape(R * N1, N2)
        p_ref[:, 0:N2] = (br * kr - bi * ki).astype(BF16)
        p_ref[:, N2:2 * N2] = (br * ki + bi * kr).astype(BF16)
        d = _dot(p_ref[...], mi_ref[...])
        dr, di = d[:, :N2], d[:, N2:]
        for g in range(ng):
            sl = slice(g * mo, (g + 1) * mo)
            x = jnp.concatenate([dr[sl] * twr + di[sl] * twi, di[sl] * twr - dr[sl] * twi], axis=0)
            c_ref[g * 2 * mi:(g + 1) * 2 * mi, :] = _dot(fg_ref[...], x.astype(BF16))

    def gate(order, x_ref, final):
        for g in range(ng):
            skv = wide(i_skip + order, g)
            for pbatch in range(2):
                rs = slice((g * 2 + pbatch) * mi, (g * 2 + pbatch + 1) * mi)
                zn = x_ref[rs, :] * (c_ref[rs, :] + skv * z_ref[rs, :])
                if final:
                    o_ref[0, pbatch, g * G:(g + 1) * G] = zn.reshape(G, N1h, N2)
                else:
                    z_ref[rs, :] = zn

    conv(0)
    gate(0, x1_ref, False)
    conv(1)
    gate(1, x2_ref, True)


def _hyena_conv(ucm, conv_w, conv_b, skip, kr, ki, fc):
    B = ucm.shape[0]
    C = HY_CH
    N1, N2, G = fc["N1"], fc["N2"], fc["G"]
    N1h = N1 // 2
    R = 1024 // N1
    u = ucm.reshape(B // 2, 2, 3, C, N1h, N2)
    twr = jnp.asarray(np.tile(fc["twr"], (G, 1)))
    twi = jnp.asarray(np.tile(fc["twi"], (G, 1)))
    consts = [fc["fa"], fc["fg"], twr, twi, fc["m2f"], fc["m2i"]]
    rows = R * N1h
    smem = pl.BlockSpec(memory_space=pltpu.SMEM)
    out = pl.pallas_call(
        functools.partial(_hyconv_kernel, R=R, N1=N1, N2=N2, G=G), name="hyconv",
        grid=(C // R, B // 2),
        in_specs=[smem, smem, smem,
                  pl.BlockSpec((1, 2, 3, R, N1h, N2), lambda c, p: (p, 0, 0, c, 0, 0)),
                  pl.BlockSpec((HY_ORDER, R, N1, N2), lambda c, p: (0, c, 0, 0)),
                  pl.BlockSpec((HY_ORDER, R, N1, N2), lambda c, p: (0, c, 0, 0))]
                 + [_full(a.shape) for a in consts],
        out_specs=pl.BlockSpec((1, 2, R, N1h, N2), lambda c, p: (p, 0, c, 0, 0)),
        out_shape=jax.ShapeDtypeStruct((B // 2, 2, C, N1h, N2), F32),
        scratch_shapes=[pltpu.VMEM((2 * rows, N2), F32)] * 4
                       + [pltpu.VMEM((R * N1, 2 * N2), BF16), pltpu.VMEM((4 * 3 + HY_ORDER, rows, LANES), F32)],
        compiler_params=_cparams(("parallel", "arbitrary")))(
            conv_w, conv_b, skip, u, kr.reshape(HY_ORDER, C, N1, N2), ki.reshape(HY_ORDER, C, N1, N2), *consts)
    return out.reshape(B, C, N1h * N2)


def _lambda_init(layer):
    return 0.8 - 0.6 * math.exp(-0.3 * layer)


def _tile_gain(g, reps, scale=1.0):
    return jnp.tile(g.astype(F32), reps) * scale


def _trunk(x, mem, p, shared):
    B, L, D = x.shape
    M = B * L
    scale = HEAD_DIM ** -0.5
    x2 = x.reshape(M, D)
    for layer in range(DEPTH):
        j = layer // 2
        if layer % 2 == 0:
            w_in = p["w_in_even"][j].astype(BF16)
            k_gain = _tile_gain(p["da_q_gain"][j] * p["da_k_gain"][j], DA_QK // HEAD_DIM, scale * LOG2E)
            w_cm = jnp.concatenate([w_in[:, :DA_QK], w_in[:, 2 * DA_QK:]], axis=1).T
            kk, qt, vt, ucm = _inproj_even(x2, p["norm_mix"][layer], w_in[:, DA_QK:2 * DA_QK], k_gain, w_cm,
                                           batch=B, tm=shared["da_T"])
            oa = _diffattn(qt, kk.reshape(B, L, DA_QK), vt, shared["da_bias"], shared["da_cfar"],
                           p["da_lambda"][j], p["da_sub_gain"][j], _lambda_init(layer))
            fc = _fft_consts(L)
            ktd, ksum = _hyena_filters_td(L, p["hy_w1"][j], p["hy_b1"][j], p["hy_freq"][j], p["hy_w2"][j],
                                          p["hy_b2"][j], p["hy_w3"][j])
            kr, ki = _hyena_spectra(ktd, ksum, fc)
            ob = _hyena_conv(ucm, p["hy_conv_w"][j], p["hy_conv_b"][j], p["hy_skip"][j], kr, ki, fc)
            w_out = p["w_out_even"][j].astype(BF16)
            x2 = _outproj(x2, oa.reshape(M, DA_V), ob, w_out[:DA_V], w_out[DA_V:], b_cm=True, batch=B)
        else:
            w_in = p["w_in_odd"][j].astype(BF16)
            c0 = 3 * NA_W
            c1 = c0 + WG_Q
            c2 = c1 + WG_KV
            w_k = jnp.concatenate([w_in[:, NA_W:2 * NA_W], w_in[:, c1:c2]], axis=1)
            w_v = jnp.concatenate([w_in[:, 2 * NA_W:c0], w_in[:, c2:]], axis=1)
            w_q = jnp.concatenate([w_in[:, :NA_W], w_in[:, c0:c1]], axis=1).T
            k_gain = jnp.concatenate([
                _tile_gain(p["na_q_gain"][j] * p["na_k_gain"][j], NA_HEADS, scale * LOG2E),
                _tile_gain(p["wg_q_gain"][j] * p["wg_k_gain"][j], WG_KV_HEADS, scale * LOG2E)])
            kk, vv, qt = _inproj_odd(x2, p["norm_mix"][layer], w_k, k_gain, w_v, w_q, batch=B)
            kk = kk.reshape(B, L, -1)
            vv = vv.reshape(B, L, -1)
            oc = _natten(qt, kk, vv, shared["na_bias"][j])
            od = _wgqa(qt, kk, vv, shared["wg_bias"], p["wg_sink"][j], q_row=NA_W // WG_Q,
                       k_col=NA_W // LANES, v_col=NA_W // LANES)
            w_out = p["w_out_odd"][j].astype(BF16)
            grp = WG_HEADS // WG_KV_HEADS
            head_order = np.stack([np.arange(grp), np.arange(grp) + grp], axis=1).reshape(-1)
            wd_rows = NA_W + (head_order[:, None] * HEAD_DIM + np.arange(HEAD_DIM)[None, :]).reshape(-1)
            x2 = _outproj(x2, oc.reshape(M, NA_W), od.reshape(M, WG_Q), w_out[:NA_W], w_out[wd_rows], b_cm=False)
        wkv = p["mem_wkv"][layer].astype(BF16)
        kn, vt = _memkv(mem, p["norm_memkv"][layer], wkv[:, :MEM_W],
                        _tile_gain(p["mem_q_gain"][layer] * p["mem_k_gain"][layer], MEM_HEADS, scale * LOG2E),
                        wkv[:, MEM_W:].T)
        x3 = _memattn(x2.reshape(B, L, D), kn, vt, p["norm_mem"][layer], p["mem_wq"][layer].astype(BF16).T,
                      p["mem_wo"][layer].astype(BF16))
        x2 = _mlp(x3.reshape(M, D), p["norm_mlp"][layer], p["mlp_w1"][layer].astype(BF16),
                  p["mlp_w2"][layer].astype(BF16))
    return x2.reshape(B, L, D)


def _shared_tables(p, da_T):
    idx = np.arange(da_T)
    far = np.arange(T5_MAX_DIST, 1 << 20)
    assert da_T >= T5_MAX_DIST and (_t5_bucket_np(far) == _t5_bucket_np(far[:1])).all()
    assert (_t5_bucket_np(-far) == _t5_bucket_np(-far[:1])).all()
    rel_da = np.stack([d * da_T + idx[:, None] - idx[None, :] for d in (-1, 0, 1)])
    qi = np.arange(WG_BLOCK)
    ki = np.arange(3 * WG_BLOCK)
    rel_wg = ki[:, None] - WG_BLOCK - qi[None, :]
    mask_wg = np.where(np.abs(rel_wg) <= WG_WINDOW, 0.0, NEG_INF).astype(np.float32)
    return dict(da_T=da_T,
                da_bias=_t5bias(p["t5_table"], rel_da, scale=LOG2E),
                da_cfar=p["t5_table"][_t5_bucket_np(np.array([-T5_MAX_DIST, T5_MAX_DIST]))] * LOG2E,
                wg_bias=_t5bias(p["t5_table"], rel_wg, mask_wg, scale=LOG2E),
                na_bias=[_nabias(p["na_rpb"][j]) for j in range(p["na_rpb"].shape[0])])


def kernel(x_prompt, x_sample, mem_prompt, mem_sample, t5_table, norm_mix, norm_mem, norm_memkv, norm_mlp, w_in_even, da_q_gain, da_k_gain, da_lambda, da_sub_gain, hy_conv_w, hy_conv_b, hy_w1, hy_b1, hy_freq, hy_w2, hy_b2, hy_w3, hy_skip, w_out_even, w_in_odd, na_q_gain, na_k_gain, na_rpb, wg_q_gain, wg_k_gain, wg_sink, w_out_odd, mem_wq, mem_wkv, mem_wo, mem_q_gain, mem_k_gain, mlp_w1, mlp_w2):
    p = dict(t5_table=t5_table, norm_mix=norm_mix, norm_mem=norm_mem, norm_memkv=norm_memkv, norm_mlp=norm_mlp,
             w_in_even=w_in_even, da_q_gain=da_q_gain, da_k_gain=da_k_gain, da_lambda=da_lambda,
             da_sub_gain=da_sub_gain, hy_conv_w=hy_conv_w, hy_conv_b=hy_conv_b, hy_w1=hy_w1, hy_b1=hy_b1,
             hy_freq=hy_freq, hy_w2=hy_w2, hy_b2=hy_b2, hy_w3=hy_w3, hy_skip=hy_skip, w_out_even=w_out_even,
             w_in_odd=w_in_odd, na_q_gain=na_q_gain, na_k_gain=na_k_gain, na_rpb=na_rpb, wg_q_gain=wg_q_gain,
             wg_k_gain=wg_k_gain, wg_sink=wg_sink, w_out_odd=w_out_odd, mem_wq=mem_wq, mem_wkv=mem_wkv,
             mem_wo=mem_wo, mem_q_gain=mem_q_gain, mem_k_gain=mem_k_gain, mlp_w1=mlp_w1, mlp_w2=mlp_w2)
    shared = _shared_tables(p, da_T=512)
    return (_trunk(x_prompt, mem_prompt, p, shared), _trunk(x_sample, mem_sample, p, shared))
```
